```python
import math
import jax, jax.numpy as jnp
from jax import lax
import numpy as np

D_MODEL = 1024
BATCH = 8
SEQ = 8192
DEPTH = 4

N_MIXERS = 2
N_HEADS = 16
HEAD_DIM = D_MODEL // N_HEADS
BLOCK_Q = 128
POOL_WINDOWS = (2, 4, 8, 16)
N_POOL_GROUPS = len(POOL_WINDOWS)
POOL_GROUP = D_MODEL // N_POOL_GROUPS
D_FF = 2816
PLE_DIM = 256
EPS = 1e-6

kernel_name = "hybrid_stickbreak_pool_macaron"


def rms_norm(x, g):
    xf = x.astype(jnp.float32)
    y = xf * lax.rsqrt(jnp.mean(xf * xf, axis=-1, keepdims=True) + EPS)
    return (y * g.astype(jnp.float32)).astype(x.dtype)


def swiglu(h, w_gu, w_down):
    gate, up = jnp.split(h @ w_gu, 2, axis=-1)
    return (jax.nn.silu(gate) * up) @ w_down


def stick_breaking_attention(h, w_qkv, q_gain, k_gain, w_o):
    B, S, _ = h.shape
    q, k, v = jnp.split(h @ w_qkv, 3, axis=-1)
    q = rms_norm(q.reshape(B, S, N_HEADS, HEAD_DIM), q_gain)
    k = rms_norm(k.reshape(B, S, N_HEADS, HEAD_DIM), k_gain)
    v = v.reshape(B, S, N_HEADS, HEAD_DIM)
    q, k, v = (t.transpose(0, 2, 1, 3) for t in (q, k, v))
    scale = 1.0 / math.sqrt(HEAD_DIM)
    outs = []
    for blk in range(S // BLOCK_Q):
        t0 = blk * BLOCK_Q
        t1 = t0 + BLOCK_Q
        qb = q[:, :, t0:t1]
        kb = k[:, :, :t1]
        vb = v[:, :, :t1]
        z = jnp.einsum('bhqd,bhkd->bhqk', qb, kb).astype(jnp.float32) * scale
        qpos = t0 + jnp.arange(BLOCK_Q)
        kpos = jnp.arange(t1)
        causal = kpos[None, :] < qpos[:, None]
        log_stay = jnp.where(causal, jax.nn.log_sigmoid(-z), 0.0)
        log_after = lax.cumsum(log_stay, axis=3, reverse=True) - log_stay
        weights = jnp.where(causal, jnp.exp(jax.nn.log_sigmoid(z) + log_after), 0.0)
        outs.append(jnp.einsum('bhqk,bhkd->bhqd', weights.astype(vb.dtype), vb))
    o = jnp.concatenate(outs, axis=2)
    o = o.transpose(0, 2, 1, 3).reshape(B, S, D_MODEL)
    return o @ w_o


def multiscale_pool_mixer(h, w_in, w_grp, scale):
    B, S, _ = h.shape
    u = (h @ w_in).reshape(B, S, N_POOL_GROUPS, POOL_GROUP)
    uf = u.astype(jnp.float32)
    c = jnp.cumsum(uf, axis=1)
    pos = jnp.arange(S)
    outs = []
    for gi, w in enumerate(POOL_WINDOWS):
        cg = c[:, :, gi]
        cpad = jnp.pad(cg, ((0, 0), (w, 0), (0, 0)))
        wsum = cpad[:, w:] - cpad[:, :S]
        cnt = jnp.minimum(pos + 1, w).astype(jnp.float32)
        outs.append(wsum / cnt[None, :, None] - uf[:, :, gi])
    pooled = jnp.stack(outs, axis=2).astype(h.dtype)
    y = jnp.einsum('bsgc,gcd->bsgd', pooled, w_grp).reshape(B, S, D_MODEL)
    return y * scale


def _fwd_setup_inputs(seed: int = 0) -> dict:
    key = jax.random.key(seed)
    ks = iter(jax.random.split(key, 32))
    n_a = (DEPTH + 1) // 2
    n_b = DEPTH // 2
    f32 = jnp.float32

    def w(shape, fan_in):
        return jax.random.normal(next(ks), shape, f32) * fan_in ** -0.5

    def gain(shape):
        return 1.0 + 0.05 * jax.random.normal(next(ks), shape, f32)

    return {
        "x": jax.random.normal(next(ks), (BATCH, SEQ, D_MODEL), f32),
        "p": jax.random.normal(next(ks), (DEPTH, BATCH, SEQ, PLE_DIM), f32),
        "norm_ffn1": gain((DEPTH, D_MODEL)),
        "w_ffn1_gu": w((DEPTH, D_MODEL, 2 * D_FF), D_MODEL),
        "w_ffn1_down": w((DEPTH, D_FF, D_MODEL), D_FF),
        "norm_mix": gain((DEPTH, D_MODEL)),
        "w_qkv": w((n_a, D_MODEL, 3 * D_MODEL), D_MODEL),
        "q_norm": gain((n_a, HEAD_DIM)),
        "k_norm": gain((n_a, HEAD_DIM)),
        "w_o": w((n_a, D_MODEL, D_MODEL), D_MODEL),
        "w_pool_in": w((n_b, D_MODEL, D_MODEL), D_MODEL),
        "w_pool_grp": w((n_b, N_POOL_GROUPS, POOL_GROUP, POOL_GROUP), POOL_GROUP),
        "pool_scale": gain((n_b, D_MODEL)),
        "norm_ffn2": gain((DEPTH, D_MODEL)),
        "w_ffn2_gu": w((DEPTH, D_MODEL, 2 * D_FF), D_MODEL),
        "w_ffn2_down": w((DEPTH, D_FF, D_MODEL), D_FF),
        "norm_ple": gain((DEPTH, D_MODEL)),
        "w_ple_gate": w((DEPTH, D_MODEL, D_MODEL), D_MODEL),
        "w_ple_proj": w((DEPTH, PLE_DIM, D_MODEL), PLE_DIM),
    }


def _fwd_reference(x, p, norm_ffn1, w_ffn1_gu, w_ffn1_down, norm_mix, w_qkv, q_norm,
              k_norm, w_o, w_pool_in, w_pool_grp, pool_scale, norm_ffn2,
              w_ffn2_gu, w_ffn2_down, norm_ple, w_ple_gate, w_ple_proj):
    for i in range(DEPTH):
        x = x + 0.5 * swiglu(rms_norm(x, norm_ffn1[i]), w_ffn1_gu[i], w_ffn1_down[i])
        h = rms_norm(x, norm_mix[i])
        j = i // N_MIXERS
        if i % N_MIXERS == 0:
            mix = stick_breaking_attention(h, w_qkv[j], q_norm[j], k_norm[j], w_o[j])
        else:
            mix = multiscale_pool_mixer(h, w_pool_in[j], w_pool_grp[j], pool_scale[j])
        x = x + mix
        x = x + 0.5 * swiglu(rms_norm(x, norm_ffn2[i]), w_ffn2_gu[i], w_ffn2_down[i])
        gate = jax.nn.sigmoid(rms_norm(x, norm_ple[i]) @ w_ple_gate[i])
        x = x + gate * (p[i] @ w_ple_proj[i])
    return x


import jax as _jax
import jax.numpy as _jnp

TWIN_FORMAT = 'train_step'
FWD_PARAMS = ['x', 'p', 'norm_ffn1', 'w_ffn1_gu', 'w_ffn1_down', 'norm_mix', 'w_qkv', 'q_norm', 'k_norm', 'w_o', 'w_pool_in', 'w_pool_grp', 'pool_scale', 'norm_ffn2', 'w_ffn2_gu', 'w_ffn2_down', 'norm_ple', 'w_ple_gate', 'w_ple_proj']
TWIN_WEIGHTS = ['norm_ffn1', 'w_ffn1_gu', 'w_ffn1_down', 'norm_mix', 'w_qkv', 'q_norm', 'k_norm', 'w_o', 'w_pool_in', 'w_pool_grp', 'pool_scale', 'norm_ffn2', 'w_ffn2_gu', 'w_ffn2_down', 'norm_ple', 'w_ple_gate', 'w_ple_proj']
TWIN_DIFF_INPUT = 'x'
TWIN_INPUTS = ['x', 'p', 'norm_ffn1', 'w_ffn1_gu', 'w_ffn1_down', 'norm_mix', 'w_qkv', 'q_norm', 'k_norm', 'w_o', 'w_pool_in', 'w_pool_grp', 'pool_scale', 'norm_ffn2', 'w_ffn2_gu', 'w_ffn2_down', 'norm_ple', 'w_ple_gate', 'w_ple_proj', 'loss_target', 'm_norm_ffn1', 'm_w_ffn1_gu', 'm_w_ffn1_down', 'm_norm_mix', 'm_w_qkv', 'm_q_norm', 'm_k_norm', 'm_w_o', 'm_w_pool_in', 'm_w_pool_grp', 'm_pool_scale', 'm_norm_ffn2', 'm_w_ffn2_gu', 'm_w_ffn2_down', 'm_norm_ple', 'm_w_ple_gate', 'm_w_ple_proj', 'v_norm_ffn1', 'v_w_ffn1_gu', 'v_w_ffn1_down', 'v_norm_mix', 'v_w_qkv', 'v_q_norm', 'v_k_norm', 'v_w_o', 'v_w_pool_in', 'v_w_pool_grp', 'v_pool_scale', 'v_norm_ffn2', 'v_w_ffn2_gu', 'v_w_ffn2_down', 'v_norm_ple', 'v_w_ple_gate', 'v_w_ple_proj']
TWIN_OUTPUTS = ['loss', 'grad_x', 'grad_norm_ffn1', 'grad_w_ffn1_gu', 'grad_w_ffn1_down', 'grad_norm_mix', 'grad_w_qkv', 'grad_q_norm', 'grad_k_norm', 'grad_w_o', 'grad_w_pool_in', 'grad_w_pool_grp', 'grad_pool_scale', 'grad_norm_ffn2', 'grad_w_ffn2_gu', 'grad_w_ffn2_down', 'grad_norm_ple', 'grad_w_ple_gate', 'grad_w_ple_proj', 'delta_norm_ffn1', 'delta_w_ffn1_gu', 'delta_w_ffn1_down', 'delta_norm_mix', 'delta_w_qkv', 'delta_q_norm', 'delta_k_norm', 'delta_w_o', 'delta_w_pool_in', 'delta_w_pool_grp', 'delta_pool_scale', 'delta_norm_ffn2', 'delta_w_ffn2_gu', 'delta_w_ffn2_down', 'delta_norm_ple', 'delta_w_ple_gate', 'delta_w_ple_proj', 'new_m_norm_ffn1', 'new_m_w_ffn1_gu', 'new_m_w_ffn1_down', 'new_m_norm_mix', 'new_m_w_qkv', 'new_m_q_norm', 'new_m_k_norm', 'new_m_w_o', 'new_m_w_pool_in', 'new_m_w_pool_grp', 'new_m_pool_scale', 'new_m_norm_ffn2', 'new_m_w_ffn2_gu', 'new_m_w_ffn2_down', 'new_m_norm_ple', 'new_m_w_ple_gate', 'new_m_w_ple_proj', 'new_v_norm_ffn1', 'new_v_w_ffn1_gu', 'new_v_w_ffn1_down', 'new_v_norm_mix', 'new_v_w_qkv', 'new_v_q_norm', 'new_v_k_norm', 'new_v_w_o', 'new_v_w_pool_in', 'new_v_w_pool_grp', 'new_v_pool_scale', 'new_v_norm_ffn2', 'new_v_w_ffn2_gu', 'new_v_w_ffn2_down', 'new_v_norm_ple', 'new_v_w_ple_gate', 'new_v_w_ple_proj']
TWIN_LEAF_KINDS = {'loss': 'loss', 'grad_x': 'grad_x', 'grad_norm_ffn1': 'grad_w', 'grad_w_ffn1_gu': 'grad_w', 'grad_w_ffn1_down': 'grad_w', 'grad_norm_mix': 'grad_w', 'grad_w_qkv': 'grad_w', 'grad_q_norm': 'grad_w', 'grad_k_norm': 'grad_w', 'grad_w_o': 'grad_w', 'grad_w_pool_in': 'grad_w', 'grad_w_pool_grp': 'grad_w', 'grad_pool_scale': 'grad_w', 'grad_norm_ffn2': 'grad_w', 'grad_w_ffn2_gu': 'grad_w', 'grad_w_ffn2_down': 'grad_w', 'grad_norm_ple': 'grad_w', 'grad_w_ple_gate': 'grad_w', 'grad_w_ple_proj': 'grad_w', 'delta_norm_ffn1': 'delta_w', 'delta_w_ffn1_gu': 'delta_w', 'delta_w_ffn1_down': 'delta_w', 'delta_norm_mix': 'delta_w', 'delta_w_qkv': 'delta_w', 'delta_q_norm': 'delta_w', 'delta_k_norm': 'delta_w', 'delta_w_o': 'delta_w', 'delta_w_pool_in': 'delta_w', 'delta_w_pool_grp': 'delta_w', 'delta_pool_scale': 'delta_w', 'delta_norm_ffn2': 'delta_w', 'delta_w_ffn2_gu': 'delta_w', 'delta_w_ffn2_down': 'delta_w', 'delta_norm_ple': 'delta_w', 'delta_w_ple_gate': 'delta_w', 'delta_w_ple_proj': 'delta_w', 'new_m_norm_ffn1': 'new_m', 'new_m_w_ffn1_gu': 'new_m', 'new_m_w_ffn1_down': 'new_m', 'new_m_norm_mix': 'new_m', 'new_m_w_qkv': 'new_m', 'new_m_q_norm': 'new_m', 'new_m_k_norm': 'new_m', 'new_m_w_o': 'new_m', 'new_m_w_pool_in': 'new_m', 'new_m_w_pool_grp': 'new_m', 'new_m_pool_scale': 'new_m', 'new_m_norm_ffn2': 'new_m', 'new_m_w_ffn2_gu': 'new_m', 'new_m_w_ffn2_down': 'new_m', 'new_m_norm_ple': 'new_m', 'new_m_w_ple_gate': 'new_m', 'new_m_w_ple_proj': 'new_m', 'new_v_norm_ffn1': 'new_v', 'new_v_w_ffn1_gu': 'new_v', 'new_v_w_ffn1_down': 'new_v', 'new_v_norm_mix': 'new_v', 'new_v_w_qkv': 'new_v', 'new_v_q_norm': 'new_v', 'new_v_k_norm': 'new_v', 'new_v_w_o': 'new_v', 'new_v_w_pool_in': 'new_v', 'new_v_w_pool_grp': 'new_v', 'new_v_pool_scale': 'new_v', 'new_v_norm_ffn2': 'new_v', 'new_v_w_ffn2_gu': 'new_v', 'new_v_w_ffn2_down': 'new_v', 'new_v_norm_ple': 'new_v', 'new_v_w_ple_gate': 'new_v', 'new_v_w_ple_proj': 'new_v'}


def _forward(args):
    return _fwd_reference(*[args[k] for k in FWD_PARAMS])


def _output_shape():
    def fwd():
        inp = _fwd_setup_inputs(0)
        return _fwd_reference(*[inp[k] for k in FWD_PARAMS])
    out = _jax.eval_shape(fwd)
    return out.shape, out.dtype

N_MICROBATCH = 1
ADAM_LR = 0.001
ADAM_B1 = 0.9
ADAM_B2 = 0.999
ADAM_EPS = 1e-08
ADAM_WD = 0.01
ADAM_STEP = 10
PER_EXAMPLE_BATCH_AXIS = {'x': 0, 'p': 1, 'loss_target': 0}
SHARED_INPUTS = []
_WEIGHT_DTYPES = {'norm_ffn1': _jnp.float32, 'w_ffn1_gu': _jnp.float32, 'w_ffn1_down': _jnp.float32, 'norm_mix': _jnp.float32, 'w_qkv': _jnp.float32, 'q_norm': _jnp.float32, 'k_norm': _jnp.float32, 'w_o': _jnp.float32, 'w_pool_in': _jnp.float32, 'w_pool_grp': _jnp.float32, 'pool_scale': _jnp.float32, 'norm_ffn2': _jnp.float32, 'w_ffn2_gu': _jnp.float32, 'w_ffn2_down': _jnp.float32, 'norm_ple': _jnp.float32, 'w_ple_gate': _jnp.float32, 'w_ple_proj': _jnp.float32}
MOMENT_SCALE = {'norm_ffn1': 1.242944e+01, 'w_ffn1_gu': 2.380552e-01, 'w_ffn1_down': 4.456005e-01, 'norm_mix': 3.891898e+01, 'w_qkv': 7.231962e-01, 'q_norm': 6.361330e+01, 'k_norm': 6.313214e+01, 'w_o': 1.244732e+00, 'w_pool_in': 3.450010e+00, 'w_pool_grp': 4.265476e+00, 'pool_scale': 4.945442e+01, 'norm_ffn2': 1.236952e+01, 'w_ffn2_gu': 2.112709e-01, 'w_ffn2_down': 4.154747e-01, 'norm_ple': 1.805422e+00, 'w_ple_gate': 1.940041e-01, 'w_ple_proj': 1.021899e+00}


def _to_microbatches(a, axis):
    t = _jnp.moveaxis(a, axis, 0)
    t = t.reshape((N_MICROBATCH, t.shape[0] // N_MICROBATCH) + t.shape[1:])
    return _jnp.moveaxis(t, 1, axis + 1)


def setup_inputs(seed: int = 0) -> dict:
    inp = _fwd_setup_inputs(seed)
    key = _jax.random.fold_in(_jax.random.key(seed), 7919)
    shape, _ = _output_shape()
    out = dict(inp)
    out["loss_target"] = _jax.random.normal(_jax.random.fold_in(key, 0), shape, _jnp.float32)
    for i, name in enumerate(TWIN_WEIGHTS):
        w = inp[name].astype(_jnp.float32)
        if MOMENT_SCALE is None:
            s = _jnp.sqrt(_jnp.mean(_jnp.square(w)) + 1e-30)
        else:
            s = MOMENT_SCALE[name]
        km, kv = _jax.random.split(_jax.random.fold_in(key, i + 1))
        out[name] = w
        out["m_" + name] = s * _jax.random.normal(km, w.shape, _jnp.float32)
        out["v_" + name] = (s * s) * _jax.random.uniform(kv, w.shape, _jnp.float32, 0.5, 1.5)
    if N_MICROBATCH > 1:
        for name, axis in PER_EXAMPLE_BATCH_AXIS.items():
            out[name] = _to_microbatches(out[name], axis)
    return {'x': out['x'], 'p': out['p'], 'norm_ffn1': out['norm_ffn1'], 'w_ffn1_gu': out['w_ffn1_gu'], 'w_ffn1_down': out['w_ffn1_down'], 'norm_mix': out['norm_mix'], 'w_qkv': out['w_qkv'], 'q_norm': out['q_norm'], 'k_norm': out['k_norm'], 'w_o': out['w_o'], 'w_pool_in': out['w_pool_in'], 'w_pool_grp': out['w_pool_grp'], 'pool_scale': out['pool_scale'], 'norm_ffn2': out['norm_ffn2'], 'w_ffn2_gu': out['w_ffn2_gu'], 'w_ffn2_down': out['w_ffn2_down'], 'norm_ple': out['norm_ple'], 'w_ple_gate': out['w_ple_gate'], 'w_ple_proj': out['w_ple_proj'], 'loss_target': out['loss_target'], 'm_norm_ffn1': out['m_norm_ffn1'], 'm_w_ffn1_gu': out['m_w_ffn1_gu'], 'm_w_ffn1_down': out['m_w_ffn1_down'], 'm_norm_mix': out['m_norm_mix'], 'm_w_qkv': out['m_w_qkv'], 'm_q_norm': out['m_q_norm'], 'm_k_norm': out['m_k_norm'], 'm_w_o': out['m_w_o'], 'm_w_pool_in': out['m_w_pool_in'], 'm_w_pool_grp': out['m_w_pool_grp'], 'm_pool_scale': out['m_pool_scale'], 'm_norm_ffn2': out['m_norm_ffn2'], 'm_w_ffn2_gu': out['m_w_ffn2_gu'], 'm_w_ffn2_down': out['m_w_ffn2_down'], 'm_norm_ple': out['m_norm_ple'], 'm_w_ple_gate': out['m_w_ple_gate'], 'm_w_ple_proj': out['m_w_ple_proj'], 'v_norm_ffn1': out['v_norm_ffn1'], 'v_w_ffn1_gu': out['v_w_ffn1_gu'], 'v_w_ffn1_down': out['v_w_ffn1_down'], 'v_norm_mix': out['v_norm_mix'], 'v_w_qkv': out['v_w_qkv'], 'v_q_norm': out['v_q_norm'], 'v_k_norm': out['v_k_norm'], 'v_w_o': out['v_w_o'], 'v_w_pool_in': out['v_w_pool_in'], 'v_w_pool_grp': out['v_w_pool_grp'], 'v_pool_scale': out['v_pool_scale'], 'v_norm_ffn2': out['v_norm_ffn2'], 'v_w_ffn2_gu': out['v_w_ffn2_gu'], 'v_w_ffn2_down': out['v_w_ffn2_down'], 'v_norm_ple': out['v_norm_ple'], 'v_w_ple_gate': out['v_w_ple_gate'], 'v_w_ple_proj': out['v_w_ple_proj']}


def _loss(weights, diff, rest, loss_target):
    with _jax.named_scope("forward"):
        args = {**rest, TWIN_DIFF_INPUT: diff, **{k: w.astype(_WEIGHT_DTYPES[k]) for k, w in weights.items()}}
        y = _forward(args)
    with _jax.named_scope("loss_head"):
        err = _jnp.square(y.astype(_jnp.float32) - loss_target)
        return 0.5 * _jnp.sum(_jnp.mean(err, axis=-1)) if err.ndim else 0.5 * err


def _adamw(w, g, m, v):
    m = ADAM_B1 * m + (1.0 - ADAM_B1) * g
    v = ADAM_B2 * v + (1.0 - ADAM_B2) * _jnp.square(g)
    m_hat = m / (1.0 - ADAM_B1 ** ADAM_STEP)
    v_hat = v / (1.0 - ADAM_B2 ** ADAM_STEP)
    delta = -ADAM_LR * (m_hat / (_jnp.sqrt(v_hat) + ADAM_EPS) + ADAM_WD * w)
    return delta, m, v


def reference(x, p, norm_ffn1, w_ffn1_gu, w_ffn1_down, norm_mix, w_qkv, q_norm, k_norm, w_o, w_pool_in, w_pool_grp, pool_scale, norm_ffn2, w_ffn2_gu, w_ffn2_down, norm_ple, w_ple_gate, w_ple_proj, loss_target, m_norm_ffn1, m_w_ffn1_gu, m_w_ffn1_down, m_norm_mix, m_w_qkv, m_q_norm, m_k_norm, m_w_o, m_w_pool_in, m_w_pool_grp, m_pool_scale, m_norm_ffn2, m_w_ffn2_gu, m_w_ffn2_down, m_norm_ple, m_w_ple_gate, m_w_ple_proj, v_norm_ffn1, v_w_ffn1_gu, v_w_ffn1_down, v_norm_mix, v_w_qkv, v_q_norm, v_k_norm, v_w_o, v_w_pool_in, v_w_pool_grp, v_pool_scale, v_norm_ffn2, v_w_ffn2_gu, v_w_ffn2_down, v_norm_ple, v_w_ple_gate, v_w_ple_proj):
    given = dict(x=x, p=p, norm_ffn1=norm_ffn1, w_ffn1_gu=w_ffn1_gu, w_ffn1_down=w_ffn1_down, norm_mix=norm_mix, w_qkv=w_qkv, q_norm=q_norm, k_norm=k_norm, w_o=w_o, w_pool_in=w_pool_in, w_pool_grp=w_pool_grp, pool_scale=pool_scale, norm_ffn2=norm_ffn2, w_ffn2_gu=w_ffn2_gu, w_ffn2_down=w_ffn2_down, norm_ple=norm_ple, w_ple_gate=w_ple_gate, w_ple_proj=w_ple_proj, loss_target=loss_target, m_norm_ffn1=m_norm_ffn1, m_w_ffn1_gu=m_w_ffn1_gu, m_w_ffn1_down=m_w_ffn1_down, m_norm_mix=m_norm_mix, m_w_qkv=m_w_qkv, m_q_norm=m_q_norm, m_k_norm=m_k_norm, m_w_o=m_w_o, m_w_pool_in=m_w_pool_in, m_w_pool_grp=m_w_pool_grp, m_pool_scale=m_pool_scale, m_norm_ffn2=m_norm_ffn2, m_w_ffn2_gu=m_w_ffn2_gu, m_w_ffn2_down=m_w_ffn2_down, m_norm_ple=m_norm_ple, m_w_ple_gate=m_w_ple_gate, m_w_ple_proj=m_w_ple_proj, v_norm_ffn1=v_norm_ffn1, v_w_ffn1_gu=v_w_ffn1_gu, v_w_ffn1_down=v_w_ffn1_down, v_norm_mix=v_norm_mix, v_w_qkv=v_w_qkv, v_q_norm=v_q_norm, v_k_norm=v_k_norm, v_w_o=v_w_o, v_w_pool_in=v_w_pool_in, v_w_pool_grp=v_w_pool_grp, v_pool_scale=v_pool_scale, v_norm_ffn2=v_norm_ffn2, v_w_ffn2_gu=v_w_ffn2_gu, v_w_ffn2_down=v_w_ffn2_down, v_norm_ple=v_norm_ple, v_w_ple_gate=v_w_ple_gate, v_w_ple_proj=v_w_ple_proj)
    weights = {n: given[n] for n in TWIN_WEIGHTS}
    shared = {n: given[n] for n in SHARED_INPUTS}
    per_example = {n: given[n] for n in ['x', 'p']}
    grad_fn = _jax.value_and_grad(_loss, argnums=(0, 1))

    def one_microbatch(ex, loss_target):
        ex = dict(ex)
        diff = ex.pop(TWIN_DIFF_INPUT)
        return grad_fn(weights, diff, {**shared, **ex}, loss_target)

    if N_MICROBATCH == 1:
        loss, (grad_w, grad_x) = one_microbatch(per_example, given["loss_target"])
    else:
        def body(carry, xs):
            loss_sum, grad_sum = carry
            l_k, (gw_k, gx_k) = one_microbatch(xs[0], xs[1])
            with _jax.named_scope("update"):
                return (loss_sum + l_k, _jax.tree.map(_jnp.add, grad_sum, gw_k)), gx_k

        init = (_jnp.zeros((), _jnp.float32), _jax.tree.map(_jnp.zeros_like, weights))
        (loss, grad_w), grad_x = _jax.lax.scan(body, init, (per_example, given["loss_target"]))
    with _jax.named_scope("update"):
        delta_w, new_m, new_v = {}, {}, {}
        for n in TWIN_WEIGHTS:
            delta_w[n], new_m[n], new_v[n] = _adamw(weights[n], grad_w[n], given["m_" + n], given["v_" + n])
    return (loss, grad_x, *[grad_w[n] for n in TWIN_WEIGHTS], *[delta_w[n] for n in TWIN_WEIGHTS],
            *[new_m[n] for n in TWIN_WEIGHTS], *[new_v[n] for n in TWIN_WEIGHTS])
```

```python
import jax
import jax.numpy as jnp
from jax import lax
from jax.experimental import pallas as pl
from jax.experimental.pallas import tpu as pltpu

BF = jnp.bfloat16
F32 = jnp.float32

N_HEADS = 16
HEAD_DIM = 64
POOL_WINDOWS = (2, 4, 8, 16)
POOL_HALO = 16
EPS = 1e-6
ADAM_LR = 0.001
ADAM_B1 = 0.9
ADAM_B2 = 0.999
ADAM_EPS = 1e-08
ADAM_WD = 0.01
ADAM_STEP = 10

N_CHIPS = 4
N_DEV = 8
LANES = 128
VMEM_LIMIT = 56 * 1024 * 1024
ATT_T = 256
MESH = pl.DeviceIdType.MESH

NN = (((1,), (0,)), ((), ()))
NT = (((1,), (1,)), ((), ()))
TN = (((0,), (0,)), ((), ()))


def _params(n_axes):
    return pltpu.CompilerParams(dimension_semantics=("arbitrary",) * n_axes, vmem_limit_bytes=VMEM_LIMIT)


def _dot(a, b, dims):
    return lax.dot_general(a, b, dims, preferred_element_type=F32)


def _mm(a, b, extra, out_shapes, *, dims, grid, a_spec, b_spec, extra_specs, out_specs, acc_shape, epilogue, name):
    nk = grid[-1]
    n_extra = len(extra)
    n_out = len(out_shapes)

    def body(a_ref, b_ref, *rest):
        ex = rest[:n_extra]
        outs = rest[n_extra:n_extra + n_out]
        acc = rest[-1]
        k = pl.program_id(len(grid) - 1)

        @pl.when(k == 0)
        def _():
            acc[...] = jnp.zeros_like(acc)

        acc[...] += _dot(a_ref[...].astype(BF), b_ref[...].astype(BF), dims)
        first = pl.program_id(0) == 0

        @pl.when(k == nk - 1)
        def _():
            epilogue(acc[...], ex, outs, first)

    return pl.pallas_call(
        body, grid=grid, in_specs=[a_spec, b_spec, *extra_specs], out_specs=out_specs, out_shape=out_shapes,
        scratch_shapes=[pltpu.VMEM(acc_shape, F32)], compiler_params=_params(len(grid)), name=name,
    )(a, b, *extra)


def _store(dtype, scale=1.0):
    def ep(acc, ex, outs, first):
        outs[0][...] = (acc * scale).astype(dtype)
    return ep


def _residual(scale):
    def ep(acc, ex, outs, first):
        outs[0][...] = ex[0][...] + scale * acc
    return ep


def _rms_bwd_epilogue(acc, ex, outs, first):
    x_ref, g_ref, dx_ref = ex
    dxo_ref, dg_ref = outs
    x = x_ref[...]
    r = lax.rsqrt(jnp.mean(x * x, axis=-1, keepdims=True) + EPS)
    xh = x * r
    dxh = acc * g_ref[...]
    dxo_ref[...] = dx_ref[...] + r * (dxh - xh * jnp.mean(dxh * xh, axis=-1, keepdims=True))

    @pl.when(first)
    def _():
        dg_ref[...] = jnp.zeros_like(dg_ref)

    dg_ref[...] += jnp.sum(acc * xh, axis=0, keepdims=True)


def _mm_fwd(a, w, x, scale, tm, name):
    S, K = a.shape
    N = w.shape[1]
    return _mm(a, w, [x], [jax.ShapeDtypeStruct((S, N), F32)], dims=NN, grid=(S // tm, 1),
               a_spec=pl.BlockSpec((tm, K), lambda m, k: (m, 0)), b_spec=pl.BlockSpec((K, N), lambda m, k: (0, 0)),
               extra_specs=[pl.BlockSpec((tm, N), lambda m, k: (m, 0))], out_specs=[pl.BlockSpec((tm, N), lambda m, k: (m, 0))],
               acc_shape=(tm, N), epilogue=_residual(scale), name=name)[0]


def _mm_plain(a, w, dtype, tm, name):
    S, K = a.shape
    N = w.shape[1]
    return _mm(a, w, [], [jax.ShapeDtypeStruct((S, N), dtype)], dims=NN, grid=(S // tm, 1),
               a_spec=pl.BlockSpec((tm, K), lambda m, k: (m, 0)), b_spec=pl.BlockSpec((K, N), lambda m, k: (0, 0)),
               extra_specs=[], out_specs=[pl.BlockSpec((tm, N), lambda m, k: (m, 0))],
               acc_shape=(tm, N), epilogue=_store(dtype), name=name)[0]


def _mm_cols(a, w4, dtype, tm, name):
    S, K = a.shape
    nj, _, ns = w4.shape
    return _mm(a, w4, [], [jax.ShapeDtypeStruct((S, nj * ns), dtype)], dims=NN, grid=(S // tm, nj, 1),
               a_spec=pl.BlockSpec((tm, K), lambda m, j, k: (m, 0)), b_spec=pl.BlockSpec((None, K, ns), lambda m, j, k: (j, 0, 0)),
               extra_specs=[], out_specs=[pl.BlockSpec((tm, ns), lambda m, j, k: (m, j))],
               acc_shape=(tm, ns), epilogue=_store(dtype), name=name)[0]


def _mm_wgrad(a, b, scale, tk, name, tn=None):
    S, M = a.shape
    N = b.shape[1]
    tn = N if tn is None else tn
    return _mm(a, b, [], [jax.ShapeDtypeStruct((M, N), BF)], dims=TN, grid=(N // tn, S // tk),
               a_spec=pl.BlockSpec((tk, M), lambda n, k: (k, 0)), b_spec=pl.BlockSpec((tk, tn), lambda n, k: (k, n)),
               extra_specs=[], out_specs=[pl.BlockSpec((M, tn), lambda n, k: (0, n))],
               acc_shape=(M, tn), epilogue=_store(BF, scale), name=name)[0]


def _mm_wgrad_cols(a, b, nj, tk, name):
    S, M = a.shape
    ns = b.shape[1] // nj
    return _mm(a, b, [], [jax.ShapeDtypeStruct((nj, M, ns), BF)], dims=TN, grid=(nj, S // tk),
               a_spec=pl.BlockSpec((tk, M), lambda j, k: (k, 0)), b_spec=pl.BlockSpec((tk, ns), lambda j, k: (k, j)),
               extra_specs=[], out_specs=[pl.BlockSpec((None, M, ns), lambda j, k: (j, 0, 0))],
               acc_shape=(M, ns), epilogue=_store(BF), name=name)[0]


def _mm_dx_norm(a, w, x, gain, dx, tm, name):
    S, K = a.shape
    N = w.shape[0]
    row = pl.BlockSpec((tm, N), lambda m, k: (m, 0))
    vec = pl.BlockSpec((1, N), lambda m, k: (0, 0))
    return _mm(a, w, [x, gain, dx], [jax.ShapeDtypeStruct((S, N), F32), jax.ShapeDtypeStruct((1, N), F32)], dims=NT,
               grid=(S // tm, 1), a_spec=pl.BlockSpec((tm, K), lambda m, k: (m, 0)), b_spec=pl.BlockSpec((N, K), lambda m, k: (0, 0)),
               extra_specs=[row, vec, row], out_specs=[row, vec], acc_shape=(tm, N), epilogue=_rms_bwd_epilogue, name=name)


def _mm_dx_norm_cols(a, w4, x, gain, dx, tm, name):
    S = a.shape[0]
    nj, N, ks = w4.shape
    row = pl.BlockSpec((tm, N), lambda m, j: (m, 0))
    vec = pl.BlockSpec((1, N), lambda m, j: (0, 0))
    return _mm(a, w4, [x, gain, dx], [jax.ShapeDtypeStruct((S, N), F32), jax.ShapeDtypeStruct((1, N), F32)], dims=NT,
               grid=(S // tm, nj), a_spec=pl.BlockSpec((tm, ks), lambda m, j: (m, j)),
               b_spec=pl.BlockSpec((None, N, ks), lambda m, j: (j, 0, 0)),
               extra_specs=[row, vec, row], out_specs=[row, vec], acc_shape=(tm, N), epilogue=_rms_bwd_epilogue, name=name)


def _rmsnorm(x, gain, tm, name):
    S, D = x.shape

    def body(x_ref, g_ref, h_ref):
        xv = x_ref[...]
        r = lax.rsqrt(jnp.mean(xv * xv, axis=-1, keepdims=True) + EPS)
        h_ref[...] = (xv * r * g_ref[...]).astype(BF)

    return pl.pallas_call(
        body, grid=(S // tm,), in_specs=[pl.BlockSpec((tm, D), lambda m: (m, 0)), pl.BlockSpec((1, D), lambda m: (0, 0))],
        out_specs=pl.BlockSpec((tm, D), lambda m: (m, 0)), out_shape=jax.ShapeDtypeStruct((S, D), BF),
        compiler_params=_params(1), name=name)(x, gain)


def _ffn_up(h, wgu4, tm, name):
    S, D = h.shape
    ns = wgu4.shape[2]
    half = wgu4.shape[0] // 2

    def body(h_ref, wg_ref, wu_ref, g_ref, u_ref, act_ref):
        hv = h_ref[...]
        g = _dot(hv, wg_ref[...], NN)
        u = _dot(hv, wu_ref[...], NN)
        g_ref[...] = g.astype(BF)
        u_ref[...] = u.astype(BF)
        act_ref[...] = (g * jax.nn.sigmoid(g) * u).astype(BF)

    out = jax.ShapeDtypeStruct((S, half * ns), BF)
    tile = pl.BlockSpec((tm, ns), lambda j, m: (m, j))
    return pl.pallas_call(
        body, grid=(half, S // tm),
        in_specs=[pl.BlockSpec((tm, D), lambda j, m: (m, 0)), pl.BlockSpec((None, D, ns), lambda j, m: (j, 0, 0)),
                  pl.BlockSpec((None, D, ns), lambda j, m: (j + half, 0, 0))],
        out_specs=[tile, tile, tile], out_shape=[out, out, out], compiler_params=_params(2), name=name)(h, wgu4, wgu4)


def _ffn_dact(dx, wdown, g, u, tm, name):
    S, D = dx.shape
    F = wdown.shape[0]

    def ep(acc, ex, outs, first):
        gv = ex[0][...].astype(F32)
        uv = ex[1][...].astype(F32)
        da = 0.5 * acc
        sg = jax.nn.sigmoid(gv)
        outs[0][:, :F] = (da * uv * (sg * (1.0 + gv * (1.0 - sg)))).astype(BF)
        outs[0][:, F:] = (da * (gv * sg)).astype(BF)

    row = pl.BlockSpec((tm, F), lambda m, k: (m, 0))
    return _mm(dx, wdown, [g, u], [jax.ShapeDtypeStruct((S, 2 * F), BF)], dims=NT, grid=(S // tm, 1),
               a_spec=pl.BlockSpec((tm, D), lambda m, k: (m, 0)), b_spec=pl.BlockSpec((F, D), lambda m, k: (0, 0)),
               extra_specs=[row, row], out_specs=[pl.BlockSpec((tm, 2 * F), lambda m, k: (m, 0))],
               acc_shape=(tm, F), epilogue=ep, name=name)[0]


def _ple_fwd(x, gain, wgate, p, wproj, tm, name):
    S, D = x.shape
    P = p.shape[1]

    def body(x_ref, g_ref, wg_ref, p_ref, wp_ref, xo_ref, hp_ref, gp_ref, pe_ref):
        xv = x_ref[...]
        r = lax.rsqrt(jnp.mean(xv * xv, axis=-1, keepdims=True) + EPS)
        hp = (xv * r * g_ref[...]).astype(BF)
        gp = _dot(hp, wg_ref[...], NN)
        pe = _dot(p_ref[...].astype(BF), wp_ref[...], NN)
        xo_ref[...] = xv + jax.nn.sigmoid(gp) * pe
        hp_ref[...] = hp
        gp_ref[...] = gp.astype(BF)
        pe_ref[...] = pe.astype(BF)

    row = pl.BlockSpec((tm, D), lambda m: (m, 0))
    return pl.pallas_call(
        body, grid=(S // tm,),
        in_specs=[row, pl.BlockSpec((1, D), lambda m: (0, 0)), pl.BlockSpec((D, D), lambda m: (0, 0)),
                  pl.BlockSpec((tm, P), lambda m: (m, 0)), pl.BlockSpec((P, D), lambda m: (0, 0))],
        out_specs=[row, row, row, row],
        out_shape=[jax.ShapeDtypeStruct((S, D), F32)] + [jax.ShapeDtypeStruct((S, D), BF)] * 3,
        compiler_params=_params(1), name=name)(x, gain, wgate, p, wproj)


def _ple_bwd(dx, gp, pe, wgate, x, gain, tm, name):
    S, D = dx.shape

    def body(dx_ref, gp_ref, pe_ref, wg_ref, x_ref, g_ref, dxo_ref, dgp_ref, dpe_ref, dg_ref):
        dxv = dx_ref[...]
        sg = jax.nn.sigmoid(gp_ref[...].astype(F32))
        dgp = (dxv * pe_ref[...].astype(F32) * (sg * (1.0 - sg))).astype(BF)
        dgp_ref[...] = dgp
        dpe_ref[...] = (dxv * sg).astype(BF)
        dhp = _dot(dgp, wg_ref[...], NT)
        _rms_bwd_epilogue(dhp, (x_ref, g_ref, dx_ref), (dxo_ref, dg_ref), pl.program_id(0) == 0)

    row = pl.BlockSpec((tm, D), lambda m: (m, 0))
    vec = pl.BlockSpec((1, D), lambda m: (0, 0))
    return pl.pallas_call(
        body, grid=(S // tm,), in_specs=[row, row, row, pl.BlockSpec((D, D), lambda m: (0, 0)), row, vec],
        out_specs=[row, row, row, vec],
        out_shape=[jax.ShapeDtypeStruct((S, D), F32), jax.ShapeDtypeStruct((S, D), BF), jax.ShapeDtypeStruct((S, D), BF),
                   jax.ShapeDtypeStruct((1, D), F32)],
        compiler_params=_params(1), name=name)(dx, gp, pe, wgate, x, gain)


def _loss_head(y, target, tm, name):
    S, D = y.shape

    def body(y_ref, t_ref, sq_ref, dy_ref):
        d = y_ref[...] - t_ref[...]
        dy_ref[...] = d * (1.0 / D)

        @pl.when(pl.program_id(0) == 0)
        def _():
            sq_ref[...] = jnp.zeros_like(sq_ref)

        sq_ref[...] += jnp.sum(d * d, keepdims=True)

    row = pl.BlockSpec((tm, D), lambda m: (m, 0))
    return pl.pallas_call(
        body, grid=(S // tm,), in_specs=[row, row], out_specs=[pl.BlockSpec((1, 1), lambda m: (0, 0)), row],
        out_shape=[jax.ShapeDtypeStruct((1, 1), F32), jax.ShapeDtypeStruct((S, D), F32)],
        compiler_params=_params(1), name=name)(y, target)


def _qk_norm(qkv, qgain, kgain, tm, name):
    S = qkv.shape[0]
    D = qkv.shape[1] // 3
    nb = D // LANES

    def norm2(t, gain):
        lo = lax.broadcasted_iota(jnp.int32, t.shape, 1) < HEAD_DIM
        sq = t * t
        s_lo = jnp.sum(jnp.where(lo, sq, 0.0), axis=1, keepdims=True)
        s_hi = jnp.sum(jnp.where(lo, 0.0, sq), axis=1, keepdims=True)
        r = lax.rsqrt(jnp.where(lo, s_lo, s_hi) * (1.0 / HEAD_DIM) + EPS)
        return t * r * gain

    def body(q_ref, k_ref, v_ref, qg_ref, kg_ref, qo_ref, ko_ref, vo_ref):
        qo_ref[...] = (norm2(q_ref[...], qg_ref[...]) * (HEAD_DIM ** -0.5)).astype(BF)
        ko_ref[...] = norm2(k_ref[...], kg_ref[...]).astype(BF)
        vo_ref[...] = v_ref[...].astype(BF)

    vec = pl.BlockSpec((1, LANES), lambda m, h: (0, 0))
    tile = pl.BlockSpec((tm, LANES), lambda m, h: (m, h))
    out = jax.ShapeDtypeStruct((S, D), BF)
    return pl.pallas_call(
        body, grid=(S // tm, nb),
        in_specs=[tile, pl.BlockSpec((tm, LANES), lambda m, h: (m, nb + h)), pl.BlockSpec((tm, LANES), lambda m, h: (m, 2 * nb + h)), vec, vec],
        out_specs=[tile, tile, tile], out_shape=[out, out, out], compiler_params=_params(2), name=name)(qkv, qkv, qkv, qgain, kgain)


def _qk_norm_bwd(qkv, dqs, dkn, dv, qgain, kgain, tm, name):
    S = qkv.shape[0]
    D = qkv.shape[1] // 3
    nb = D // LANES

    def norm2_bwd(t, gain, dn):
        lo = lax.broadcasted_iota(jnp.int32, t.shape, 1) < HEAD_DIM

        def headsum(val):
            s_lo = jnp.sum(jnp.where(lo, val, 0.0), axis=1, keepdims=True)
            s_hi = jnp.sum(jnp.where(lo, 0.0, val), axis=1, keepdims=True)
            return jnp.where(lo, s_lo, s_hi)

        r = lax.rsqrt(headsum(t * t) * (1.0 / HEAD_DIM) + EPS)
        th = t * r
        dth = dn * gain
        dt = r * (dth - th * (headsum(dth * th) * (1.0 / HEAD_DIM)))
        return dt, jnp.sum(dn * th, axis=0, keepdims=True)

    def body(q_ref, k_ref, dq_ref, dk_ref, dv_ref, qg_ref, kg_ref, dqo_ref, dko_ref, dvo_ref, dqg_ref, dkg_ref):
        dq, dqg = norm2_bwd(q_ref[...], qg_ref[...], dq_ref[...] * (HEAD_DIM ** -0.5))
        dk, dkg = norm2_bwd(k_ref[...], kg_ref[...], dk_ref[...])
        dqo_ref[...] = dq.astype(BF)
        dko_ref[...] = dk.astype(BF)
        dvo_ref[...] = dv_ref[...].astype(BF)

        @pl.when((pl.program_id(0) == 0) & (pl.program_id(1) == 0))
        def _():
            dqg_ref[...] = jnp.zeros_like(dqg_ref)
            dkg_ref[...] = jnp.zeros_like(dkg_ref)

        dqg_ref[...] += dqg
        dkg_ref[...] += dkg

    vec = pl.BlockSpec((1, LANES), lambda m, h: (0, 0))
    tile = pl.BlockSpec((tm, LANES), lambda m, h: (m, h))
    return pl.pallas_call(
        body, grid=(S // tm, nb),
        in_specs=[tile, pl.BlockSpec((tm, LANES), lambda m, h: (m, nb + h)), tile, tile, tile, vec, vec],
        out_specs=[tile, tile, tile, vec, vec],
        out_shape=[jax.ShapeDtypeStruct((S, D), BF)] * 3 + [jax.ShapeDtypeStruct((1, LANES), F32)] * 2,
        compiler_params=_params(2), name=name)(qkv, qkv, dqs, dkn, dv, qgain, kgain)


def _split_bf16(t):
    hi = t.astype(BF)
    return hi, (t - hi.astype(F32)).astype(BF)


def _attn_scores(q, k, tri):
    z = _dot(q, k, NT)
    sp = jnp.log(1.0 + jnp.exp(-jnp.abs(z)))
    a = jnp.minimum(z, 0.0) - sp
    b = a - z
    if tri is not None:
        b = jnp.where(tri, b, 0.0)
    return a, b


def _attn_fwd(qs, kn, vb, name):
    S, D = qs.shape
    T = ATT_T

    def body(q_ref, k_ref, v_ref, o_ref):
        qi = pl.program_id(1)
        row = lax.broadcasted_iota(jnp.int32, (T, T), 0)
        col = lax.broadcasted_iota(jnp.int32, (T, T), 1)
        after = (row > col).astype(BF)
        tri = col < row
        for hh in range(LANES // HEAD_DIM):
            lanes = slice(hh * HEAD_DIM, (hh + 1) * HEAD_DIM)
            q = q_ref[:, lanes]

            def block(start, mask, carry):
                o_acc, stay = carry
                k = k_ref[pl.ds(start, T), lanes]
                v = v_ref[pl.ds(start, T), lanes]
                a, b = _attn_scores(q, k, mask)
                b_hi, b_lo = _split_bf16(b)
                c_in = _dot(b_hi, after, NN) + _dot(b_lo, after, NN)
                w = jnp.exp(a + c_in + stay)
                if mask is not None:
                    w = jnp.where(mask, w, 0.0)
                o_acc = o_acc + _dot(w.astype(BF), v, NN)
                return o_acc, stay + c_in[:, :1] + b[:, :1]

            carry = (jnp.zeros((T, HEAD_DIM), F32), jnp.zeros((T, 1), F32))
            carry = block(pl.multiple_of(qi * T, T), tri, carry)
            carry = lax.fori_loop(0, qi, lambda i, c: block(pl.multiple_of((qi - 1 - i) * T, T), None, c), carry)
            o_ref[:, lanes] = carry[0]

    tile = pl.BlockSpec((T, LANES), lambda h, m: (m, h))
    full = pl.BlockSpec((S, LANES), lambda h, m: (0, h))
    return pl.pallas_call(
        body, grid=(D // LANES, S // T), in_specs=[tile, full, full], out_specs=tile,
        out_shape=jax.ShapeDtypeStruct((S, D), F32), compiler_params=_params(2), name=name)(qs, kn, vb)


def _attn_bwd(qs, kn, vb, o, do, name):
    S, D = qs.shape
    T = ATT_T

    def body(q_ref, k_ref, v_ref, o_ref, do_ref, dq_ref, dk_ref, dv_ref):
        qi = pl.program_id(1)

        @pl.when(qi == 0)
        def _():
            dk_ref[...] = jnp.zeros_like(dk_ref)
            dv_ref[...] = jnp.zeros_like(dv_ref)

        row = lax.broadcasted_iota(jnp.int32, (T, T), 0)
        col = lax.broadcasted_iota(jnp.int32, (T, T), 1)
        after = (row > col).astype(BF)
        from_here = (row >= col).astype(BF)
        tri = col < row
        for hh in range(LANES // HEAD_DIM):
            lanes = slice(hh * HEAD_DIM, (hh + 1) * HEAD_DIM)
            q = q_ref[:, lanes]
            dout = do_ref[:, lanes]
            total = jnp.sum(dout.astype(F32) * o_ref[:, lanes], axis=1, keepdims=True)

            def block(start, mask, carry):
                dq_acc, stay, g_right = carry
                rows = pl.ds(start, T)
                k = k_ref[rows, lanes]
                v = v_ref[rows, lanes]
                a, b = _attn_scores(q, k, mask)
                b_hi, b_lo = _split_bf16(b)
                c_in = _dot(b_hi, after, NN) + _dot(b_lo, after, NN)
                w = jnp.exp(a + c_in + stay)
                if mask is not None:
                    w = jnp.where(mask, w, 0.0)
                wb = w.astype(BF)
                g = wb.astype(F32) * _dot(dout, v, NT)
                g_hi, g_lo = _split_bf16(g)
                g_in = _dot(g_hi, from_here, NN) + _dot(g_lo, from_here, NN)
                g_left = total - g_right - g_in
                dz = g - jnp.exp(a) * (g + g_left)
                if mask is not None:
                    dz = jnp.where(mask, dz, 0.0)
                dzb = dz.astype(BF)
                dq_acc = dq_acc + _dot(dzb, k, NN)
                dk_ref[rows, lanes] += _dot(dzb, q, TN)
                dv_ref[rows, lanes] += _dot(wb, dout, TN)
                return dq_acc, stay + c_in[:, :1] + b[:, :1], g_right + g_in[:, :1]

            carry = (jnp.zeros((T, HEAD_DIM), F32), jnp.zeros((T, 1), F32), jnp.zeros((T, 1), F32))
            carry = block(pl.multiple_of(qi * T, T), tri, carry)
            carry = lax.fori_loop(0, qi, lambda i, c: block(pl.multiple_of((qi - 1 - i) * T, T), None, c), carry)
            dq_ref[:, lanes] = carry[0]

    tile = pl.BlockSpec((T, LANES), lambda h, m: (m, h))
    full = pl.BlockSpec((S, LANES), lambda h, m: (0, h))
    out = jax.ShapeDtypeStruct((S, D), F32)
    return pl.pallas_call(
        body, grid=(D // LANES, S // T), in_specs=[tile, full, full, tile, tile], out_specs=[tile, full, full],
        out_shape=[out, out, out], compiler_params=_params(2), name=name)(qs, kn, vb, o, do)


def _pool_counts(T, first_row):
    pos = first_row + lax.broadcasted_iota(jnp.int32, (T, 1), 0)
    return [jnp.minimum(pos + 1, w).astype(F32) for w in POOL_WINDOWS]


def _pool_fwd(u, wgrp, scale, x, tm, name):
    S, D = u.shape
    G = len(POOL_WINDOWS)
    C = D // G
    H = POOL_HALO

    def body(u_ref, prev_ref, w_ref, s_ref, x_ref, xo_ref, pooled_ref):
        m = pl.program_id(0)
        prev = jnp.where(m == 0, 0.0, prev_ref[...])
        ext = jnp.concatenate([prev, u_ref[...]], axis=0)
        counts = _pool_counts(tm, m * tm)
        ys = []
        acc = ext
        shift = 1
        for gi, w in enumerate(POOL_WINDOWS):
            while shift < w:
                acc = acc + pltpu.roll(acc, shift, axis=0)
                shift *= 2
            cols = slice(gi * C, (gi + 1) * C)
            pooled = (acc[H:, cols] / counts[gi] - ext[H:, cols]).astype(BF)
            pooled_ref[:, cols] = pooled
            ys.append(_dot(pooled, w_ref[gi], NN))
        xo_ref[...] = x_ref[...] + jnp.concatenate(ys, axis=1) * s_ref[...]

    row = pl.BlockSpec((tm, D), lambda m: (m, 0))
    return pl.pallas_call(
        body, grid=(S // tm,),
        in_specs=[row, pl.BlockSpec((H, D), lambda m: (jnp.maximum(m * (tm // H) - 1, 0), 0)),
                  pl.BlockSpec((G, C, C), lambda m: (0, 0, 0)), pl.BlockSpec((1, D), lambda m: (0, 0)), row],
        out_specs=[row, row], out_shape=[jax.ShapeDtypeStruct((S, D), F32), jax.ShapeDtypeStruct((S, D), BF)],
        compiler_params=_params(1), name=name)(u, u, wgrp, scale, x)


def _pool_bwd_grp(dx, pooled, wgrp, scale, tm, name):
    S, D = dx.shape
    G = len(POOL_WINDOWS)
    C = D // G

    def body(dx_ref, pooled_ref, w_ref, s_ref, dp_ref, dw_ref, ds_ref, dw_acc):
        m = pl.program_id(0)

        @pl.when(m == 0)
        def _():
            dw_acc[...] = jnp.zeros_like(dw_acc)
            ds_ref[...] = jnp.zeros_like(ds_ref)

        dxv = dx_ref[...]
        dy = (dxv * s_ref[...]).astype(BF)
        ys = []
        for gi in range(G):
            cols = slice(gi * C, (gi + 1) * C)
            pg = pooled_ref[:, cols]
            ys.append(_dot(pg, w_ref[gi], NN))
            dw_acc[gi] += _dot(pg, dy[:, cols], TN)
            dp_ref[:, cols] = _dot(dy[:, cols], w_ref[gi], NT)
        ds_ref[...] += jnp.sum(dxv * jnp.concatenate(ys, axis=1), axis=0, keepdims=True)

        @pl.when(m == S // tm - 1)
        def _():
            dw_ref[...] = dw_acc[...].astype(BF)

    row = pl.BlockSpec((tm, D), lambda m: (m, 0))
    wspec = pl.BlockSpec((G, C, C), lambda m: (0, 0, 0))
    vec = pl.BlockSpec((1, D), lambda m: (0, 0))
    return pl.pallas_call(
        body, grid=(S // tm,), in_specs=[row, row, wspec, vec], out_specs=[row, wspec, vec],
        out_shape=[jax.ShapeDtypeStruct((S, D), F32), jax.ShapeDtypeStruct((G, C, C), BF), jax.ShapeDtypeStruct((1, D), F32)],
        scratch_shapes=[pltpu.VMEM((G, C, C), F32)], compiler_params=_params(1), name=name)(dx, pooled, wgrp, scale)


def _pool_bwd_window(dp, tm, name):
    S, D = dp.shape
    G = len(POOL_WINDOWS)
    C = D // G
    H = POOL_HALO
    last = S // tm - 1

    def body(dp_ref, next_ref, du_ref):
        m = pl.program_id(0)
        counts = _pool_counts(tm + H, m * tm)
        nxt = jnp.where(m == last, 0.0, next_ref[...])
        ext = jnp.concatenate([dp_ref[...], nxt], axis=0)
        for gi, w in enumerate(POOL_WINDOWS):
            cols = slice(gi * C, (gi + 1) * C)
            acc = ext[:, cols] / counts[gi]
            shift = 1
            while shift < w:
                acc = acc + pltpu.roll(acc, tm + H - shift, axis=0)
                shift *= 2
            du_ref[:, cols] = (acc[:tm] - ext[:tm, cols]).astype(BF)

    row = pl.BlockSpec((tm, D), lambda m: (m, 0))
    return pl.pallas_call(
        body, grid=(S // tm,),
        in_specs=[row, pl.BlockSpec((H, D), lambda m: (jnp.minimum((m + 1) * (tm // H), S // H - 1), 0))],
        out_specs=row, out_shape=jax.ShapeDtypeStruct((S, D), BF), compiler_params=_params(1), name=name)(dp, dp)


ELEMENTWISE_TILE_BYTES = 1 << 20


def _row_tile(rows, row_bytes):
    for cand in (512, 256, 128, 64, 32, 16, 8):
        if rows % cand == 0 and cand * row_bytes <= ELEMENTWISE_TILE_BYTES:
            return cand
    return rows


def _adamw(w, g, m, v, name):
    shape = w.shape
    cols = shape[-1]
    rows = w.size // cols
    tr = _row_tile(rows, cols * 4)

    def body(w_ref, g_ref, m_ref, v_ref, d_ref, mo_ref, vo_ref):
        gv = g_ref[...]
        mn = ADAM_B1 * m_ref[...] + (1.0 - ADAM_B1) * gv
        vn = ADAM_B2 * v_ref[...] + (1.0 - ADAM_B2) * jnp.square(gv)
        m_hat = mn / (1.0 - ADAM_B1 ** ADAM_STEP)
        v_hat = vn / (1.0 - ADAM_B2 ** ADAM_STEP)
        d_ref[...] = -ADAM_LR * (m_hat / (jnp.sqrt(v_hat) + ADAM_EPS) + ADAM_WD * w_ref[...])
        mo_ref[...] = mn
        vo_ref[...] = vn

    tile = pl.BlockSpec((tr, cols), lambda i: (i, 0))
    out = jax.ShapeDtypeStruct((rows, cols), F32)
    res = pl.pallas_call(
        body, grid=(rows // tr,), in_specs=[tile] * 4, out_specs=[tile] * 3, out_shape=[out] * 3,
        compiler_params=_params(1), name=name)(*[t.reshape(rows, cols) for t in (w, g, m, v)])
    return [t.reshape(shape) for t in res]


def _sum_slots(r, name):
    n = r.shape[0]
    shape = r.shape[1:]
    cols = shape[-1]
    rows = r.size // (n * cols)
    tr = _row_tile(rows, n * cols * r.dtype.itemsize)

    def body(r_ref, o_ref):
        acc = r_ref[0].astype(F32)
        for d in range(1, n):
            acc = acc + r_ref[d].astype(F32)
        o_ref[...] = acc

    return pl.pallas_call(
        body, grid=(rows // tr,), in_specs=[pl.BlockSpec((n, tr, cols), lambda i: (0, i, 0))],
        out_specs=pl.BlockSpec((tr, cols), lambda i: (i, 0)), out_shape=jax.ShapeDtypeStruct((rows, cols), F32),
        compiler_params=_params(1), name=name)(r.reshape(n, rows, cols)).reshape(shape)


ANY = pl.BlockSpec(memory_space=pl.ANY)


def _position():
    return lax.axis_index("x"), lax.axis_index("y"), lax.axis_index("c")


def _gather_small(t, name):
    rows, cols = t.shape

    def body(t_ref, o_ref, send_sems, recv_sems):
        x, y, c = _position()
        me = 4 * x + 2 * y + c
        o_ref[me] = t_ref[...]
        copies = []
        for k in range(1, N_DEV):
            peer = (x ^ (k >> 2), y ^ ((k >> 1) & 1), c ^ (k & 1))
            cp = pltpu.make_async_remote_copy(src_ref=t_ref, dst_ref=o_ref.at[me], send_sem=send_sems.at[k - 1],
                                              recv_sem=recv_sems.at[k - 1], device_id=peer, device_id_type=MESH)
            cp.start()
            copies.append((cp, 4 * peer[0] + 2 * peer[1] + peer[2]))
        for k, (cp, src) in enumerate(copies):
            pltpu.make_async_remote_copy(src_ref=t_ref, dst_ref=o_ref.at[src], send_sem=send_sems.at[k], recv_sem=recv_sems.at[k],
                                         device_id=(x, y, c), device_id_type=MESH).wait_recv()
        for cp, _ in copies:
            cp.wait_send()

    return pl.pallas_call(
        body, in_specs=[pl.BlockSpec(memory_space=pltpu.VMEM)], out_specs=pl.BlockSpec(memory_space=pltpu.VMEM),
        out_shape=jax.ShapeDtypeStruct((N_DEV, rows, cols), F32),
        scratch_shapes=[pltpu.SemaphoreType.DMA((N_DEV - 1,)), pltpu.SemaphoreType.DMA((N_DEV - 1,))], name=name)(t)


def _gather_weights(shards, name):
    n = len(shards)

    def body(*refs):
        ins, outs = refs[:n], refs[n:2 * n]
        send_sems, recv_sems, local_sems = refs[2 * n:]
        x, y, c = _position()
        chip = 2 * x + y
        chips = [(1 - x, y), (x, 1 - y), (1 - x, 1 - y)]
        local = []
        sends = []
        for p in range(n):
            cp = pltpu.make_async_copy(ins[p], outs[p].at[chip], local_sems.at[p])
            cp.start()
            local.append(cp)
            for k, (px, py) in enumerate(chips):
                cp = pltpu.make_async_remote_copy(src_ref=ins[p], dst_ref=outs[p].at[chip], send_sem=send_sems.at[p, k],
                                                  recv_sem=recv_sems.at[p, k], device_id=(px, py, c), device_id_type=MESH)
                cp.start()
                sends.append(cp)
        for p in range(n):
            for k, (px, py) in enumerate(chips):
                pltpu.make_async_remote_copy(src_ref=ins[p], dst_ref=outs[p].at[2 * px + py], send_sem=send_sems.at[p, k],
                                             recv_sem=recv_sems.at[p, k], device_id=(x, y, c), device_id_type=MESH).wait_recv()
        for cp in sends:
            cp.wait_send()
        for cp in local:
            cp.wait()

    return pl.pallas_call(
        body, in_specs=[ANY] * n, out_specs=[ANY] * n,
        out_shape=[jax.ShapeDtypeStruct((N_CHIPS,) + s.shape, s.dtype) for s in shards],
        scratch_shapes=[pltpu.SemaphoreType.DMA((n, 3)), pltpu.SemaphoreType.DMA((n, 3)), pltpu.SemaphoreType.DMA((n,))],
        name=name)(*shards)


def _scatter_grads(grads, name):
    n = len(grads)

    def body(*refs):
        ins, outs = refs[:n], refs[n:2 * n]
        send_sems, recv_sems, local_sems = refs[2 * n:]
        x, y, c = _position()
        me = 4 * x + 2 * y + c
        local = []
        sends = []
        for p in range(n):
            half = grads[p].shape[1] // 2

            def piece(px, py, pc, p=p, half=half):
                return ins[p].at[2 * px + py, pl.ds(pc * half, half)]

            cp = pltpu.make_async_copy(piece(x, y, c), outs[p].at[me], local_sems.at[p])
            cp.start()
            local.append(cp)
            for k in range(1, N_DEV):
                peer = (x ^ (k >> 2), y ^ ((k >> 1) & 1), c ^ (k & 1))
                cp = pltpu.make_async_remote_copy(src_ref=piece(*peer), dst_ref=outs[p].at[me], send_sem=send_sems.at[p, k - 1],
                                                  recv_sem=recv_sems.at[p, k - 1], device_id=peer, device_id_type=MESH)
                cp.start()
                sends.append(cp)
        for p in range(n):
            half = grads[p].shape[1] // 2
            for k in range(1, N_DEV):
                src = 4 * (x ^ (k >> 2)) + 2 * (y ^ ((k >> 1) & 1)) + (c ^ (k & 1))
                pltpu.make_async_remote_copy(src_ref=ins[p].at[0, pl.ds(0, half)], dst_ref=outs[p].at[src],
                                             send_sem=send_sems.at[p, k - 1], recv_sem=recv_sems.at[p, k - 1],
                                             device_id=(x, y, c), device_id_type=MESH).wait_recv()
        for cp in sends:
            cp.wait_send()
        for cp in local:
            cp.wait()

    return pl.pallas_call(
        body, in_specs=[ANY] * n, out_specs=[ANY] * n,
        out_shape=[jax.ShapeDtypeStruct((N_DEV, g.shape[1] // 2) + g.shape[2:], g.dtype) for g in grads],
        scratch_shapes=[pltpu.SemaphoreType.DMA((n, N_DEV - 1)), pltpu.SemaphoreType.DMA((n, N_DEV - 1)), pltpu.SemaphoreType.DMA((n,))],
        name=name)(*grads)


def _join_halves(halves, name):
    n = len(halves)

    def body(*refs):
        ins, outs = refs[:n], refs[n:2 * n]
        send_sems, recv_sems, local_sems = refs[2 * n:]
        x, y, c = _position()
        local = []
        sends = []
        for p in range(n):
            half = halves[p].shape[0]
            mine = outs[p].at[pl.ds(c * half, half)]
            cp = pltpu.make_async_copy(ins[p], mine, local_sems.at[p])
            cp.start()
            local.append(cp)
            cp = pltpu.make_async_remote_copy(src_ref=ins[p], dst_ref=mine, send_sem=send_sems.at[p], recv_sem=recv_sems.at[p],
                                              device_id=(x, y, 1 - c), device_id_type=MESH)
            cp.start()
            sends.append(cp)
        for p in range(n):
            half = halves[p].shape[0]
            pltpu.make_async_remote_copy(src_ref=ins[p], dst_ref=outs[p].at[pl.ds((1 - c) * half, half)], send_sem=send_sems.at[p],
                                         recv_sem=recv_sems.at[p], device_id=(x, y, c), device_id_type=MESH).wait_recv()
        for cp in sends:
            cp.wait_send()
        for cp in local:
            cp.wait()

    return pl.pallas_call(
        body, in_specs=[ANY] * n, out_specs=[ANY] * n,
        out_shape=[jax.ShapeDtypeStruct((2 * h.shape[0],) + h.shape[1:], h.dtype) for h in halves],
        scratch_shapes=[pltpu.SemaphoreType.DMA((n,)), pltpu.SemaphoreType.DMA((n,)), pltpu.SemaphoreType.DMA((n,))],
        name=name)(*halves)


TM = 512


def _row(v):
    return v.reshape(1, -1)


def _ffn_forward(x, gain, wgu4, wdown, tag):
    h = _rmsnorm(x, gain, TM, f"norm_{tag}")
    g, u, act = _ffn_up(h, wgu4, TM, f"ffn_up_{tag}")
    xo = _mm_fwd(act, wdown, x, 0.5, TM, f"ffn_down_{tag}")
    return xo, (x, h, g, u, act)


def _ffn_backward(dx, saved, gain, wgu4, wdown, tag):
    x, h, g, u, act = saved
    dgu = _ffn_dact(dx, wdown, g, u, 256, f"ffn_dact_{tag}")
    d_wdown = _mm_wgrad(act, dx, 0.5, TM, f"ffn_dwdown_{tag}", tn=512)
    d_wgu4 = _mm_wgrad_cols(h, dgu, N_CHIPS, TM, f"ffn_dwgu_{tag}")
    dx, d_gain = _mm_dx_norm_cols(dgu, wgu4, x, gain, dx, TM, f"ffn_dx_{tag}")
    return dx, d_gain, d_wgu4, d_wdown


def _local_step(x, p, target, W):
    depth = p.shape[0]
    saved = []
    for i in range(depth):
        j = i // 2
        s = {}
        x, s["ffn1"] = _ffn_forward(x, W["norm_ffn1"][i], W["w_ffn1_gu"][i], W["w_ffn1_down"][i], f"a{i}")
        s["x_mix"] = x
        hm = _rmsnorm(x, W["norm_mix"][i], TM, f"norm_mix{i}")
        s["hm"] = hm
        if i % 2 == 0:
            qkv = _mm_cols(hm, W["w_qkv"][j], F32, TM, f"qkv{i}")
            qs, kn, vb = _qk_norm(qkv, W["q_norm"][j], W["k_norm"][j], TM, f"qk_norm{i}")
            o = _attn_fwd(qs, kn, vb, f"attn_fwd{i}")
            x = _mm_fwd(o, W["w_o"][j], x, 1.0, TM, f"attn_out{i}")
            s["mix"] = (qkv, qs, kn, vb, o)
        else:
            u = _mm_plain(hm, W["w_pool_in"][j], F32, TM, f"pool_in{i}")
            x, pooled = _pool_fwd(u, W["w_pool_grp"][j], W["pool_scale"][j], x, TM, f"pool_fwd{i}")
            s["mix"] = (pooled,)
        x, s["ffn2"] = _ffn_forward(x, W["norm_ffn2"][i], W["w_ffn2_gu"][i], W["w_ffn2_down"][i], f"b{i}")
        s["x_ple"] = x
        x, hp, gp, pe = _ple_fwd(x, W["norm_ple"][i], W["w_ple_gate"][i], p[i], W["w_ple_proj"][i], TM, f"ple_fwd{i}")
        s["ple"] = (hp, gp, pe)
        saved.append(s)

    sq, dx = _loss_head(x, target, TM, "loss_head")
    G = {k: [None] * len(v) for k, v in W.items()}
    for i in reversed(range(depth)):
        j = i // 2
        s = saved[i]
        hp, gp, pe = s["ple"]
        dx, dgp, dpe, G["norm_ple"][i] = _ple_bwd(dx, gp, pe, W["w_ple_gate"][i], s["x_ple"], W["norm_ple"][i], TM, f"ple_bwd{i}")
        G["w_ple_gate"][i] = _mm_wgrad(hp, dgp, 1.0, TM, f"ple_dwgate{i}")
        G["w_ple_proj"][i] = _mm_wgrad(p[i], dpe, 1.0, TM, f"ple_dwproj{i}")
        dx, G["norm_ffn2"][i], G["w_ffn2_gu"][i], G["w_ffn2_down"][i] = _ffn_backward(
            dx, s["ffn2"], W["norm_ffn2"][i], W["w_ffn2_gu"][i], W["w_ffn2_down"][i], f"b{i}")
        hm = s["hm"]
        if i % 2 == 0:
            qkv, qs, kn, vb, o = s["mix"]
            G["w_o"][j] = _mm_wgrad(o, dx, 1.0, TM, f"attn_dwo{i}")
            D = dx.shape[1]
            do = _mm(dx, W["w_o"][j], [], [jax.ShapeDtypeStruct(dx.shape, BF)], dims=NT, grid=(dx.shape[0] // TM, 1),
                     a_spec=pl.BlockSpec((TM, D), lambda m, k: (m, 0)), b_spec=pl.BlockSpec((D, D), lambda m, k: (0, 0)),
                     extra_specs=[], out_specs=[pl.BlockSpec((TM, D), lambda m, k: (m, 0))], acc_shape=(TM, D),
                     epilogue=_store(BF), name=f"attn_do{i}")[0]
            dqs, dkn, dv = _attn_bwd(qs, kn, vb, o, do, f"attn_bwd{i}")
            dq, dk, dvb, dqg, dkg = _qk_norm_bwd(qkv, dqs, dkn, dv, W["q_norm"][j], W["k_norm"][j], TM, f"qk_norm_bwd{i}")
            dqkv = jnp.concatenate([dq, dk, dvb], axis=1)
            G["q_norm"][j] = dqg[:, :HEAD_DIM] + dqg[:, HEAD_DIM:]
            G["k_norm"][j] = dkg[:, :HEAD_DIM] + dkg[:, HEAD_DIM:]
            G["w_qkv"][j] = _mm_wgrad_cols(hm, dqkv, N_CHIPS, TM, f"attn_dwqkv{i}")
            dx, G["norm_mix"][i] = _mm_dx_norm_cols(dqkv, W["w_qkv"][j], s["x_mix"], W["norm_mix"][i], dx, TM, f"attn_dx{i}")
        else:
            (pooled,) = s["mix"]
            dp, G["w_pool_grp"][j], G["pool_scale"][j] = _pool_bwd_grp(dx, pooled, W["w_pool_grp"][j], W["pool_scale"][j], TM, f"pool_bwd_grp{i}")
            du = _pool_bwd_window(dp, TM, f"pool_bwd_win{i}")
            G["w_pool_in"][j] = _mm_wgrad(hm, du, 1.0, TM, f"pool_dwin{i}")
            dx, G["norm_mix"][i] = _mm_dx_norm(du, W["w_pool_in"][j], s["x_mix"], W["norm_mix"][i], dx, TM, f"pool_dx{i}")
        dx, G["norm_ffn1"][i], G["w_ffn1_gu"][i], G["w_ffn1_down"][i] = _ffn_backward(
            dx, s["ffn1"], W["norm_ffn1"][i], W["w_ffn1_gu"][i], W["w_ffn1_down"][i], f"a{i}")
    return sq, dx, G


SHARDED = ("w_ffn1_gu", "w_ffn1_down", "w_qkv", "w_o", "w_pool_in", "w_pool_grp", "w_ffn2_gu", "w_ffn2_down", "w_ple_gate", "w_ple_proj")
COLUMN_SHARDED = ("w_ffn1_gu", "w_qkv", "w_ffn2_gu", "w_ple_proj")
NORMS = ("norm_ffn1", "norm_mix", "norm_ffn2", "norm_ple")
HEAD_GAINS = ("q_norm", "k_norm")
WEIGHTS = ("norm_ffn1", "w_ffn1_gu", "w_ffn1_down", "norm_mix", "w_qkv", "q_norm", "k_norm", "w_o", "w_pool_in", "w_pool_grp",
           "pool_scale", "norm_ffn2", "w_ffn2_gu", "w_ffn2_down", "norm_ple", "w_ple_gate", "w_ple_proj")
SMALL_ROWS = 24


def _whole_weights(w, gathered, pool_scale_all):
    W = {}
    for k in NORMS:
        W[k] = [_row(w[k][i]) for i in range(w[k].shape[0])]
    for k in HEAD_GAINS:
        W[k] = [_row(jnp.tile(w[k][j], LANES // HEAD_DIM)) for j in range(w[k].shape[0])]
    for k in SHARDED:
        g = gathered[k]
        L = g.shape[1]
        if k == "w_pool_grp":
            W[k] = [jnp.transpose(g[:, i], (1, 0, 2, 3)).reshape(g.shape[2], -1, g.shape[4]) for i in range(L)]
        elif k == "w_ple_proj":
            W[k] = [jnp.transpose(g[:, i], (1, 0, 2)).reshape(g.shape[2], -1) for i in range(L)]
        elif k in COLUMN_SHARDED:
            W[k] = [g[:, i] for i in range(L)]
        else:
            W[k] = [g[:, i].reshape(-1, g.shape[3]) for i in range(L)]
    W["pool_scale"] = [_row(pool_scale_all[j]) for j in range(pool_scale_all.shape[0])]
    return W


def _shard_major(k, parts):
    out = []
    for g in parts:
        if k == "w_pool_grp":
            G_, C, _ = g.shape
            g = jnp.transpose(g.reshape(G_, N_CHIPS, C // N_CHIPS, C), (1, 0, 2, 3))
        elif k == "w_ple_proj":
            P, D = g.shape
            g = jnp.transpose(g.reshape(P, N_CHIPS, D // N_CHIPS), (1, 0, 2))
        elif k not in COLUMN_SHARDED:
            g = g.reshape(N_CHIPS, g.shape[0] // N_CHIPS, g.shape[1])
        out.append(g)
    return jnp.stack(out, axis=1)


def kernel(x, p, norm_ffn1, w_ffn1_gu, w_ffn1_down, norm_mix, w_qkv, q_norm, k_norm, w_o, w_pool_in, w_pool_grp, pool_scale, norm_ffn2, w_ffn2_gu, w_ffn2_down, norm_ple, w_ple_gate, w_ple_proj, loss_target, m_norm_ffn1, m_w_ffn1_gu, m_w_ffn1_down, m_norm_mix, m_w_qkv, m_q_norm, m_k_norm, m_w_o, m_w_pool_in, m_w_pool_grp, m_pool_scale, m_norm_ffn2, m_w_ffn2_gu, m_w_ffn2_down, m_norm_ple, m_w_ple_gate, m_w_ple_proj, v_norm_ffn1, v_w_ffn1_gu, v_w_ffn1_down, v_norm_mix, v_w_qkv, v_q_norm, v_k_norm, v_w_o, v_w_pool_in, v_w_pool_grp, v_pool_scale, v_norm_ffn2, v_w_ffn2_gu, v_w_ffn2_down, v_norm_ple, v_w_ple_gate, v_w_ple_proj):
    w = dict(norm_ffn1=norm_ffn1, w_ffn1_gu=w_ffn1_gu, w_ffn1_down=w_ffn1_down, norm_mix=norm_mix, w_qkv=w_qkv, q_norm=q_norm,
             k_norm=k_norm, w_o=w_o, w_pool_in=w_pool_in, w_pool_grp=w_pool_grp, pool_scale=pool_scale, norm_ffn2=norm_ffn2,
             w_ffn2_gu=w_ffn2_gu, w_ffn2_down=w_ffn2_down, norm_ple=norm_ple, w_ple_gate=w_ple_gate, w_ple_proj=w_ple_proj)
    m = dict(norm_ffn1=m_norm_ffn1, w_ffn1_gu=m_w_ffn1_gu, w_ffn1_down=m_w_ffn1_down, norm_mix=m_norm_mix, w_qkv=m_w_qkv,
             q_norm=m_q_norm, k_norm=m_k_norm, w_o=m_w_o, w_pool_in=m_w_pool_in, w_pool_grp=m_w_pool_grp, pool_scale=m_pool_scale,
             norm_ffn2=m_norm_ffn2, w_ffn2_gu=m_w_ffn2_gu, w_ffn2_down=m_w_ffn2_down, norm_ple=m_norm_ple, w_ple_gate=m_w_ple_gate,
             w_ple_proj=m_w_ple_proj)
    v = dict(norm_ffn1=v_norm_ffn1, w_ffn1_gu=v_w_ffn1_gu, w_ffn1_down=v_w_ffn1_down, norm_mix=v_norm_mix, w_qkv=v_w_qkv,
             q_norm=v_q_norm, k_norm=v_k_norm, w_o=v_w_o, w_pool_in=v_w_pool_in, w_pool_grp=v_w_pool_grp, pool_scale=v_pool_scale,
             norm_ffn2=v_norm_ffn2, w_ffn2_gu=v_w_ffn2_gu, w_ffn2_down=v_w_ffn2_down, norm_ple=v_norm_ple, w_ple_gate=v_w_ple_gate,
             w_ple_proj=v_w_ple_proj)
    chip = 2 * lax.axis_index("x") + lax.axis_index("y")
    D = x.shape[-1]
    shard_cols = pool_scale.shape[1]

    gathered = dict(zip(SHARDED, _gather_weights([w[k].astype(BF) for k in SHARDED], "gather_weights")))
    scale_rows = jnp.zeros((8, shard_cols), F32).at[:pool_scale.shape[0]].set(pool_scale)
    scale_all = _gather_small(scale_rows, "gather_pool_scale")
    pool_scale_all = jnp.transpose(scale_all[::2, :pool_scale.shape[0]], (1, 0, 2)).reshape(pool_scale.shape[0], D)
    W = _whole_weights(w, gathered, pool_scale_all)

    sq, dx, G = _local_step(x[0], p[:, 0], loss_target[0], W)
    loss = lax.psum(0.5 / D * sq[0, 0], ("x", "y", "c"))

    slots = _scatter_grads([_shard_major(k, G[k]) for k in SHARDED], "scatter_grads")
    halves = [_sum_slots(s, f"sum_{k}") for k, s in zip(SHARDED, slots)]
    grads = dict(zip(SHARDED, _join_halves(halves, "join_halves")))

    small = [G[k][i] for k in NORMS for i in range(len(G[k]))]
    small += [jnp.pad(jnp.concatenate(G[k], axis=1), ((0, 0), (0, D - len(G[k]) * HEAD_DIM))) for k in HEAD_GAINS]
    small += G["pool_scale"]
    small = jnp.concatenate(small + [jnp.zeros((SMALL_ROWS - len(small), D), F32)], axis=0)
    small = _sum_slots(_gather_small(small, "gather_small_grads"), "sum_small_grads")
    row = 0
    for k in NORMS:
        grads[k] = small[row:row + w[k].shape[0]]
        row += w[k].shape[0]
    for k in HEAD_GAINS:
        grads[k] = small[row, :w[k].size].reshape(w[k].shape)
        row += 1
    grads["pool_scale"] = lax.dynamic_slice_in_dim(small[row:row + pool_scale.shape[0]], chip * shard_cols, shard_cols, axis=1)

    delta, new_m, new_v = {}, {}, {}
    small_names = NORMS + HEAD_GAINS + ("pool_scale",)

    def pack(d):
        rows = [jnp.pad(d[k].reshape(-1, d[k].shape[-1]) if k in NORMS + ("pool_scale",) else d[k].reshape(1, -1),
                        ((0, 0), (0, D - (d[k].shape[-1] if k in NORMS + ("pool_scale",) else d[k].size))), constant_values=1.0)
                for k in small_names]
        n = sum(r.shape[0] for r in rows)
        return jnp.concatenate(rows + [jnp.ones((SMALL_ROWS - n, D), F32)], axis=0)

    packed = _adamw(pack(w), pack(grads), pack(m), pack(v), "adamw_small")
    row = 0
    for k in small_names:
        n = w[k].shape[0] if k in NORMS + ("pool_scale",) else 1
        width = w[k].shape[-1] if k in NORMS + ("pool_scale",) else w[k].size
        for dst, src in zip((delta, new_m, new_v), packed):
            dst[k] = src[row:row + n, :width].reshape(w[k].shape)
        row += n
    for k in SHARDED:
        delta[k], new_m[k], new_v[k] = _adamw(w[k], grads[k], m[k], v[k], f"adamw_{k}")

    return (loss, dx[None], *[grads[k] for k in WEIGHTS], *[delta[k] for k in WEIGHTS],
            *[new_m[k] for k in WEIGHTS], *[new_v[k] for k in WEIGHTS])
```

```python
import jax
import jax.numpy as jnp
from jax import lax
from jax.experimental import pallas as pl
from jax.experimental.pallas import tpu as pltpu

BF = jnp.bfloat16
F32 = jnp.float32

N_HEADS = 16
HEAD_DIM = 64
POOL_WINDOWS = (2, 4, 8, 16)
POOL_HALO = 16
EPS = 1e-6
ADAM_LR = 0.001
ADAM_B1 = 0.9
ADAM_B2 = 0.999
ADAM_EPS = 1e-08
ADAM_WD = 0.01
ADAM_STEP = 10

N_CHIPS = 4
N_DEV = 8
LANES = 128
VMEM_LIMIT = 56 * 1024 * 1024
ATT_TK = 256
ATT_TQ = 512
MESH = pl.DeviceIdType.MESH
JOIN_CHUNKS = 8

NN = (((1,), (0,)), ((), ()))
NT = (((1,), (1,)), ((), ()))
TN = (((0,), (0,)), ((), ()))


def _params(n_axes):
    return pltpu.CompilerParams(dimension_semantics=("arbitrary",) * n_axes, vmem_limit_bytes=VMEM_LIMIT)


def _dot(a, b, dims):
    return lax.dot_general(a, b, dims, preferred_element_type=F32)


def _mm(a, b, extra, out_shapes, *, dims, grid, a_spec, b_spec, extra_specs, out_specs, acc_shape, epilogue, name):
    nk = grid[-1]
    n_extra = len(extra)
    n_out = len(out_shapes)

    def body(a_ref, b_ref, *rest):
        ex = rest[:n_extra]
        outs = rest[n_extra:n_extra + n_out]
        acc = rest[-1]
        k = pl.program_id(len(grid) - 1)

        @pl.when(k == 0)
        def _():
            acc[...] = jnp.zeros_like(acc)

        acc[...] += _dot(a_ref[...].astype(BF), b_ref[...].astype(BF), dims)
        first = pl.program_id(0) == 0

        @pl.when(k == nk - 1)
        def _():
            epilogue(acc[...], ex, outs, first)

    return pl.pallas_call(
        body, grid=grid, in_specs=[a_spec, b_spec, *extra_specs], out_specs=out_specs, out_shape=out_shapes,
        scratch_shapes=[pltpu.VMEM(acc_shape, F32)], compiler_params=_params(len(grid)), name=name,
    )(a, b, *extra)


def _store(dtype, scale=1.0):
    def ep(acc, ex, outs, first):
        outs[0][...] = (acc * scale).astype(dtype)
    return ep


def _residual(scale):
    def ep(acc, ex, outs, first):
        outs[0][...] = ex[0][...] + scale * acc
    return ep


def _rms_bwd_epilogue(acc, ex, outs, first):
    x_ref, g_ref, dx_ref = ex
    dxo_ref, dg_ref = outs
    x = x_ref[...]
    r = lax.rsqrt(jnp.mean(x * x, axis=-1, keepdims=True) + EPS)
    xh = x * r
    dxh = acc * g_ref[...]
    dxo_ref[...] = dx_ref[...] + r * (dxh - xh * jnp.mean(dxh * xh, axis=-1, keepdims=True))

    @pl.when(first)
    def _():
        dg_ref[...] = jnp.zeros_like(dg_ref)

    dg_ref[...] += jnp.sum(acc * xh, axis=0, keepdims=True)


def _mm_fwd(a, w, x, scale, tm, name):
    S, K = a.shape
    N = w.shape[1]
    return _mm(a, w, [x], [jax.ShapeDtypeStruct((S, N), F32)], dims=NN, grid=(S // tm, 1),
               a_spec=pl.BlockSpec((tm, K), lambda m, k: (m, 0)), b_spec=pl.BlockSpec((K, N), lambda m, k: (0, 0)),
               extra_specs=[pl.BlockSpec((tm, N), lambda m, k: (m, 0))], out_specs=[pl.BlockSpec((tm, N), lambda m, k: (m, 0))],
               acc_shape=(tm, N), epilogue=_residual(scale), name=name)[0]


def _mm_plain(a, w, dtype, tm, name):
    S, K = a.shape
    N = w.shape[1]
    return _mm(a, w, [], [jax.ShapeDtypeStruct((S, N), dtype)], dims=NN, grid=(S // tm, 1),
               a_spec=pl.BlockSpec((tm, K), lambda m, k: (m, 0)), b_spec=pl.BlockSpec((K, N), lambda m, k: (0, 0)),
               extra_specs=[], out_specs=[pl.BlockSpec((tm, N), lambda m, k: (m, 0))],
               acc_shape=(tm, N), epilogue=_store(dtype), name=name)[0]


def _mm_cols(a, w4, dtype, tm, name):
    S, K = a.shape
    nj, _, ns = w4.shape
    return _mm(a, w4, [], [jax.ShapeDtypeStruct((S, nj * ns), dtype)], dims=NN, grid=(S // tm, nj, 1),
               a_spec=pl.BlockSpec((tm, K), lambda m, j, k: (m, 0)), b_spec=pl.BlockSpec((None, K, ns), lambda m, j, k: (j, 0, 0)),
               extra_specs=[], out_specs=[pl.BlockSpec((tm, ns), lambda m, j, k: (m, j))],
               acc_shape=(tm, ns), epilogue=_store(dtype), name=name)[0]


def _mm_wgrad(a, b, scale, tk, name, tn=None):
    S, M = a.shape
    N = b.shape[1]
    tn = N if tn is None else tn
    return _mm(a, b, [], [jax.ShapeDtypeStruct((M, N), BF)], dims=TN, grid=(N // tn, S // tk),
               a_spec=pl.BlockSpec((tk, M), lambda n, k: (k, 0)), b_spec=pl.BlockSpec((tk, tn), lambda n, k: (k, n)),
               extra_specs=[], out_specs=[pl.BlockSpec((M, tn), lambda n, k: (0, n))],
               acc_shape=(M, tn), epilogue=_store(BF, scale), name=name)[0]


def _mm_wgrad_cols(a, b, nj, tk, name):
    S, M = a.shape
    ns = b.shape[1] // nj
    return _mm(a, b, [], [jax.ShapeDtypeStruct((nj, M, ns), BF)], dims=TN, grid=(nj, S // tk),
               a_spec=pl.BlockSpec((tk, M), lambda j, k: (k, 0)), b_spec=pl.BlockSpec((tk, ns), lambda j, k: (k, j)),
               extra_specs=[], out_specs=[pl.BlockSpec((None, M, ns), lambda j, k: (j, 0, 0))],
               acc_shape=(M, ns), epilogue=_store(BF), name=name)[0]


def _mm_dx_norm(a, w, x, gain, dx, tm, name):
    S, K = a.shape
    N = w.shape[0]
    row = pl.BlockSpec((tm, N), lambda m, k: (m, 0))
    vec = pl.BlockSpec((1, N), lambda m, k: (0, 0))
    return _mm(a, w, [x, gain, dx], [jax.ShapeDtypeStruct((S, N), F32), jax.ShapeDtypeStruct((1, N), F32)], dims=NT,
               grid=(S // tm, 1), a_spec=pl.BlockSpec((tm, K), lambda m, k: (m, 0)), b_spec=pl.BlockSpec((N, K), lambda m, k: (0, 0)),
               extra_specs=[row, vec, row], out_specs=[row, vec], acc_shape=(tm, N), epilogue=_rms_bwd_epilogue, name=name)


def _mm_dx_norm_cols(a, w4, x, gain, dx, tm, name):
    S = a.shape[0]
    nj, N, ks = w4.shape
    row = pl.BlockSpec((tm, N), lambda m, j: (m, 0))
    vec = pl.BlockSpec((1, N), lambda m, j: (0, 0))
    return _mm(a, w4, [x, gain, dx], [jax.ShapeDtypeStruct((S, N), F32), jax.ShapeDtypeStruct((1, N), F32)], dims=NT,
               grid=(S // tm, nj), a_spec=pl.BlockSpec((tm, ks), lambda m, j: (m, j)),
               b_spec=pl.BlockSpec((None, N, ks), lambda m, j: (j, 0, 0)),
               extra_specs=[row, vec, row], out_specs=[row, vec], acc_shape=(tm, N), epilogue=_rms_bwd_epilogue, name=name)


def _rmsnorm(x, gain, tm, name):
    S, D = x.shape

    def body(x_ref, g_ref, h_ref):
        xv = x_ref[...]
        r = lax.rsqrt(jnp.mean(xv * xv, axis=-1, keepdims=True) + EPS)
        h_ref[...] = (xv * r * g_ref[...]).astype(BF)

    return pl.pallas_call(
        body, grid=(S // tm,), in_specs=[pl.BlockSpec((tm, D), lambda m: (m, 0)), pl.BlockSpec((1, D), lambda m: (0, 0))],
        out_specs=pl.BlockSpec((tm, D), lambda m: (m, 0)), out_shape=jax.ShapeDtypeStruct((S, D), BF),
        compiler_params=_params(1), name=name)(x, gain)


def _ffn_up(h, wgu4, tm, name):
    S, D = h.shape
    ns = wgu4.shape[2]
    half = wgu4.shape[0] // 2

    def body(h_ref, wg_ref, wu_ref, g_ref, u_ref, act_ref):
        hv = h_ref[...]
        g = _dot(hv, wg_ref[...], NN)
        u = _dot(hv, wu_ref[...], NN)
        g_ref[...] = g.astype(BF)
        u_ref[...] = u.astype(BF)
        act_ref[...] = (g * jax.nn.sigmoid(g) * u).astype(BF)

    out = jax.ShapeDtypeStruct((S, half * ns), BF)
    tile = pl.BlockSpec((tm, ns), lambda j, m: (m, j))
    return pl.pallas_call(
        body, grid=(half, S // tm),
        in_specs=[pl.BlockSpec((tm, D), lambda j, m: (m, 0)), pl.BlockSpec((None, D, ns), lambda j, m: (j, 0, 0)),
                  pl.BlockSpec((None, D, ns), lambda j, m: (j + half, 0, 0))],
        out_specs=[tile, tile, tile], out_shape=[out, out, out], compiler_params=_params(2), name=name)(h, wgu4, wgu4)


def _ffn_dact(dx, wdown, g, u, tm, name):
    S, D = dx.shape
    F = wdown.shape[0]

    def ep(acc, ex, outs, first):
        gv = ex[0][...].astype(F32)
        uv = ex[1][...].astype(F32)
        da = 0.5 * acc
        sg = jax.nn.sigmoid(gv)
        outs[0][:, :F] = (da * uv * (sg * (1.0 + gv * (1.0 - sg)))).astype(BF)
        outs[0][:, F:] = (da * (gv * sg)).astype(BF)

    row = pl.BlockSpec((tm, F), lambda m, k: (m, 0))
    return _mm(dx, wdown, [g, u], [jax.ShapeDtypeStruct((S, 2 * F), BF)], dims=NT, grid=(S // tm, 1),
               a_spec=pl.BlockSpec((tm, D), lambda m, k: (m, 0)), b_spec=pl.BlockSpec((F, D), lambda m, k: (0, 0)),
               extra_specs=[row, row], out_specs=[pl.BlockSpec((tm, 2 * F), lambda m, k: (m, 0))],
               acc_shape=(tm, F), epilogue=ep, name=name)[0]


def _ple_fwd(x, gain, wgate, p, wproj, tm, name):
    S, D = x.shape
    P = p.shape[1]

    def body(x_ref, g_ref, wg_ref, p_ref, wp_ref, xo_ref, hp_ref, gp_ref, pe_ref):
        xv = x_ref[...]
        r = lax.rsqrt(jnp.mean(xv * xv, axis=-1, keepdims=True) + EPS)
        hp = (xv * r * g_ref[...]).astype(BF)
        gp = _dot(hp, wg_ref[...], NN)
        pe = _dot(p_ref[...].astype(BF), wp_ref[...], NN)
        xo_ref[...] = xv + jax.nn.sigmoid(gp) * pe
        hp_ref[...] = hp
        gp_ref[...] = gp.astype(BF)
        pe_ref[...] = pe.astype(BF)

    row = pl.BlockSpec((tm, D), lambda m: (m, 0))
    return pl.pallas_call(
        body, grid=(S // tm,),
        in_specs=[row, pl.BlockSpec((1, D), lambda m: (0, 0)), pl.BlockSpec((D, D), lambda m: (0, 0)),
                  pl.BlockSpec((tm, P), lambda m: (m, 0)), pl.BlockSpec((P, D), lambda m: (0, 0))],
        out_specs=[row, row, row, row],
        out_shape=[jax.ShapeDtypeStruct((S, D), F32)] + [jax.ShapeDtypeStruct((S, D), BF)] * 3,
        compiler_params=_params(1), name=name)(x, gain, wgate, p, wproj)


def _ple_bwd(dx, gp, pe, wgate, x, gain, tm, name):
    S, D = dx.shape

    def body(dx_ref, gp_ref, pe_ref, wg_ref, x_ref, g_ref, dxo_ref, dgp_ref, dpe_ref, dg_ref):
        dxv = dx_ref[...]
        sg = jax.nn.sigmoid(gp_ref[...].astype(F32))
        dgp = (dxv * pe_ref[...].astype(F32) * (sg * (1.0 - sg))).astype(BF)
        dgp_ref[...] = dgp
        dpe_ref[...] = (dxv * sg).astype(BF)
        dhp = _dot(dgp, wg_ref[...], NT)
        _rms_bwd_epilogue(dhp, (x_ref, g_ref, dx_ref), (dxo_ref, dg_ref), pl.program_id(0) == 0)

    row = pl.BlockSpec((tm, D), lambda m: (m, 0))
    vec = pl.BlockSpec((1, D), lambda m: (0, 0))
    return pl.pallas_call(
        body, grid=(S // tm,), in_specs=[row, row, row, pl.BlockSpec((D, D), lambda m: (0, 0)), row, vec],
        out_specs=[row, row, row, vec],
        out_shape=[jax.ShapeDtypeStruct((S, D), F32), jax.ShapeDtypeStruct((S, D), BF), jax.ShapeDtypeStruct((S, D), BF),
                   jax.ShapeDtypeStruct((1, D), F32)],
        compiler_params=_params(1), name=name)(dx, gp, pe, wgate, x, gain)


def _loss_head(y, target, tm, name):
    S, D = y.shape

    def body(y_ref, t_ref, sq_ref, dy_ref):
        d = y_ref[...] - t_ref[...]
        dy_ref[...] = d * (1.0 / D)

        @pl.when(pl.program_id(0) == 0)
        def _():
            sq_ref[...] = jnp.zeros_like(sq_ref)

        sq_ref[...] += jnp.sum(d * d, keepdims=True)

    row = pl.BlockSpec((tm, D), lambda m: (m, 0))
    return pl.pallas_call(
        body, grid=(S // tm,), in_specs=[row, row], out_specs=[pl.BlockSpec((1, 1), lambda m: (0, 0)), row],
        out_shape=[jax.ShapeDtypeStruct((1, 1), F32), jax.ShapeDtypeStruct((S, D), F32)],
        compiler_params=_params(1), name=name)(y, target)


def _qk_norm(qkv, qgain, kgain, tm, name):
    S = qkv.shape[0]
    D = qkv.shape[1] // 3
    nb = D // LANES

    def norm2(t, gain):
        lo = lax.broadcasted_iota(jnp.int32, t.shape, 1) < HEAD_DIM
        sq = t * t
        s_lo = jnp.sum(jnp.where(lo, sq, 0.0), axis=1, keepdims=True)
        s_hi = jnp.sum(jnp.where(lo, 0.0, sq), axis=1, keepdims=True)
        r = lax.rsqrt(jnp.where(lo, s_lo, s_hi) * (1.0 / HEAD_DIM) + EPS)
        return t * r * gain

    def body(q_ref, k_ref, v_ref, qg_ref, kg_ref, qo_ref, ko_ref, vo_ref):
        qo_ref[...] = (norm2(q_ref[...], qg_ref[...]) * (HEAD_DIM ** -0.5)).astype(BF)
        ko_ref[...] = norm2(k_ref[...], kg_ref[...]).astype(BF)
        vo_ref[...] = v_ref[...].astype(BF)

    vec = pl.BlockSpec((1, LANES), lambda m, h: (0, 0))
    tile = pl.BlockSpec((tm, LANES), lambda m, h: (m, h))
    out = jax.ShapeDtypeStruct((S, D), BF)
    return pl.pallas_call(
        body, grid=(S // tm, nb),
        in_specs=[tile, pl.BlockSpec((tm, LANES), lambda m, h: (m, nb + h)), pl.BlockSpec((tm, LANES), lambda m, h: (m, 2 * nb + h)), vec, vec],
        out_specs=[tile, tile, tile], out_shape=[out, out, out], compiler_params=_params(2), name=name)(qkv, qkv, qkv, qgain, kgain)


def _qk_norm_bwd(qkv, dqs, dkn, dv, qgain, kgain, tm, name):
    S = qkv.shape[0]
    D = qkv.shape[1] // 3
    nb = D // LANES

    def norm2_bwd(t, gain, dn):
        lo = lax.broadcasted_iota(jnp.int32, t.shape, 1) < HEAD_DIM

        def headsum(val):
            s_lo = jnp.sum(jnp.where(lo, val, 0.0), axis=1, keepdims=True)
            s_hi = jnp.sum(jnp.where(lo, 0.0, val), axis=1, keepdims=True)
            return jnp.where(lo, s_lo, s_hi)

        r = lax.rsqrt(headsum(t * t) * (1.0 / HEAD_DIM) + EPS)
        th = t * r
        dth = dn * gain
        dt = r * (dth - th * (headsum(dth * th) * (1.0 / HEAD_DIM)))
        return dt, jnp.sum(dn * th, axis=0, keepdims=True)

    def body(q_ref, k_ref, dq_ref, dk_ref, dv_ref, qg_ref, kg_ref, dqo_ref, dko_ref, dvo_ref, dqg_ref, dkg_ref):
        dq, dqg = norm2_bwd(q_ref[...], qg_ref[...], dq_ref[...] * (HEAD_DIM ** -0.5))
        dk, dkg = norm2_bwd(k_ref[...], kg_ref[...], dk_ref[...])
        dqo_ref[...] = dq.astype(BF)
        dko_ref[...] = dk.astype(BF)
        dvo_ref[...] = dv_ref[...].astype(BF)

        @pl.when((pl.program_id(0) == 0) & (pl.program_id(1) == 0))
        def _():
            dqg_ref[...] = jnp.zeros_like(dqg_ref)
            dkg_ref[...] = jnp.zeros_like(dkg_ref)

        dqg_ref[...] += dqg
        dkg_ref[...] += dkg

    vec = pl.BlockSpec((1, LANES), lambda m, h: (0, 0))
    tile = pl.BlockSpec((tm, LANES), lambda m, h: (m, h))
    return pl.pallas_call(
        body, grid=(S // tm, nb),
        in_specs=[tile, pl.BlockSpec((tm, LANES), lambda m, h: (m, nb + h)), tile, tile, tile, vec, vec],
        out_specs=[tile, tile, tile, vec, vec],
        out_shape=[jax.ShapeDtypeStruct((S, D), BF)] * 3 + [jax.ShapeDtypeStruct((1, LANES), F32)] * 2,
        compiler_params=_params(2), name=name)(qkv, qkv, dqs, dkn, dv, qgain, kgain)


def _split_bf16(t):
    hi = t.astype(BF)
    return hi, (t - hi.astype(F32)).astype(BF)


HEADS_PER_BLOCK = LANES // HEAD_DIM


def _suffix_sums(t, from_here):
    hi, lo = _split_bf16(t)
    return _dot(hi, from_here, NN) + _dot(lo, from_here, NN)


def _attn_tiles(qs, ks, stays, valid, from_here):
    zs = [_dot(q, k, NT) for q, k in zip(qs, ks)]
    sps = [jnp.maximum(z, 0.0) + jnp.log(1.0 + jnp.exp(-jnp.abs(z))) for z in zs]
    if valid is not None:
        sps = [jnp.where(valid, sp, 0.0) for sp in sps]
    sums = [_suffix_sums(sp, from_here) for sp in sps]
    ws = [jnp.exp(z - s + stay) for z, s, stay in zip(zs, sums, stays)]
    if valid is not None:
        ws = [jnp.where(valid, w, 0.0) for w in ws]
    return zs, sps, sums, ws


def _attn_walk(qi, tq, tk, block, carry):
    n_diag = tq // tk
    row = lax.broadcasted_iota(jnp.int32, (tq, tk), 0)
    col = lax.broadcasted_iota(jnp.int32, (tq, tk), 1)
    for d in reversed(range(n_diag)):
        carry = block(pl.multiple_of((n_diag * qi + d) * tk, tk), col < row - d * tk, carry)
    return lax.fori_loop(0, n_diag * qi, lambda i, c: block(pl.multiple_of((n_diag * qi - 1 - i) * tk, tk), None, c), carry)


def _from_here(tk):
    return (lax.broadcasted_iota(jnp.int32, (tk, tk), 0) >= lax.broadcasted_iota(jnp.int32, (tk, tk), 1)).astype(BF)


def _attn_fwd(qs, kn, vb, name):
    S, D = qs.shape
    TQ, TK = ATT_TQ, ATT_TK
    heads = [slice(hh * HEAD_DIM, (hh + 1) * HEAD_DIM) for hh in range(HEADS_PER_BLOCK)]

    def body(q_ref, k_ref, v_ref, o_ref):
        from_here = _from_here(TK)
        q = [q_ref[:, lanes] for lanes in heads]

        def block(start, valid, carry):
            rows = pl.ds(start, TK)
            _, _, sums, ws = _attn_tiles(q, [k_ref[rows, lanes] for lanes in heads], [c[1] for c in carry], valid, from_here)
            outs = [_dot(w.astype(BF), v_ref[rows, lanes], NN) for w, lanes in zip(ws, heads)]
            return tuple((c[0] + o, c[1] - s[:, :1]) for c, o, s in zip(carry, outs, sums))

        carry = tuple((jnp.zeros((TQ, HEAD_DIM), F32), jnp.zeros((TQ, 1), F32)) for _ in heads)
        carry = _attn_walk(pl.program_id(1), TQ, TK, block, carry)
        for hh, lanes in enumerate(heads):
            o_ref[:, lanes] = carry[hh][0]

    tile = pl.BlockSpec((TQ, LANES), lambda h, m: (m, h))
    full = pl.BlockSpec((S, LANES), lambda h, m: (0, h))
    return pl.pallas_call(
        body, grid=(D // LANES, S // TQ), in_specs=[tile, full, full], out_specs=tile,
        out_shape=jax.ShapeDtypeStruct((S, D), F32), compiler_params=_params(2), name=name)(qs, kn, vb)


def _attn_bwd(qs, kn, vb, o, do, name):
    S, D = qs.shape
    TQ, TK = ATT_TQ, ATT_TK
    heads = [slice(hh * HEAD_DIM, (hh + 1) * HEAD_DIM) for hh in range(HEADS_PER_BLOCK)]

    def body(q_ref, k_ref, v_ref, o_ref, do_ref, dq_ref, dk_ref, dv_ref):
        @pl.when(pl.program_id(1) == 0)
        def _():
            dk_ref[...] = jnp.zeros_like(dk_ref)
            dv_ref[...] = jnp.zeros_like(dv_ref)

        from_here = _from_here(TK)
        q = [q_ref[:, lanes] for lanes in heads]
        dout = [do_ref[:, lanes] for lanes in heads]

        def block(start, valid, carry):
            rows = pl.ds(start, TK)
            ks = [k_ref[rows, lanes] for lanes in heads]
            zs, sps, sums, ws = _attn_tiles(q, ks, [c[1] for c in carry], valid, from_here)
            wbs = [w.astype(BF) for w in ws]
            das = [_dot(d, v_ref[rows, lanes], NT) for d, lanes in zip(dout, heads)]
            gs = [wb.astype(F32) * da for wb, da in zip(wbs, das)]
            g_sums = [_suffix_sums(g, from_here) for g in gs]
            dzs = [g - jnp.exp(z - sp) * (g + (c[2] - gsum)) for g, z, sp, c, gsum in zip(gs, zs, sps, carry, g_sums)]
            if valid is not None:
                dzs = [jnp.where(valid, dz, 0.0) for dz in dzs]
            dzbs = [dz.astype(BF) for dz in dzs]
            for hh, lanes in enumerate(heads):
                dk_ref[rows, lanes] += _dot(dzbs[hh], q[hh], TN)
                dv_ref[rows, lanes] += _dot(wbs[hh], dout[hh], TN)
            return tuple((c[0] + _dot(dzb, k, NN), c[1] - s[:, :1], c[2] - gsum[:, :1])
                         for c, dzb, k, s, gsum in zip(carry, dzbs, ks, sums, g_sums))

        carry = tuple((jnp.zeros((TQ, HEAD_DIM), F32), jnp.zeros((TQ, 1), F32),
                       jnp.sum(dout[hh].astype(F32) * o_ref[:, lanes], axis=1, keepdims=True)) for hh, lanes in enumerate(heads))
        carry = _attn_walk(pl.program_id(1), TQ, TK, block, carry)
        for hh, lanes in enumerate(heads):
            dq_ref[:, lanes] = carry[hh][0]

    tile = pl.BlockSpec((TQ, LANES), lambda h, m: (m, h))
    full = pl.BlockSpec((S, LANES), lambda h, m: (0, h))
    out = jax.ShapeDtypeStruct((S, D), F32)
    return pl.pallas_call(
        body, grid=(D // LANES, S // TQ), in_specs=[tile, full, full, tile, tile], out_specs=[tile, full, full],
        out_shape=[out, out, out], compiler_params=_params(2), name=name)(qs, kn, vb, o, do)


def _pool_counts(T, first_row):
    pos = first_row + lax.broadcasted_iota(jnp.int32, (T, 1), 0)
    return [jnp.minimum(pos + 1, w).astype(F32) for w in POOL_WINDOWS]


def _pool_fwd(u, wgrp, scale, x, tm, name):
    S, D = u.shape
    G = len(POOL_WINDOWS)
    C = D // G
    H = POOL_HALO

    def body(u_ref, prev_ref, w_ref, s_ref, x_ref, xo_ref, pooled_ref):
        m = pl.program_id(0)
        prev = jnp.where(m == 0, 0.0, prev_ref[...])
        ext = jnp.concatenate([prev, u_ref[...]], axis=0)
        counts = _pool_counts(tm, m * tm)
        ys = []
        acc = ext
        shift = 1
        for gi, w in enumerate(POOL_WINDOWS):
            while shift < w:
                acc = acc + pltpu.roll(acc, shift, axis=0)
                shift *= 2
            cols = slice(gi * C, (gi + 1) * C)
            pooled = (acc[H:, cols] / counts[gi] - ext[H:, cols]).astype(BF)
            pooled_ref[:, cols] = pooled
            ys.append(_dot(pooled, w_ref[gi], NN))
        xo_ref[...] = x_ref[...] + jnp.concatenate(ys, axis=1) * s_ref[...]

    row = pl.BlockSpec((tm, D), lambda m: (m, 0))
    return pl.pallas_call(
        body, grid=(S // tm,),
        in_specs=[row, pl.BlockSpec((H, D), lambda m: (jnp.maximum(m * (tm // H) - 1, 0), 0)),
                  pl.BlockSpec((G, C, C), lambda m: (0, 0, 0)), pl.BlockSpec((1, D), lambda m: (0, 0)), row],
        out_specs=[row, row], out_shape=[jax.ShapeDtypeStruct((S, D), F32), jax.ShapeDtypeStruct((S, D), BF)],
        compiler_params=_params(1), name=name)(u, u, wgrp, scale, x)


def _pool_bwd_grp(dx, pooled, wgrp, scale, tm, name):
    S, D = dx.shape
    G = len(POOL_WINDOWS)
    C = D // G

    def body(dx_ref, pooled_ref, w_ref, s_ref, dp_ref, dw_ref, ds_ref, dw_acc):
        m = pl.program_id(0)

        @pl.when(m == 0)
        def _():
            dw_acc[...] = jnp.zeros_like(dw_acc)
            ds_ref[...] = jnp.zeros_like(ds_ref)

        dxv = dx_ref[...]
        dy = (dxv * s_ref[...]).astype(BF)
        ys = []
        for gi in range(G):
            cols = slice(gi * C, (gi + 1) * C)
            pg = pooled_ref[:, cols]
            ys.append(_dot(pg, w_ref[gi], NN))
            dw_acc[gi] += _dot(pg, dy[:, cols], TN)
            dp_ref[:, cols] = _dot(dy[:, cols], w_ref[gi], NT)
        ds_ref[...] += jnp.sum(dxv * jnp.concatenate(ys, axis=1), axis=0, keepdims=True)

        @pl.when(m == S // tm - 1)
        def _():
            dw_ref[...] = dw_acc[...].astype(BF)

    row = pl.BlockSpec((tm, D), lambda m: (m, 0))
    wspec = pl.BlockSpec((G, C, C), lambda m: (0, 0, 0))
    vec = pl.BlockSpec((1, D), lambda m: (0, 0))
    return pl.pallas_call(
        body, grid=(S // tm,), in_specs=[row, row, wspec, vec], out_specs=[row, wspec, vec],
        out_shape=[jax.ShapeDtypeStruct((S, D), F32), jax.ShapeDtypeStruct((G, C, C), BF), jax.ShapeDtypeStruct((1, D), F32)],
        scratch_shapes=[pltpu.VMEM((G, C, C), F32)], compiler_params=_params(1), name=name)(dx, pooled, wgrp, scale)


def _pool_bwd_window(dp, tm, name):
    S, D = dp.shape
    G = len(POOL_WINDOWS)
    C = D // G
    H = POOL_HALO
    last = S // tm - 1

    def body(dp_ref, next_ref, du_ref):
        m = pl.program_id(0)
        counts = _pool_counts(tm + H, m * tm)
        nxt = jnp.where(m == last, 0.0, next_ref[...])
        ext = jnp.concatenate([dp_ref[...], nxt], axis=0)
        for gi, w in enumerate(POOL_WINDOWS):
            cols = slice(gi * C, (gi + 1) * C)
            acc = ext[:, cols] / counts[gi]
            shift = 1
            while shift < w:
                acc = acc + pltpu.roll(acc, tm + H - shift, axis=0)
                shift *= 2
            du_ref[:, cols] = (acc[:tm] - ext[:tm, cols]).astype(BF)

    row = pl.BlockSpec((tm, D), lambda m: (m, 0))
    return pl.pallas_call(
        body, grid=(S // tm,),
        in_specs=[row, pl.BlockSpec((H, D), lambda m: (jnp.minimum((m + 1) * (tm // H), S // H - 1), 0))],
        out_specs=row, out_shape=jax.ShapeDtypeStruct((S, D), BF), compiler_params=_params(1), name=name)(dp, dp)


ELEMENTWISE_TILE_BYTES = 1 << 20


def _row_tile(rows, row_bytes):
    for cand in (512, 256, 128, 64, 32, 16, 8):
        if rows % cand == 0 and cand * row_bytes <= ELEMENTWISE_TILE_BYTES:
            return cand
    return rows


def _adamw(w, g, m, v, name):
    shape = w.shape
    cols = shape[-1]
    rows = w.size // cols
    tr = _row_tile(rows, cols * 4)

    def body(w_ref, g_ref, m_ref, v_ref, d_ref, mo_ref, vo_ref):
        gv = g_ref[...]
        mn = ADAM_B1 * m_ref[...] + (1.0 - ADAM_B1) * gv
        vn = ADAM_B2 * v_ref[...] + (1.0 - ADAM_B2) * jnp.square(gv)
        m_hat = mn / (1.0 - ADAM_B1 ** ADAM_STEP)
        v_hat = vn / (1.0 - ADAM_B2 ** ADAM_STEP)
        d_ref[...] = -ADAM_LR * (m_hat / (jnp.sqrt(v_hat) + ADAM_EPS) + ADAM_WD * w_ref[...])
        mo_ref[...] = mn
        vo_ref[...] = vn

    tile = pl.BlockSpec((tr, cols), lambda i: (i, 0))
    out = jax.ShapeDtypeStruct((rows, cols), F32)
    res = pl.pallas_call(
        body, grid=(rows // tr,), in_specs=[tile] * 4, out_specs=[tile] * 3, out_shape=[out] * 3,
        compiler_params=_params(1), name=name)(*[t.reshape(rows, cols) for t in (w, g, m, v)])
    return [t.reshape(shape) for t in res]


def _sum_slots(r, name):
    n = r.shape[0]
    shape = r.shape[1:]
    cols = shape[-1]
    rows = r.size // (n * cols)
    tr = _row_tile(rows, n * cols * r.dtype.itemsize)

    def body(r_ref, o_ref):
        acc = r_ref[0].astype(F32)
        for d in range(1, n):
            acc = acc + r_ref[d].astype(F32)
        o_ref[...] = acc

    return pl.pallas_call(
        body, grid=(rows // tr,), in_specs=[pl.BlockSpec((n, tr, cols), lambda i: (0, i, 0))],
        out_specs=pl.BlockSpec((tr, cols), lambda i: (i, 0)), out_shape=jax.ShapeDtypeStruct((rows, cols), F32),
        compiler_params=_params(1), name=name)(r.reshape(n, rows, cols)).reshape(shape)


ANY = pl.BlockSpec(memory_space=pl.ANY)


def _position():
    return lax.axis_index("x"), lax.axis_index("y"), lax.axis_index("c")


def _gather_small(t, name):
    rows, cols = t.shape

    def body(t_ref, o_ref, send_sems, recv_sems):
        x, y, c = _position()
        me = 4 * x + 2 * y + c
        o_ref[me] = t_ref[...]
        copies = []
        for k in range(1, N_DEV):
            peer = (x ^ (k >> 2), y ^ ((k >> 1) & 1), c ^ (k & 1))
            cp = pltpu.make_async_remote_copy(src_ref=t_ref, dst_ref=o_ref.at[me], send_sem=send_sems.at[k - 1],
                                              recv_sem=recv_sems.at[k - 1], device_id=peer, device_id_type=MESH)
            cp.start()
            copies.append((cp, 4 * peer[0] + 2 * peer[1] + peer[2]))
        for k, (cp, src) in enumerate(copies):
            pltpu.make_async_remote_copy(src_ref=t_ref, dst_ref=o_ref.at[src], send_sem=send_sems.at[k], recv_sem=recv_sems.at[k],
                                         device_id=(x, y, c), device_id_type=MESH).wait_recv()
        for cp, _ in copies:
            cp.wait_send()

    return pl.pallas_call(
        body, in_specs=[pl.BlockSpec(memory_space=pltpu.VMEM)], out_specs=pl.BlockSpec(memory_space=pltpu.VMEM),
        out_shape=jax.ShapeDtypeStruct((N_DEV, rows, cols), F32),
        scratch_shapes=[pltpu.SemaphoreType.DMA((N_DEV - 1,)), pltpu.SemaphoreType.DMA((N_DEV - 1,))], name=name)(t)


def _gather_weights(shards, name):
    n = len(shards)

    def body(*refs):
        ins, outs = refs[:n], refs[n:2 * n]
        send_sems, recv_sems, local_sems = refs[2 * n:]
        x, y, c = _position()
        chip = 2 * x + y
        chips = [(1 - x, y), (x, 1 - y), (1 - x, 1 - y)]
        local = []
        sends = []
        for p in range(n):
            cp = pltpu.make_async_copy(ins[p], outs[p].at[chip], local_sems.at[p])
            cp.start()
            local.append(cp)
            for k, (px, py) in enumerate(chips):
                cp = pltpu.make_async_remote_copy(src_ref=ins[p], dst_ref=outs[p].at[chip], send_sem=send_sems.at[p, k],
                                                  recv_sem=recv_sems.at[p, k], device_id=(px, py, c), device_id_type=MESH)
                cp.start()
                sends.append(cp)
        for p in range(n):
            for k, (px, py) in enumerate(chips):
                pltpu.make_async_remote_copy(src_ref=ins[p], dst_ref=outs[p].at[2 * px + py], send_sem=send_sems.at[p, k],
                                             recv_sem=recv_sems.at[p, k], device_id=(x, y, c), device_id_type=MESH).wait_recv()
        for cp in sends:
            cp.wait_send()
        for cp in local:
            cp.wait()

    return pl.pallas_call(
        body, in_specs=[ANY] * n, out_specs=[ANY] * n,
        out_shape=[jax.ShapeDtypeStruct((N_CHIPS,) + s.shape, s.dtype) for s in shards],
        scratch_shapes=[pltpu.SemaphoreType.DMA((n, 3)), pltpu.SemaphoreType.DMA((n, 3)), pltpu.SemaphoreType.DMA((n,))],
        name=name)(*shards)


def _scatter_grads(grads, name):
    n = len(grads)

    def body(*refs):
        ins, outs = refs[:n], refs[n:2 * n]
        send_sems, recv_sems, local_sems = refs[2 * n:]
        x, y, c = _position()
        me = 4 * x + 2 * y + c
        local = []
        sends = []
        for p in range(n):
            half = grads[p].shape[1] // 2

            def piece(px, py, pc, p=p, half=half):
                return ins[p].at[2 * px + py, pl.ds(pc * half, half)]

            cp = pltpu.make_async_copy(piece(x, y, c), outs[p].at[me], local_sems.at[p])
            cp.start()
            local.append(cp)
            for k in range(1, N_DEV):
                peer = (x ^ (k >> 2), y ^ ((k >> 1) & 1), c ^ (k & 1))
                cp = pltpu.make_async_remote_copy(src_ref=piece(*peer), dst_ref=outs[p].at[me], send_sem=send_sems.at[p, k - 1],
                                                  recv_sem=recv_sems.at[p, k - 1], device_id=peer, device_id_type=MESH)
                cp.start()
                sends.append(cp)
        for p in range(n):
            half = grads[p].shape[1] // 2
            for k in range(1, N_DEV):
                src = 4 * (x ^ (k >> 2)) + 2 * (y ^ ((k >> 1) & 1)) + (c ^ (k & 1))
                pltpu.make_async_remote_copy(src_ref=ins[p].at[0, pl.ds(0, half)], dst_ref=outs[p].at[src],
                                             send_sem=send_sems.at[p, k - 1], recv_sem=recv_sems.at[p, k - 1],
                                             device_id=(x, y, c), device_id_type=MESH).wait_recv()
        for cp in sends:
            cp.wait_send()
        for cp in local:
            cp.wait()

    return pl.pallas_call(
        body, in_specs=[ANY] * n, out_specs=[ANY] * n,
        out_shape=[jax.ShapeDtypeStruct((N_DEV, g.shape[1] // 2) + g.shape[2:], g.dtype) for g in grads],
        scratch_shapes=[pltpu.SemaphoreType.DMA((n, N_DEV - 1)), pltpu.SemaphoreType.DMA((n, N_DEV - 1)), pltpu.SemaphoreType.DMA((n,))],
        name=name)(*grads)


def _join_halves(halves, name):
    n = len(halves)
    flat = [h.reshape(-1, h.shape[-1]) for h in halves]

    def body(*refs):
        ins, outs = refs[:n], refs[n:2 * n]
        send_sems, recv_sems, local_sems = refs[2 * n:]
        x, y, c = _position()
        for p in range(n):
            rows = flat[p].shape[0]
            step = rows // JOIN_CHUNKS if rows % JOIN_CHUNKS == 0 else rows
            for r0 in range(0, rows, step):
                src = ins[p].at[pl.ds(r0, step)]
                dst = outs[p].at[pl.ds(c * rows + r0, step)]
                pltpu.make_async_copy(src, dst, local_sems.at[p]).start()
                pltpu.make_async_remote_copy(src_ref=src, dst_ref=dst, send_sem=send_sems.at[p], recv_sem=recv_sems.at[p],
                                             device_id=(x, y, 1 - c), device_id_type=MESH).start()
        for p in range(n):
            rows = flat[p].shape[0]
            mine = outs[p].at[pl.ds(c * rows, rows)]
            theirs = outs[p].at[pl.ds((1 - c) * rows, rows)]
            whole = pltpu.make_async_remote_copy(src_ref=ins[p], dst_ref=theirs, send_sem=send_sems.at[p], recv_sem=recv_sems.at[p],
                                                 device_id=(x, y, c), device_id_type=MESH)
            whole.wait_recv()
            whole.wait_send()
            pltpu.make_async_copy(ins[p], mine, local_sems.at[p]).wait()

    outs = pl.pallas_call(
        body, in_specs=[ANY] * n, out_specs=[ANY] * n,
        out_shape=[jax.ShapeDtypeStruct((2 * f.shape[0], f.shape[1]), f.dtype) for f in flat],
        scratch_shapes=[pltpu.SemaphoreType.DMA((n,)), pltpu.SemaphoreType.DMA((n,)), pltpu.SemaphoreType.DMA((n,))],
        name=name)(*flat)
    return [o.reshape((2 * h.shape[0],) + h.shape[1:]) for o, h in zip(outs, halves)]


TM = 512


def _row(v):
    return v.reshape(1, -1)


def _ffn_forward(x, gain, wgu4, wdown, tag):
    h = _rmsnorm(x, gain, TM, f"norm_{tag}")
    g, u, act = _ffn_up(h, wgu4, TM, f"ffn_up_{tag}")
    xo = _mm_fwd(act, wdown, x, 0.5, TM, f"ffn_down_{tag}")
    return xo, (x, h, g, u, act)


def _ffn_backward(dx, saved, gain, wgu4, wdown, tag):
    x, h, g, u, act = saved
    dgu = _ffn_dact(dx, wdown, g, u, 256, f"ffn_dact_{tag}")
    d_wdown = _mm_wgrad(act, dx, 0.5, TM, f"ffn_dwdown_{tag}", tn=512)
    d_wgu4 = _mm_wgrad_cols(h, dgu, N_CHIPS, TM, f"ffn_dwgu_{tag}")
    dx, d_gain = _mm_dx_norm_cols(dgu, wgu4, x, gain, dx, TM, f"ffn_dx_{tag}")
    return dx, d_gain, d_wgu4, d_wdown


def _local_step(x, p, target, W):
    depth = p.shape[0]
    saved = []
    for i in range(depth):
        j = i // 2
        s = {}
        x, s["ffn1"] = _ffn_forward(x, W["norm_ffn1"][i], W["w_ffn1_gu"][i], W["w_ffn1_down"][i], f"a{i}")
        s["x_mix"] = x
        hm = _rmsnorm(x, W["norm_mix"][i], TM, f"norm_mix{i}")
        s["hm"] = hm
        if i % 2 == 0:
            qkv = _mm_cols(hm, W["w_qkv"][j], F32, TM, f"qkv{i}")
            qs, kn, vb = _qk_norm(qkv, W["q_norm"][j], W["k_norm"][j], TM, f"qk_norm{i}")
            o = _attn_fwd(qs, kn, vb, f"attn_fwd{i}")
            x = _mm_fwd(o, W["w_o"][j], x, 1.0, TM, f"attn_out{i}")
            s["mix"] = (qkv, qs, kn, vb, o)
        else:
            u = _mm_plain(hm, W["w_pool_in"][j], F32, TM, f"pool_in{i}")
            x, pooled = _pool_fwd(u, W["w_pool_grp"][j], W["pool_scale"][j], x, TM, f"pool_fwd{i}")
            s["mix"] = (pooled,)
        x, s["ffn2"] = _ffn_forward(x, W["norm_ffn2"][i], W["w_ffn2_gu"][i], W["w_ffn2_down"][i], f"b{i}")
        s["x_ple"] = x
        x, hp, gp, pe = _ple_fwd(x, W["norm_ple"][i], W["w_ple_gate"][i], p[i], W["w_ple_proj"][i], TM, f"ple_fwd{i}")
        s["ple"] = (hp, gp, pe)
        saved.append(s)

    sq, dx = _loss_head(x, target, TM, "loss_head")
    G = {k: [None] * len(v) for k, v in W.items()}
    for i in reversed(range(depth)):
        j = i // 2
        s = saved[i]
        hp, gp, pe = s["ple"]
        dx, dgp, dpe, G["norm_ple"][i] = _ple_bwd(dx, gp, pe, W["w_ple_gate"][i], s["x_ple"], W["norm_ple"][i], TM, f"ple_bwd{i}")
        G["w_ple_gate"][i] = _mm_wgrad(hp, dgp, 1.0, TM, f"ple_dwgate{i}")
        G["w_ple_proj"][i] = _mm_wgrad(p[i], dpe, 1.0, TM, f"ple_dwproj{i}")
        dx, G["norm_ffn2"][i], G["w_ffn2_gu"][i], G["w_ffn2_down"][i] = _ffn_backward(
            dx, s["ffn2"], W["norm_ffn2"][i], W["w_ffn2_gu"][i], W["w_ffn2_down"][i], f"b{i}")
        hm = s["hm"]
        if i % 2 == 0:
            qkv, qs, kn, vb, o = s["mix"]
            G["w_o"][j] = _mm_wgrad(o, dx, 1.0, TM, f"attn_dwo{i}")
            D = dx.shape[1]
            do = _mm(dx, W["w_o"][j], [], [jax.ShapeDtypeStruct(dx.shape, BF)], dims=NT, grid=(dx.shape[0] // TM, 1),
                     a_spec=pl.BlockSpec((TM, D), lambda m, k: (m, 0)), b_spec=pl.BlockSpec((D, D), lambda m, k: (0, 0)),
                     extra_specs=[], out_specs=[pl.BlockSpec((TM, D), lambda m, k: (m, 0))], acc_shape=(TM, D),
                     epilogue=_store(BF), name=f"attn_do{i}")[0]
            dqs, dkn, dv = _attn_bwd(qs, kn, vb, o, do, f"attn_bwd{i}")
            dq, dk, dvb, dqg, dkg = _qk_norm_bwd(qkv, dqs, dkn, dv, W["q_norm"][j], W["k_norm"][j], TM, f"qk_norm_bwd{i}")
            dqkv = jnp.concatenate([dq, dk, dvb], axis=1)
            G["q_norm"][j] = dqg[:, :HEAD_DIM] + dqg[:, HEAD_DIM:]
            G["k_norm"][j] = dkg[:, :HEAD_DIM] + dkg[:, HEAD_DIM:]
            G["w_qkv"][j] = _mm_wgrad_cols(hm, dqkv, N_CHIPS, TM, f"attn_dwqkv{i}")
            dx, G["norm_mix"][i] = _mm_dx_norm_cols(dqkv, W["w_qkv"][j], s["x_mix"], W["norm_mix"][i], dx, TM, f"attn_dx{i}")
        else:
            (pooled,) = s["mix"]
            dp, G["w_pool_grp"][j], G["pool_scale"][j] = _pool_bwd_grp(dx, pooled, W["w_pool_grp"][j], W["pool_scale"][j], TM, f"pool_bwd_grp{i}")
            du = _pool_bwd_window(dp, TM, f"pool_bwd_win{i}")
            G["w_pool_in"][j] = _mm_wgrad(hm, du, 1.0, TM, f"pool_dwin{i}")
            dx, G["norm_mix"][i] = _mm_dx_norm(du, W["w_pool_in"][j], s["x_mix"], W["norm_mix"][i], dx, TM, f"pool_dx{i}")
        dx, G["norm_ffn1"][i], G["w_ffn1_gu"][i], G["w_ffn1_down"][i] = _ffn_backward(
            dx, s["ffn1"], W["norm_ffn1"][i], W["w_ffn1_gu"][i], W["w_ffn1_down"][i], f"a{i}")
    return sq, dx, G


SHARDED = ("w_ffn1_gu", "w_ffn1_down", "w_qkv", "w_o", "w_pool_in", "w_pool_grp", "w_ffn2_gu", "w_ffn2_down", "w_ple_gate", "w_ple_proj")
COLUMN_SHARDED = ("w_ffn1_gu", "w_qkv", "w_ffn2_gu", "w_ple_proj")
NORMS = ("norm_ffn1", "norm_mix", "norm_ffn2", "norm_ple")
HEAD_GAINS = ("q_norm", "k_norm")
WEIGHTS = ("norm_ffn1", "w_ffn1_gu", "w_ffn1_down", "norm_mix", "w_qkv", "q_norm", "k_norm", "w_o", "w_pool_in", "w_pool_grp",
           "pool_scale", "norm_ffn2", "w_ffn2_gu", "w_ffn2_down", "norm_ple", "w_ple_gate", "w_ple_proj")
SMALL_ROWS = 24


def _whole_weights(w, gathered, pool_scale_all):
    W = {}
    for k in NORMS:
        W[k] = [_row(w[k][i]) for i in range(w[k].shape[0])]
    for k in HEAD_GAINS:
        W[k] = [_row(jnp.tile(w[k][j], LANES // HEAD_DIM)) for j in range(w[k].shape[0])]
    for k in SHARDED:
        g = gathered[k]
        L = g.shape[1]
        if k == "w_pool_grp":
            W[k] = [jnp.transpose(g[:, i], (1, 0, 2, 3)).reshape(g.shape[2], -1, g.shape[4]) for i in range(L)]
        elif k == "w_ple_proj":
            W[k] = [jnp.transpose(g[:, i], (1, 0, 2)).reshape(g.shape[2], -1) for i in range(L)]
        elif k in COLUMN_SHARDED:
            W[k] = [g[:, i] for i in range(L)]
        else:
            W[k] = [g[:, i].reshape(-1, g.shape[3]) for i in range(L)]
    W["pool_scale"] = [_row(pool_scale_all[j]) for j in range(pool_scale_all.shape[0])]
    return W


def _shard_major(k, parts):
    out = []
    for g in parts:
        if k == "w_pool_grp":
            G_, C, _ = g.shape
            g = jnp.transpose(g.reshape(G_, N_CHIPS, C // N_CHIPS, C), (1, 0, 2, 3))
        elif k == "w_ple_proj":
            P, D = g.shape
            g = jnp.transpose(g.reshape(P, N_CHIPS, D // N_CHIPS), (1, 0, 2))
        elif k not in COLUMN_SHARDED:
            g = g.reshape(N_CHIPS, g.shape[0] // N_CHIPS, g.shape[1])
        out.append(g)
    return jnp.stack(out, axis=1)


def kernel(x, p, norm_ffn1, w_ffn1_gu, w_ffn1_down, norm_mix, w_qkv, q_norm, k_norm, w_o, w_pool_in, w_pool_grp, pool_scale, norm_ffn2, w_ffn2_gu, w_ffn2_down, norm_ple, w_ple_gate, w_ple_proj, loss_target, m_norm_ffn1, m_w_ffn1_gu, m_w_ffn1_down, m_norm_mix, m_w_qkv, m_q_norm, m_k_norm, m_w_o, m_w_pool_in, m_w_pool_grp, m_pool_scale, m_norm_ffn2, m_w_ffn2_gu, m_w_ffn2_down, m_norm_ple, m_w_ple_gate, m_w_ple_proj, v_norm_ffn1, v_w_ffn1_gu, v_w_ffn1_down, v_norm_mix, v_w_qkv, v_q_norm, v_k_norm, v_w_o, v_w_pool_in, v_w_pool_grp, v_pool_scale, v_norm_ffn2, v_w_ffn2_gu, v_w_ffn2_down, v_norm_ple, v_w_ple_gate, v_w_ple_proj):
    w = dict(norm_ffn1=norm_ffn1, w_ffn1_gu=w_ffn1_gu, w_ffn1_down=w_ffn1_down, norm_mix=norm_mix, w_qkv=w_qkv, q_norm=q_norm,
             k_norm=k_norm, w_o=w_o, w_pool_in=w_pool_in, w_pool_grp=w_pool_grp, pool_scale=pool_scale, norm_ffn2=norm_ffn2,
             w_ffn2_gu=w_ffn2_gu, w_ffn2_down=w_ffn2_down, norm_ple=norm_ple, w_ple_gate=w_ple_gate, w_ple_proj=w_ple_proj)
    m = dict(norm_ffn1=m_norm_ffn1, w_ffn1_gu=m_w_ffn1_gu, w_ffn1_down=m_w_ffn1_down, norm_mix=m_norm_mix, w_qkv=m_w_qkv,
             q_norm=m_q_norm, k_norm=m_k_norm, w_o=m_w_o, w_pool_in=m_w_pool_in, w_pool_grp=m_w_pool_grp, pool_scale=m_pool_scale,
             norm_ffn2=m_norm_ffn2, w_ffn2_gu=m_w_ffn2_gu, w_ffn2_down=m_w_ffn2_down, norm_ple=m_norm_ple, w_ple_gate=m_w_ple_gate,
             w_ple_proj=m_w_ple_proj)
    v = dict(norm_ffn1=v_norm_ffn1, w_ffn1_gu=v_w_ffn1_gu, w_ffn1_down=v_w_ffn1_down, norm_mix=v_norm_mix, w_qkv=v_w_qkv,
             q_norm=v_q_norm, k_norm=v_k_norm, w_o=v_w_o, w_pool_in=v_w_pool_in, w_pool_grp=v_w_pool_grp, pool_scale=v_pool_scale,
             norm_ffn2=v_norm_ffn2, w_ffn2_gu=v_w_ffn2_gu, w_ffn2_down=v_w_ffn2_down, norm_ple=v_norm_ple, w_ple_gate=v_w_ple_gate,
             w_ple_proj=v_w_ple_proj)
    chip = 2 * lax.axis_index("x") + lax.axis_index("y")
    D = x.shape[-1]
    shard_cols = pool_scale.shape[1]

    gathered = dict(zip(SHARDED, _gather_weights([w[k].astype(BF) for k in SHARDED], "gather_weights")))
    scale_rows = jnp.zeros((8, shard_cols), F32).at[:pool_scale.shape[0]].set(pool_scale)
    scale_all = _gather_small(scale_rows, "gather_pool_scale")
    pool_scale_all = jnp.transpose(scale_all[::2, :pool_scale.shape[0]], (1, 0, 2)).reshape(pool_scale.shape[0], D)
    W = _whole_weights(w, gathered, pool_scale_all)

    sq, dx, G = _local_step(x[0], p[:, 0], loss_target[0], W)
    loss = lax.psum(0.5 / D * sq[0, 0], ("x", "y", "c"))

    slots = _scatter_grads([_shard_major(k, G[k]) for k in SHARDED], "scatter_grads")
    halves = [_sum_slots(s, f"sum_{k}") for k, s in zip(SHARDED, slots)]
    grads = dict(zip(SHARDED, _join_halves(halves, "join_halves")))

    small = [G[k][i] for k in NORMS for i in range(len(G[k]))]
    small += [jnp.pad(jnp.concatenate(G[k], axis=1), ((0, 0), (0, D - len(G[k]) * HEAD_DIM))) for k in HEAD_GAINS]
    small += G["pool_scale"]
    small = jnp.concatenate(small + [jnp.zeros((SMALL_ROWS - len(small), D), F32)], axis=0)
    small = _sum_slots(_gather_small(small, "gather_small_grads"), "sum_small_grads")
    row = 0
    for k in NORMS:
        grads[k] = small[row:row + w[k].shape[0]]
        row += w[k].shape[0]
    for k in HEAD_GAINS:
        grads[k] = small[row, :w[k].size].reshape(w[k].shape)
        row += 1
    grads["pool_scale"] = lax.dynamic_slice_in_dim(small[row:row + pool_scale.shape[0]], chip * shard_cols, shard_cols, axis=1)

    delta, new_m, new_v = {}, {}, {}
    small_names = NORMS + HEAD_GAINS + ("pool_scale",)

    def pack(d):
        rows = [jnp.pad(d[k].reshape(-1, d[k].shape[-1]) if k in NORMS + ("pool_scale",) else d[k].reshape(1, -1),
                        ((0, 0), (0, D - (d[k].shape[-1] if k in NORMS + ("pool_scale",) else d[k].size))), constant_values=1.0)
                for k in small_names]
        n = sum(r.shape[0] for r in rows)
        return jnp.concatenate(rows + [jnp.ones((SMALL_ROWS - n, D), F32)], axis=0)

    packed = _adamw(pack(w), pack(grads), pack(m), pack(v), "adamw_small")
    row = 0
    for k in small_names:
        n = w[k].shape[0] if k in NORMS + ("pool_scale",) else 1
        width = w[k].shape[-1] if k in NORMS + ("pool_scale",) else w[k].size
        for dst, src in zip((delta, new_m, new_v), packed):
            dst[k] = src[row:row + n, :width].reshape(w[k].shape)
        row += n
    for k in SHARDED:
        delta[k], new_m[k], new_v[k] = _adamw(w[k], grads[k], m[k], v[k], f"adamw_{k}")

    return (loss, dx[None], *[grads[k] for k in WEIGHTS], *[delta[k] for k in WEIGHTS],
            *[new_m[k] for k in WEIGHTS], *[new_v[k] for k in WEIGHTS])
```

```python
import jax
import jax.numpy as jnp
from jax import lax
from jax.experimental import pallas as pl
from jax.experimental.pallas import tpu as pltpu

BF = jnp.bfloat16
F32 = jnp.float32

N_HEADS = 16
HEAD_DIM = 64
POOL_WINDOWS = (2, 4, 8, 16)
POOL_HALO = 16
EPS = 1e-6
ADAM_LR = 0.001
ADAM_B1 = 0.9
ADAM_B2 = 0.999
ADAM_EPS = 1e-08
ADAM_WD = 0.01
ADAM_STEP = 10

N_CHIPS = 4
N_DEV = 8
LANES = 128
VMEM_LIMIT = 56 * 1024 * 1024
ATT_TK = 256
ATT_TQ = 512
MESH = pl.DeviceIdType.MESH

NN = (((1,), (0,)), ((), ()))
NT = (((1,), (1,)), ((), ()))
TN = (((0,), (0,)), ((), ()))


def _params(n_axes):
    return pltpu.CompilerParams(dimension_semantics=("arbitrary",) * n_axes, vmem_limit_bytes=VMEM_LIMIT)


def _dot(a, b, dims):
    return lax.dot_general(a, b, dims, preferred_element_type=F32)


def _layer(li, block, index_map):
    return pl.BlockSpec((None,) + tuple(block), lambda *g: (li,) + tuple(index_map(*g)))


def _mm(a, b, extra, out_shapes, *, dims, grid, a_spec, b_spec, extra_specs, out_specs, acc_shape, epilogue, name, into=None):
    nk = grid[-1]
    aliases = {}
    if into is not None:
        aliases = {2 + len(extra): 0}
        extra = [*extra, into]
        extra_specs = [*extra_specs, pl.BlockSpec(memory_space=pl.ANY)]
        out_shapes = [jax.ShapeDtypeStruct(into.shape, into.dtype), *out_shapes[1:]]
    n_extra = len(extra)
    n_out = len(out_shapes)

    def body(a_ref, b_ref, *rest):
        ex = rest[:n_extra]
        outs = rest[n_extra:n_extra + n_out]
        acc = rest[-1]
        k = pl.program_id(len(grid) - 1)

        @pl.when(k == 0)
        def _():
            acc[...] = jnp.zeros_like(acc)

        acc[...] += _dot(a_ref[...].astype(BF), b_ref[...].astype(BF), dims)
        first = pl.program_id(0) == 0

        @pl.when(k == nk - 1)
        def _():
            epilogue(acc[...], ex, outs, first)

    return pl.pallas_call(
        body, grid=grid, in_specs=[a_spec, b_spec, *extra_specs], out_specs=out_specs, out_shape=out_shapes,
        scratch_shapes=[pltpu.VMEM(acc_shape, F32)], compiler_params=_params(len(grid)), name=name,
        input_output_aliases=aliases,
    )(a, b, *extra)


def _store(dtype, scale=1.0):
    def ep(acc, ex, outs, first):
        outs[0][...] = (acc * scale).astype(dtype)
    return ep


def _residual(scale):
    def ep(acc, ex, outs, first):
        outs[0][...] = ex[0][...] + scale * acc
    return ep


def _rms_bwd_epilogue(acc, ex, outs, first):
    x_ref, g_ref, dx_ref = ex
    dxo_ref, dg_ref = outs
    x = x_ref[...]
    r = lax.rsqrt(jnp.mean(x * x, axis=-1, keepdims=True) + EPS)
    xh = x * r
    dxh = acc * g_ref[...]
    dxo_ref[...] = dx_ref[...] + r * (dxh - xh * jnp.mean(dxh * xh, axis=-1, keepdims=True))

    @pl.when(first)
    def _():
        dg_ref[...] = jnp.zeros_like(dg_ref)

    dg_ref[...] += jnp.sum(acc * xh, axis=0, keepdims=True)


def _mm_fwd(a, w, li, x, scale, tm, name):
    S, K = a.shape
    N = w.shape[2]
    return _mm(a, w, [x], [jax.ShapeDtypeStruct((S, N), F32)], dims=NN, grid=(S // tm, 1),
               a_spec=pl.BlockSpec((tm, K), lambda m, k: (m, 0)), b_spec=_layer(li, (K, N), lambda m, k: (0, 0)),
               extra_specs=[pl.BlockSpec((tm, N), lambda m, k: (m, 0))], out_specs=[pl.BlockSpec((tm, N), lambda m, k: (m, 0))],
               acc_shape=(tm, N), epilogue=_residual(scale), name=name)[0]


def _mm_plain(a, w, li, dtype, tm, name, dims=NN):
    S, K = a.shape
    N = w.shape[2] if dims == NN else w.shape[1]
    return _mm(a, w, [], [jax.ShapeDtypeStruct((S, N), dtype)], dims=dims, grid=(S // tm, 1),
               a_spec=pl.BlockSpec((tm, K), lambda m, k: (m, 0)), b_spec=_layer(li, w.shape[1:], lambda m, k: (0, 0)),
               extra_specs=[], out_specs=[pl.BlockSpec((tm, N), lambda m, k: (m, 0))],
               acc_shape=(tm, N), epilogue=_store(dtype), name=name)[0]


def _mm_cols(a, w, li, dtype, tm, name):
    S, K = a.shape
    _, nj, _, ns = w.shape
    return _mm(a, w, [], [jax.ShapeDtypeStruct((S, nj * ns), dtype)], dims=NN, grid=(S // tm, nj, 1),
               a_spec=pl.BlockSpec((tm, K), lambda m, j, k: (m, 0)), b_spec=_layer(li, (None, K, ns), lambda m, j, k: (j, 0, 0)),
               extra_specs=[], out_specs=[pl.BlockSpec((tm, ns), lambda m, j, k: (m, j))],
               acc_shape=(tm, ns), epilogue=_store(dtype), name=name)[0]


def _mm_wgrad(a, b, scale, into, li, tk, name, tn=None):
    S, M = a.shape
    N = b.shape[1]
    tn = N if tn is None else tn
    return _mm(a, b, [], [None], dims=TN, grid=(N // tn, S // tk),
               a_spec=pl.BlockSpec((tk, M), lambda n, k: (k, 0)), b_spec=pl.BlockSpec((tk, tn), lambda n, k: (k, n)),
               extra_specs=[], out_specs=[_layer(li, (M, tn), lambda n, k: (0, n))],
               acc_shape=(M, tn), epilogue=_store(BF, scale), name=name, into=into)[0]


def _mm_wgrad_cols(a, b, into, li, tk, name):
    S, M = a.shape
    _, nj, _, ns = into.shape
    return _mm(a, b, [], [None], dims=TN, grid=(nj, S // tk),
               a_spec=pl.BlockSpec((tk, M), lambda j, k: (k, 0)), b_spec=pl.BlockSpec((tk, ns), lambda j, k: (k, j)),
               extra_specs=[], out_specs=[_layer(li, (None, M, ns), lambda j, k: (j, 0, 0))],
               acc_shape=(M, ns), epilogue=_store(BF), name=name, into=into)[0]


def _mm_dx_norm(a, w, li, x, gain, dx, tm, name):
    S, K = a.shape
    N = w.shape[1]
    row = pl.BlockSpec((tm, N), lambda m, k: (m, 0))
    vec = pl.BlockSpec((1, N), lambda m, k: (0, 0))
    return _mm(a, w, [x, gain, dx], [jax.ShapeDtypeStruct((S, N), F32), jax.ShapeDtypeStruct((1, N), F32)], dims=NT,
               grid=(S // tm, 1), a_spec=pl.BlockSpec((tm, K), lambda m, k: (m, 0)), b_spec=_layer(li, (N, K), lambda m, k: (0, 0)),
               extra_specs=[row, vec, row], out_specs=[row, vec], acc_shape=(tm, N), epilogue=_rms_bwd_epilogue, name=name)


def _mm_dx_norm_cols(a, w, li, x, gain, dx, tm, name):
    S = a.shape[0]
    _, nj, N, ks = w.shape
    row = pl.BlockSpec((tm, N), lambda m, j: (m, 0))
    vec = pl.BlockSpec((1, N), lambda m, j: (0, 0))
    return _mm(a, w, [x, gain, dx], [jax.ShapeDtypeStruct((S, N), F32), jax.ShapeDtypeStruct((1, N), F32)], dims=NT,
               grid=(S // tm, nj), a_spec=pl.BlockSpec((tm, ks), lambda m, j: (m, j)),
               b_spec=_layer(li, (None, N, ks), lambda m, j: (j, 0, 0)),
               extra_specs=[row, vec, row], out_specs=[row, vec], acc_shape=(tm, N), epilogue=_rms_bwd_epilogue, name=name)


def _rmsnorm(x, gain, tm, name):
    S, D = x.shape

    def body(x_ref, g_ref, h_ref):
        xv = x_ref[...]
        r = lax.rsqrt(jnp.mean(xv * xv, axis=-1, keepdims=True) + EPS)
        h_ref[...] = (xv * r * g_ref[...]).astype(BF)

    return pl.pallas_call(
        body, grid=(S // tm,), in_specs=[pl.BlockSpec((tm, D), lambda m: (m, 0)), pl.BlockSpec((1, D), lambda m: (0, 0))],
        out_specs=pl.BlockSpec((tm, D), lambda m: (m, 0)), out_shape=jax.ShapeDtypeStruct((S, D), BF),
        compiler_params=_params(1), name=name)(x, gain)


def _ffn_up(h, wgu, li, tm, name):
    S, D = h.shape
    ns = wgu.shape[3]
    half = wgu.shape[1] // 2

    def body(h_ref, wg_ref, wu_ref, g_ref, u_ref, act_ref):
        hv = h_ref[...]
        g = _dot(hv, wg_ref[...], NN)
        u = _dot(hv, wu_ref[...], NN)
        g_ref[...] = g.astype(BF)
        u_ref[...] = u.astype(BF)
        act_ref[...] = (g * jax.nn.sigmoid(g) * u).astype(BF)

    out = jax.ShapeDtypeStruct((S, half * ns), BF)
    tile = pl.BlockSpec((tm, ns), lambda j, m: (m, j))
    return pl.pallas_call(
        body, grid=(half, S // tm),
        in_specs=[pl.BlockSpec((tm, D), lambda j, m: (m, 0)), _layer(li, (None, D, ns), lambda j, m: (j, 0, 0)),
                  _layer(li, (None, D, ns), lambda j, m: (j + half, 0, 0))],
        out_specs=[tile, tile, tile], out_shape=[out, out, out], compiler_params=_params(2), name=name)(h, wgu, wgu)


def _ffn_dact(dx, wdown, li, g, u, tm, name):
    S, D = dx.shape
    F = wdown.shape[1]

    def ep(acc, ex, outs, first):
        gv = ex[0][...].astype(F32)
        uv = ex[1][...].astype(F32)
        da = 0.5 * acc
        sg = jax.nn.sigmoid(gv)
        outs[0][:, :F] = (da * uv * (sg * (1.0 + gv * (1.0 - sg)))).astype(BF)
        outs[0][:, F:] = (da * (gv * sg)).astype(BF)

    row = pl.BlockSpec((tm, F), lambda m, k: (m, 0))
    return _mm(dx, wdown, [g, u], [jax.ShapeDtypeStruct((S, 2 * F), BF)], dims=NT, grid=(S // tm, 1),
               a_spec=pl.BlockSpec((tm, D), lambda m, k: (m, 0)), b_spec=_layer(li, (F, D), lambda m, k: (0, 0)),
               extra_specs=[row, row], out_specs=[pl.BlockSpec((tm, 2 * F), lambda m, k: (m, 0))],
               acc_shape=(tm, F), epilogue=ep, name=name)[0]


def _ple_fwd(x, gain, wgate, li, p, wproj, tm, name):
    S, D = x.shape
    P = p.shape[1]

    def body(x_ref, g_ref, wg_ref, p_ref, wp_ref, xo_ref, hp_ref, gp_ref, pe_ref):
        xv = x_ref[...]
        r = lax.rsqrt(jnp.mean(xv * xv, axis=-1, keepdims=True) + EPS)
        hp = (xv * r * g_ref[...]).astype(BF)
        gp = _dot(hp, wg_ref[...], NN)
        pe = _dot(p_ref[...].astype(BF), wp_ref[...], NN)
        xo_ref[...] = xv + jax.nn.sigmoid(gp) * pe
        hp_ref[...] = hp
        gp_ref[...] = gp.astype(BF)
        pe_ref[...] = pe.astype(BF)

    row = pl.BlockSpec((tm, D), lambda m: (m, 0))
    return pl.pallas_call(
        body, grid=(S // tm,),
        in_specs=[row, pl.BlockSpec((1, D), lambda m: (0, 0)), _layer(li, (D, D), lambda m: (0, 0)),
                  pl.BlockSpec((tm, P), lambda m: (m, 0)), pl.BlockSpec((P, D), lambda m: (0, 0))],
        out_specs=[row, row, row, row],
        out_shape=[jax.ShapeDtypeStruct((S, D), F32)] + [jax.ShapeDtypeStruct((S, D), BF)] * 3,
        compiler_params=_params(1), name=name)(x, gain, wgate, p, wproj)


def _ple_bwd(dx, gp, pe, wgate, li, x, gain, tm, name):
    S, D = dx.shape

    def body(dx_ref, gp_ref, pe_ref, wg_ref, x_ref, g_ref, dxo_ref, dgp_ref, dpe_ref, dg_ref):
        dxv = dx_ref[...]
        sg = jax.nn.sigmoid(gp_ref[...].astype(F32))
        dgp = (dxv * pe_ref[...].astype(F32) * (sg * (1.0 - sg))).astype(BF)
        dgp_ref[...] = dgp
        dpe_ref[...] = (dxv * sg).astype(BF)
        dhp = _dot(dgp, wg_ref[...], NT)
        _rms_bwd_epilogue(dhp, (x_ref, g_ref, dx_ref), (dxo_ref, dg_ref), pl.program_id(0) == 0)

    row = pl.BlockSpec((tm, D), lambda m: (m, 0))
    vec = pl.BlockSpec((1, D), lambda m: (0, 0))
    return pl.pallas_call(
        body, grid=(S // tm,), in_specs=[row, row, row, _layer(li, (D, D), lambda m: (0, 0)), row, vec],
        out_specs=[row, row, row, vec],
        out_shape=[jax.ShapeDtypeStruct((S, D), F32), jax.ShapeDtypeStruct((S, D), BF), jax.ShapeDtypeStruct((S, D), BF),
                   jax.ShapeDtypeStruct((1, D), F32)],
        compiler_params=_params(1), name=name)(dx, gp, pe, wgate, x, gain)


def _loss_head(y, target, tm, name):
    S, D = y.shape

    def body(y_ref, t_ref, sq_ref, dy_ref):
        d = y_ref[...] - t_ref[...]
        dy_ref[...] = d * (1.0 / D)

        @pl.when(pl.program_id(0) == 0)
        def _():
            sq_ref[...] = jnp.zeros_like(sq_ref)

        sq_ref[...] += jnp.sum(d * d, keepdims=True)

    row = pl.BlockSpec((tm, D), lambda m: (m, 0))
    return pl.pallas_call(
        body, grid=(S // tm,), in_specs=[row, row], out_specs=[pl.BlockSpec((1, 1), lambda m: (0, 0)), row],
        out_shape=[jax.ShapeDtypeStruct((1, 1), F32), jax.ShapeDtypeStruct((S, D), F32)],
        compiler_params=_params(1), name=name)(y, target)


def _qk_norm(qkv, qgain, kgain, tm, name):
    S = qkv.shape[0]
    D = qkv.shape[1] // 3
    nb = D // LANES

    def norm2(t, gain):
        lo = lax.broadcasted_iota(jnp.int32, t.shape, 1) < HEAD_DIM
        sq = t * t
        s_lo = jnp.sum(jnp.where(lo, sq, 0.0), axis=1, keepdims=True)
        s_hi = jnp.sum(jnp.where(lo, 0.0, sq), axis=1, keepdims=True)
        r = lax.rsqrt(jnp.where(lo, s_lo, s_hi) * (1.0 / HEAD_DIM) + EPS)
        return t * r * gain

    def body(q_ref, k_ref, v_ref, qg_ref, kg_ref, qo_ref, ko_ref, vo_ref):
        qo_ref[...] = (norm2(q_ref[...], qg_ref[...]) * (HEAD_DIM ** -0.5)).astype(BF)
        ko_ref[...] = norm2(k_ref[...], kg_ref[...]).astype(BF)
        vo_ref[...] = v_ref[...].astype(BF)

    vec = pl.BlockSpec((1, LANES), lambda m, h: (0, 0))
    tile = pl.BlockSpec((tm, LANES), lambda m, h: (m, h))
    out = jax.ShapeDtypeStruct((S, D), BF)
    return pl.pallas_call(
        body, grid=(S // tm, nb),
        in_specs=[tile, pl.BlockSpec((tm, LANES), lambda m, h: (m, nb + h)), pl.BlockSpec((tm, LANES), lambda m, h: (m, 2 * nb + h)), vec, vec],
        out_specs=[tile, tile, tile], out_shape=[out, out, out], compiler_params=_params(2), name=name)(qkv, qkv, qkv, qgain, kgain)


def _qk_norm_bwd(qkv, dqs, dkn, dv, qgain, kgain, tm, name):
    S = qkv.shape[0]
    D = qkv.shape[1] // 3
    nb = D // LANES

    def norm2_bwd(t, gain, dn):
        lo = lax.broadcasted_iota(jnp.int32, t.shape, 1) < HEAD_DIM

        def headsum(val):
            s_lo = jnp.sum(jnp.where(lo, val, 0.0), axis=1, keepdims=True)
            s_hi = jnp.sum(jnp.where(lo, 0.0, val), axis=1, keepdims=True)
            return jnp.where(lo, s_lo, s_hi)

        r = lax.rsqrt(headsum(t * t) * (1.0 / HEAD_DIM) + EPS)
        th = t * r
        dth = dn * gain
        dt = r * (dth - th * (headsum(dth * th) * (1.0 / HEAD_DIM)))
        return dt, jnp.sum(dn * th, axis=0, keepdims=True)

    def body(q_ref, k_ref, dq_ref, dk_ref, dv_ref, qg_ref, kg_ref, dqo_ref, dko_ref, dvo_ref, dqg_ref, dkg_ref):
        dq, dqg = norm2_bwd(q_ref[...], qg_ref[...], dq_ref[...] * (HEAD_DIM ** -0.5))
        dk, dkg = norm2_bwd(k_ref[...], kg_ref[...], dk_ref[...])
        dqo_ref[...] = dq.astype(BF)
        dko_ref[...] = dk.astype(BF)
        dvo_ref[...] = dv_ref[...].astype(BF)

        @pl.when((pl.program_id(0) == 0) & (pl.program_id(1) == 0))
        def _():
            dqg_ref[...] = jnp.zeros_like(dqg_ref)
            dkg_ref[...] = jnp.zeros_like(dkg_ref)

        dqg_ref[...] += dqg
        dkg_ref[...] += dkg

    vec = pl.BlockSpec((1, LANES), lambda m, h: (0, 0))
    tile = pl.BlockSpec((tm, LANES), lambda m, h: (m, h))
    return pl.pallas_call(
        body, grid=(S // tm, nb),
        in_specs=[tile, pl.BlockSpec((tm, LANES), lambda m, h: (m, nb + h)), tile, tile, tile, vec, vec],
        out_specs=[tile, tile, tile, vec, vec],
        out_shape=[jax.ShapeDtypeStruct((S, D), BF)] * 3 + [jax.ShapeDtypeStruct((1, LANES), F32)] * 2,
        compiler_params=_params(2), name=name)(qkv, qkv, dqs, dkn, dv, qgain, kgain)


def _split_bf16(t):
    hi = t.astype(BF)
    return hi, (t - hi.astype(F32)).astype(BF)


HEADS_PER_BLOCK = LANES // HEAD_DIM


def _suffix_sums(t, from_here):
    hi, lo = _split_bf16(t)
    return _dot(hi, from_here, NN) + _dot(lo, from_here, NN)


def _attn_tiles(qs, ks, stays, valid, after):
    zs = [_dot(q, k, NT) for q, k in zip(qs, ks)]
    sps = [jnp.maximum(z, 0.0) + jnp.log(1.0 + jnp.exp(-jnp.abs(z))) for z in zs]
    if valid is not None:
        sps = [jnp.where(valid, sp, 0.0) for sp in sps]
    rights = [_dot(sp.astype(BF), after, NN) for sp in sps]
    logsigs = [z - sp for z, sp in zip(zs, sps)]
    ws = [jnp.exp(a - r + stay) for a, r, stay in zip(logsigs, rights, stays)]
    if valid is not None:
        ws = [jnp.where(valid, w, 0.0) for w in ws]
    return logsigs, ws, [r[:, :1] + sp[:, :1] for r, sp in zip(rights, sps)]


def _attn_walk(qi, tq, tk, block, carry):
    n_diag = tq // tk
    row = lax.broadcasted_iota(jnp.int32, (tq, tk), 0)
    col = lax.broadcasted_iota(jnp.int32, (tq, tk), 1)
    for d in reversed(range(n_diag)):
        carry = block(pl.multiple_of((n_diag * qi + d) * tk, tk), col < row - d * tk, carry)
    return lax.fori_loop(0, n_diag * qi, lambda i, c: block(pl.multiple_of((n_diag * qi - 1 - i) * tk, tk), None, c), carry)


def _key_order(tk):
    j = lax.broadcasted_iota(jnp.int32, (tk, tk), 0)
    s = lax.broadcasted_iota(jnp.int32, (tk, tk), 1)
    return (j > s).astype(BF), (j >= s).astype(BF)


def _attn_fwd(qs, kn, vb, name):
    S, D = qs.shape
    TQ, TK = ATT_TQ, ATT_TK
    heads = [slice(hh * HEAD_DIM, (hh + 1) * HEAD_DIM) for hh in range(HEADS_PER_BLOCK)]

    def body(q_ref, k_ref, v_ref, o_ref):
        after, _ = _key_order(TK)
        q = [q_ref[:, lanes] for lanes in heads]

        def block(start, valid, carry):
            rows = pl.ds(start, TK)
            _, ws, totals = _attn_tiles(q, [k_ref[rows, lanes] for lanes in heads], [c[1] for c in carry], valid, after)
            outs = [_dot(w.astype(BF), v_ref[rows, lanes], NN) for w, lanes in zip(ws, heads)]
            return tuple((c[0] + o, c[1] - t) for c, o, t in zip(carry, outs, totals))

        carry = tuple((jnp.zeros((TQ, HEAD_DIM), F32), jnp.zeros((TQ, 1), F32)) for _ in heads)
        carry = _attn_walk(pl.program_id(1), TQ, TK, block, carry)
        for hh, lanes in enumerate(heads):
            o_ref[:, lanes] = carry[hh][0]

    tile = pl.BlockSpec((TQ, LANES), lambda h, m: (m, h))
    full = pl.BlockSpec((S, LANES), lambda h, m: (0, h))
    return pl.pallas_call(
        body, grid=(D // LANES, S // TQ), in_specs=[tile, full, full], out_specs=tile,
        out_shape=jax.ShapeDtypeStruct((S, D), F32), compiler_params=_params(2), name=name)(qs, kn, vb)


def _attn_bwd(qs, kn, vb, o, do, name):
    S, D = qs.shape
    TQ, TK = ATT_TQ, ATT_TK
    heads = [slice(hh * HEAD_DIM, (hh + 1) * HEAD_DIM) for hh in range(HEADS_PER_BLOCK)]

    def body(q_ref, k_ref, v_ref, o_ref, do_ref, dq_ref, dk_ref, dv_ref):
        @pl.when(pl.program_id(1) == 0)
        def _():
            dk_ref[...] = jnp.zeros_like(dk_ref)
            dv_ref[...] = jnp.zeros_like(dv_ref)

        after, from_here = _key_order(TK)
        q = [q_ref[:, lanes] for lanes in heads]
        dout = [do_ref[:, lanes] for lanes in heads]

        def block(start, valid, carry):
            rows = pl.ds(start, TK)
            ks = [k_ref[rows, lanes] for lanes in heads]
            logsigs, ws, totals = _attn_tiles(q, ks, [c[1] for c in carry], valid, after)
            wbs = [w.astype(BF) for w in ws]
            das = [_dot(d, v_ref[rows, lanes], NT) for d, lanes in zip(dout, heads)]
            gs = [wb.astype(F32) * da for wb, da in zip(wbs, das)]
            g_sums = [_suffix_sums(g, from_here) for g in gs]
            dzs = [g - jnp.exp(a) * (g + (c[2] - gsum)) for g, a, c, gsum in zip(gs, logsigs, carry, g_sums)]
            if valid is not None:
                dzs = [jnp.where(valid, dz, 0.0) for dz in dzs]
            dzbs = [dz.astype(BF) for dz in dzs]
            for hh, lanes in enumerate(heads):
                dk_ref[rows, lanes] += _dot(dzbs[hh], q[hh], TN)
                dv_ref[rows, lanes] += _dot(wbs[hh], dout[hh], TN)
            return tuple((c[0] + _dot(dzb, k, NN), c[1] - t, c[2] - gsum[:, :1])
                         for c, dzb, k, t, gsum in zip(carry, dzbs, ks, totals, g_sums))

        carry = tuple((jnp.zeros((TQ, HEAD_DIM), F32), jnp.zeros((TQ, 1), F32),
                       jnp.sum(dout[hh].astype(F32) * o_ref[:, lanes], axis=1, keepdims=True)) for hh, lanes in enumerate(heads))
        carry = _attn_walk(pl.program_id(1), TQ, TK, block, carry)
        for hh, lanes in enumerate(heads):
            dq_ref[:, lanes] = carry[hh][0]

    tile = pl.BlockSpec((TQ, LANES), lambda h, m: (m, h))
    full = pl.BlockSpec((S, LANES), lambda h, m: (0, h))
    out = jax.ShapeDtypeStruct((S, D), F32)
    return pl.pallas_call(
        body, grid=(D // LANES, S // TQ), in_specs=[tile, full, full, tile, tile], out_specs=[tile, full, full],
        out_shape=[out, out, out], compiler_params=_params(2), name=name)(qs, kn, vb, o, do)


def _pool_counts(T, first_row):
    pos = first_row + lax.broadcasted_iota(jnp.int32, (T, 1), 0)
    return [jnp.minimum(pos + 1, w).astype(F32) for w in POOL_WINDOWS]


def _pool_fwd(u, wgrp, scale, x, tm, name):
    S, D = u.shape
    G = len(POOL_WINDOWS)
    C = D // G
    H = POOL_HALO

    def body(u_ref, prev_ref, w_ref, s_ref, x_ref, xo_ref, pooled_ref):
        m = pl.program_id(0)
        prev = jnp.where(m == 0, 0.0, prev_ref[...])
        ext = jnp.concatenate([prev, u_ref[...]], axis=0)
        counts = _pool_counts(tm, m * tm)
        ys = []
        acc = ext
        shift = 1
        for gi, w in enumerate(POOL_WINDOWS):
            while shift < w:
                acc = acc + pltpu.roll(acc, shift, axis=0)
                shift *= 2
            cols = slice(gi * C, (gi + 1) * C)
            pooled = (acc[H:, cols] / counts[gi] - ext[H:, cols]).astype(BF)
            pooled_ref[:, cols] = pooled
            ys.append(_dot(pooled, w_ref[gi], NN))
        xo_ref[...] = x_ref[...] + jnp.concatenate(ys, axis=1) * s_ref[...]

    row = pl.BlockSpec((tm, D), lambda m: (m, 0))
    return pl.pallas_call(
        body, grid=(S // tm,),
        in_specs=[row, pl.BlockSpec((H, D), lambda m: (jnp.maximum(m * (tm // H) - 1, 0), 0)),
                  pl.BlockSpec((G, C, C), lambda m: (0, 0, 0)), pl.BlockSpec((1, D), lambda m: (0, 0)), row],
        out_specs=[row, row], out_shape=[jax.ShapeDtypeStruct((S, D), F32), jax.ShapeDtypeStruct((S, D), BF)],
        compiler_params=_params(1), name=name)(u, u, wgrp, scale, x)


def _pool_bwd_grp(dx, pooled, wgrp, scale, tm, name):
    S, D = dx.shape
    G = len(POOL_WINDOWS)
    C = D // G

    def body(dx_ref, pooled_ref, w_ref, s_ref, dp_ref, dw_ref, ds_ref, dw_acc):
        m = pl.program_id(0)

        @pl.when(m == 0)
        def _():
            dw_acc[...] = jnp.zeros_like(dw_acc)
            ds_ref[...] = jnp.zeros_like(ds_ref)

        dxv = dx_ref[...]
        dy = (dxv * s_ref[...]).astype(BF)
        ys = []
        for gi in range(G):
            cols = slice(gi * C, (gi + 1) * C)
            pg = pooled_ref[:, cols]
            ys.append(_dot(pg, w_ref[gi], NN))
            dw_acc[gi] += _dot(pg, dy[:, cols], TN)
            dp_ref[:, cols] = _dot(dy[:, cols], w_ref[gi], NT)
        ds_ref[...] += jnp.sum(dxv * jnp.concatenate(ys, axis=1), axis=0, keepdims=True)

        @pl.when(m == S // tm - 1)
        def _():
            dw_ref[...] = dw_acc[...].astype(BF)

    row = pl.BlockSpec((tm, D), lambda m: (m, 0))
    wspec = pl.BlockSpec((G, C, C), lambda m: (0, 0, 0))
    vec = pl.BlockSpec((1, D), lambda m: (0, 0))
    return pl.pallas_call(
        body, grid=(S // tm,), in_specs=[row, row, wspec, vec], out_specs=[row, wspec, vec],
        out_shape=[jax.ShapeDtypeStruct((S, D), F32), jax.ShapeDtypeStruct((G, C, C), BF), jax.ShapeDtypeStruct((1, D), F32)],
        scratch_shapes=[pltpu.VMEM((G, C, C), F32)], compiler_params=_params(1), name=name)(dx, pooled, wgrp, scale)


def _pool_bwd_window(dp, tm, name):
    S, D = dp.shape
    G = len(POOL_WINDOWS)
    C = D // G
    H = POOL_HALO
    last = S // tm - 1

    def body(dp_ref, next_ref, du_ref):
        m = pl.program_id(0)
        counts = _pool_counts(tm + H, m * tm)
        nxt = jnp.where(m == last, 0.0, next_ref[...])
        ext = jnp.concatenate([dp_ref[...], nxt], axis=0)
        for gi, w in enumerate(POOL_WINDOWS):
            cols = slice(gi * C, (gi + 1) * C)
            acc = ext[:, cols] / counts[gi]
            shift = 1
            while shift < w:
                acc = acc + pltpu.roll(acc, tm + H - shift, axis=0)
                shift *= 2
            du_ref[:, cols] = (acc[:tm] - ext[:tm, cols]).astype(BF)

    row = pl.BlockSpec((tm, D), lambda m: (m, 0))
    return pl.pallas_call(
        body, grid=(S // tm,),
        in_specs=[row, pl.BlockSpec((H, D), lambda m: (jnp.minimum((m + 1) * (tm // H), S // H - 1), 0))],
        out_specs=row, out_shape=jax.ShapeDtypeStruct((S, D), BF), compiler_params=_params(1), name=name)(dp, dp)


ELEMENTWISE_TILE_BYTES = 1 << 20


def _row_tile(rows, row_bytes):
    for cand in (512, 256, 128, 64, 32, 16, 8):
        if rows % cand == 0 and cand * row_bytes <= ELEMENTWISE_TILE_BYTES:
            return cand
    return rows


def _adamw(w, g, m, v, name):
    shape = w.shape
    cols = shape[-1]
    rows = w.size // cols
    tr = _row_tile(rows, cols * 4)

    def body(w_ref, g_ref, m_ref, v_ref, d_ref, mo_ref, vo_ref):
        gv = g_ref[...]
        mn = ADAM_B1 * m_ref[...] + (1.0 - ADAM_B1) * gv
        vn = ADAM_B2 * v_ref[...] + (1.0 - ADAM_B2) * jnp.square(gv)
        m_hat = mn / (1.0 - ADAM_B1 ** ADAM_STEP)
        v_hat = vn / (1.0 - ADAM_B2 ** ADAM_STEP)
        d_ref[...] = -ADAM_LR * (m_hat / (jnp.sqrt(v_hat) + ADAM_EPS) + ADAM_WD * w_ref[...])
        mo_ref[...] = mn
        vo_ref[...] = vn

    tile = pl.BlockSpec((tr, cols), lambda i: (i, 0))
    out = jax.ShapeDtypeStruct((rows, cols), F32)
    res = pl.pallas_call(
        body, grid=(rows // tr,), in_specs=[tile] * 4, out_specs=[tile] * 3, out_shape=[out] * 3,
        compiler_params=_params(1), name=name)(*[t.reshape(rows, cols) for t in (w, g, m, v)])
    return [t.reshape(shape) for t in res]


def _sum_slots(r, name):
    n = r.shape[0]
    shape = r.shape[1:]
    cols = shape[-1]
    rows = r.size // (n * cols)
    tr = _row_tile(rows, n * cols * r.dtype.itemsize)

    def body(r_ref, o_ref):
        acc = r_ref[0].astype(F32)
        for d in range(1, n):
            acc = acc + r_ref[d].astype(F32)
        o_ref[...] = acc

    return pl.pallas_call(
        body, grid=(rows // tr,), in_specs=[pl.BlockSpec((n, tr, cols), lambda i: (0, i, 0))],
        out_specs=pl.BlockSpec((tr, cols), lambda i: (i, 0)), out_shape=jax.ShapeDtypeStruct((rows, cols), F32),
        compiler_params=_params(1), name=name)(r.reshape(n, rows, cols)).reshape(shape)


def _add_pair(a, b, name):
    shape = a.shape
    cols = shape[-1]
    rows = a.size // cols
    tr = _row_tile(rows, cols * 4)

    def body(a_ref, b_ref, o_ref):
        o_ref[...] = (a_ref[...].astype(F32) + b_ref[...].astype(F32)).astype(BF)

    tile = pl.BlockSpec((tr, cols), lambda i: (i, 0))
    return pl.pallas_call(
        body, grid=(rows // tr,), in_specs=[tile, tile], out_specs=tile, out_shape=jax.ShapeDtypeStruct((rows, cols), BF),
        compiler_params=_params(1), name=name)(a.reshape(rows, cols), b.reshape(rows, cols)).reshape(shape)


ANY = pl.BlockSpec(memory_space=pl.ANY)


def _position():
    return lax.axis_index("x"), lax.axis_index("y"), lax.axis_index("c")


def _gather_small(t, name):
    rows, cols = t.shape

    def body(t_ref, o_ref, send_sems, recv_sems):
        x, y, c = _position()
        me = 4 * x + 2 * y + c
        o_ref[me] = t_ref[...]
        copies = []
        for k in range(1, N_DEV):
            peer = (x ^ (k >> 2), y ^ ((k >> 1) & 1), c ^ (k & 1))
            cp = pltpu.make_async_remote_copy(src_ref=t_ref, dst_ref=o_ref.at[me], send_sem=send_sems.at[k - 1],
                                              recv_sem=recv_sems.at[k - 1], device_id=peer, device_id_type=MESH)
            cp.start()
            copies.append((cp, 4 * peer[0] + 2 * peer[1] + peer[2]))
        for k, (cp, src) in enumerate(copies):
            pltpu.make_async_remote_copy(src_ref=t_ref, dst_ref=o_ref.at[src], send_sem=send_sems.at[k], recv_sem=recv_sems.at[k],
                                         device_id=(x, y, c), device_id_type=MESH).wait_recv()
        for cp, _ in copies:
            cp.wait_send()

    return pl.pallas_call(
        body, in_specs=[pl.BlockSpec(memory_space=pltpu.VMEM)], out_specs=pl.BlockSpec(memory_space=pltpu.VMEM),
        out_shape=jax.ShapeDtypeStruct((N_DEV, rows, cols), F32),
        scratch_shapes=[pltpu.SemaphoreType.DMA((N_DEV - 1,)), pltpu.SemaphoreType.DMA((N_DEV - 1,))], name=name)(t)


def _other_chips(x, y):
    return [(1 - x, y), (x, 1 - y), (1 - x, 1 - y)]


def _remote(src, dst, send_sems, recv_sems, k, to):
    return pltpu.make_async_remote_copy(src_ref=src, dst_ref=dst, send_sem=send_sems.at[k], recv_sem=recv_sems.at[k],
                                        device_id=to, device_id_type=MESH)


def _gather_weights(shards, name):
    n = len(shards)

    def body(*refs):
        ins, outs = refs[:n], refs[n:2 * n]
        ici_send, ici_recv, d2d_send, d2d_recv = refs[2 * n:]
        x, y, c = _position()
        me, sibling, chip = (x, y, c), (x, y, 1 - c), 2 * x + y
        chips = _other_chips(x, y)
        halves = [s.shape[0] // 2 for s in shards]
        sends = []
        for p in range(n):
            mine = pl.ds(c * halves[p], halves[p])
            for j, (px, py) in enumerate(chips):
                sends.append(_remote(ins[p].at[mine], outs[p].at[mine, chip], ici_send, ici_recv, (p, j), (px, py, c)))
                sends[-1].start()
        for p in range(n):
            mine = pl.ds(c * halves[p], halves[p])
            for j, (px, py) in enumerate(chips):
                landed = outs[p].at[mine, 2 * px + py]
                _remote(ins[p].at[mine], landed, ici_send, ici_recv, (p, j), me).wait_recv()
                sends.append(_remote(landed, landed, d2d_send, d2d_recv, (p, j), sibling))
                sends[-1].start()
        for p in range(n):
            theirs = pl.ds((1 - c) * halves[p], halves[p])
            for j, (px, py) in enumerate(chips):
                passed = outs[p].at[theirs, 2 * px + py]
                _remote(passed, passed, d2d_send, d2d_recv, (p, j), me).wait_recv()
        for cp in sends:
            cp.wait_send()

    return pl.pallas_call(
        body, in_specs=[ANY] * n, out_specs=[ANY] * n,
        out_shape=[jax.ShapeDtypeStruct((s.shape[0], N_CHIPS) + s.shape[1:], s.dtype) for s in shards],
        scratch_shapes=[pltpu.SemaphoreType.DMA((n, 3))] * 4, name=name)(*shards)


def _swap_partials(grads, name):
    n = len(grads)

    def body(*refs):
        ins, outs = refs[:n], refs[n:2 * n]
        send_sems, recv_sems = refs[2 * n:]
        x, y, c = _position()
        copies = []
        for p in range(n):
            half = grads[p].shape[0] // 2
            copies.append(_remote(ins[p].at[pl.ds((1 - c) * half, half)], outs[p], send_sems, recv_sems, p, (x, y, 1 - c)))
            copies[-1].start()
        for cp in copies:
            cp.wait()

    return pl.pallas_call(
        body, in_specs=[ANY] * n, out_specs=[ANY] * n,
        out_shape=[jax.ShapeDtypeStruct((g.shape[0] // 2,) + g.shape[1:], g.dtype) for g in grads],
        scratch_shapes=[pltpu.SemaphoreType.DMA((n,))] * 2, name=name)(*grads)


def _scatter_sums(sums, name):
    n = len(sums)

    def body(*refs):
        ins, outs = refs[:n], refs[n:2 * n]
        send_sems, recv_sems = refs[2 * n:]
        x, y, c = _position()
        chip = 2 * x + y
        chips = _other_chips(x, y)
        sends = []
        for p in range(n):
            layers = pl.ds(0, sums[p].shape[0])
            for j, (px, py) in enumerate(chips):
                sends.append(_remote(ins[p].at[layers, 2 * px + py], outs[p].at[chip], send_sems, recv_sems, (p, j), (px, py, c)))
                sends[-1].start()
        for p in range(n):
            layers = pl.ds(0, sums[p].shape[0])
            for j, (px, py) in enumerate(chips):
                _remote(ins[p].at[layers, chip], outs[p].at[2 * px + py], send_sems, recv_sems, (p, j), (x, y, c)).wait_recv()
        for cp in sends:
            cp.wait_send()

    return pl.pallas_call(
        body, in_specs=[ANY] * n, out_specs=[ANY] * n,
        out_shape=[jax.ShapeDtypeStruct((N_CHIPS, s.shape[0]) + s.shape[2:], s.dtype) for s in sums],
        scratch_shapes=[pltpu.SemaphoreType.DMA((n, 3))] * 2, name=name)(*sums)


def _join_halves(halves, name):
    n = len(halves)
    flat = [h.reshape(-1, h.shape[-1]) for h in halves]

    def body(*refs):
        ins, outs = refs[:n], refs[n:2 * n]
        send_sems, recv_sems = refs[2 * n:]
        x, y, c = _position()
        copies = []
        for p in range(n):
            cp = pltpu.make_async_remote_copy(src_ref=ins[p], dst_ref=outs[p], send_sem=send_sems.at[p], recv_sem=recv_sems.at[p],
                                              device_id=(x, y, 1 - c), device_id_type=MESH)
            cp.start()
            copies.append(cp)
        for cp in copies:
            cp.wait()

    theirs = pl.pallas_call(
        body, in_specs=[ANY] * n, out_specs=[ANY] * n, out_shape=[jax.ShapeDtypeStruct(f.shape, f.dtype) for f in flat],
        scratch_shapes=[pltpu.SemaphoreType.DMA((n,)), pltpu.SemaphoreType.DMA((n,))], name=name)(*flat)
    south = lax.axis_index("c") == 0
    return [jnp.concatenate([jnp.where(south, h, t.reshape(h.shape)), jnp.where(south, t.reshape(h.shape), h)], axis=0)
            for h, t in zip(halves, theirs)]


TM = 512


def _row(v):
    return v.reshape(1, -1)


def _ffn_forward(x, gain, wgu, wdown, li, tag):
    h = _rmsnorm(x, gain, TM, f"norm_{tag}")
    g, u, act = _ffn_up(h, wgu, li, TM, f"ffn_up_{tag}")
    xo = _mm_fwd(act, wdown, li, x, 0.5, TM, f"ffn_down_{tag}")
    return xo, (x, h, g, u, act)


def _ffn_backward(dx, saved, gain, wgu, wdown, d_wgu, d_wdown, li, tag):
    x, h, g, u, act = saved
    dgu = _ffn_dact(dx, wdown, li, g, u, 256, f"ffn_dact_{tag}")
    d_wdown = _mm_wgrad(act, dx, 0.5, d_wdown, li, TM, f"ffn_dwdown_{tag}", tn=512)
    d_wgu = _mm_wgrad_cols(h, dgu, d_wgu, li, TM, f"ffn_dwgu_{tag}")
    dx, d_gain = _mm_dx_norm_cols(dgu, wgu, li, x, gain, dx, TM, f"ffn_dx_{tag}")
    return dx, d_gain, d_wgu, d_wdown


STACKED = ("w_ffn1_gu", "w_ffn1_down", "w_qkv", "w_o", "w_pool_in", "w_ffn2_gu", "w_ffn2_down", "w_ple_gate")


def _local_step(x, p, target, W):
    depth = p.shape[0]
    saved = []
    for i in range(depth):
        j = i // 2
        s = {}
        x, s["ffn1"] = _ffn_forward(x, W["norm_ffn1"][i], W["w_ffn1_gu"], W["w_ffn1_down"], i, f"a{i}")
        s["x_mix"] = x
        hm = _rmsnorm(x, W["norm_mix"][i], TM, f"norm_mix{i}")
        s["hm"] = hm
        if i % 2 == 0:
            qkv = _mm_cols(hm, W["w_qkv"], j, F32, TM, f"qkv{i}")
            qs, kn, vb = _qk_norm(qkv, W["q_norm"][j], W["k_norm"][j], TM, f"qk_norm{i}")
            o = _attn_fwd(qs, kn, vb, f"attn_fwd{i}")
            x = _mm_fwd(o, W["w_o"], j, x, 1.0, TM, f"attn_out{i}")
            s["mix"] = (qkv, qs, kn, vb, o)
        else:
            u = _mm_plain(hm, W["w_pool_in"], j, F32, TM, f"pool_in{i}")
            x, pooled = _pool_fwd(u, W["w_pool_grp"][j], W["pool_scale"][j], x, TM, f"pool_fwd{i}")
            s["mix"] = (pooled,)
        x, s["ffn2"] = _ffn_forward(x, W["norm_ffn2"][i], W["w_ffn2_gu"], W["w_ffn2_down"], i, f"b{i}")
        s["x_ple"] = x
        x, hp, gp, pe = _ple_fwd(x, W["norm_ple"][i], W["w_ple_gate"], i, p[i], W["w_ple_proj"][i], TM, f"ple_fwd{i}")
        s["ple"] = (hp, gp, pe)
        saved.append(s)

    sq, dx = _loss_head(x, target, TM, "loss_head")
    G = {k: (lax.empty(v.shape, BF) if k in STACKED else [None] * len(v)) for k, v in W.items()}
    for i in reversed(range(depth)):
        j = i // 2
        s = saved[i]
        hp, gp, pe = s["ple"]
        dx, dgp, dpe, G["norm_ple"][i] = _ple_bwd(dx, gp, pe, W["w_ple_gate"], i, s["x_ple"], W["norm_ple"][i], TM, f"ple_bwd{i}")
        G["w_ple_gate"] = _mm_wgrad(hp, dgp, 1.0, G["w_ple_gate"], i, TM, f"ple_dwgate{i}")
        G["w_ple_proj"][i] = _mm_wgrad(p[i], dpe, 1.0, lax.empty((1,) + W["w_ple_proj"][i].shape, BF), 0, TM, f"ple_dwproj{i}")[0]
        dx, G["norm_ffn2"][i], G["w_ffn2_gu"], G["w_ffn2_down"] = _ffn_backward(
            dx, s["ffn2"], W["norm_ffn2"][i], W["w_ffn2_gu"], W["w_ffn2_down"], G["w_ffn2_gu"], G["w_ffn2_down"], i, f"b{i}")
        hm = s["hm"]
        if i % 2 == 0:
            qkv, qs, kn, vb, o = s["mix"]
            G["w_o"] = _mm_wgrad(o, dx, 1.0, G["w_o"], j, TM, f"attn_dwo{i}")
            do = _mm_plain(dx, W["w_o"], j, BF, TM, f"attn_do{i}", dims=NT)
            dqs, dkn, dv = _attn_bwd(qs, kn, vb, o, do, f"attn_bwd{i}")
            dq, dk, dvb, dqg, dkg = _qk_norm_bwd(qkv, dqs, dkn, dv, W["q_norm"][j], W["k_norm"][j], TM, f"qk_norm_bwd{i}")
            dqkv = jnp.concatenate([dq, dk, dvb], axis=1)
            G["q_norm"][j] = dqg[:, :HEAD_DIM] + dqg[:, HEAD_DIM:]
            G["k_norm"][j] = dkg[:, :HEAD_DIM] + dkg[:, HEAD_DIM:]
            G["w_qkv"] = _mm_wgrad_cols(hm, dqkv, G["w_qkv"], j, TM, f"attn_dwqkv{i}")
            dx, G["norm_mix"][i] = _mm_dx_norm_cols(dqkv, W["w_qkv"], j, s["x_mix"], W["norm_mix"][i], dx, TM, f"attn_dx{i}")
        else:
            (pooled,) = s["mix"]
            dp, G["w_pool_grp"][j], G["pool_scale"][j] = _pool_bwd_grp(dx, pooled, W["w_pool_grp"][j], W["pool_scale"][j], TM, f"pool_bwd_grp{i}")
            du = _pool_bwd_window(dp, TM, f"pool_bwd_win{i}")
            G["w_pool_in"] = _mm_wgrad(hm, du, 1.0, G["w_pool_in"], j, TM, f"pool_dwin{i}")
            dx, G["norm_mix"][i] = _mm_dx_norm(du, W["w_pool_in"], j, s["x_mix"], W["norm_mix"][i], dx, TM, f"pool_dx{i}")
        dx, G["norm_ffn1"][i], G["w_ffn1_gu"], G["w_ffn1_down"] = _ffn_backward(
            dx, s["ffn1"], W["norm_ffn1"][i], W["w_ffn1_gu"], W["w_ffn1_down"], G["w_ffn1_gu"], G["w_ffn1_down"], i, f"a{i}")
    return sq, dx, G


SHARDED = ("w_ffn1_gu", "w_ffn1_down", "w_qkv", "w_o", "w_pool_in", "w_pool_grp", "w_ffn2_gu", "w_ffn2_down", "w_ple_gate", "w_ple_proj")
COLUMN_SHARDED = ("w_ffn1_gu", "w_qkv", "w_ffn2_gu", "w_ple_proj")
NORMS = ("norm_ffn1", "norm_mix", "norm_ffn2", "norm_ple")
HEAD_GAINS = ("q_norm", "k_norm")
WEIGHTS = ("norm_ffn1", "w_ffn1_gu", "w_ffn1_down", "norm_mix", "w_qkv", "q_norm", "k_norm", "w_o", "w_pool_in", "w_pool_grp",
           "pool_scale", "norm_ffn2", "w_ffn2_gu", "w_ffn2_down", "norm_ple", "w_ple_gate", "w_ple_proj")
SMALL_ROWS = 24


def _whole_weights(w, gathered, pool_scale_all):
    W = {}
    for k in NORMS:
        W[k] = [_row(w[k][i]) for i in range(w[k].shape[0])]
    for k in HEAD_GAINS:
        W[k] = [_row(jnp.tile(w[k][j], LANES // HEAD_DIM)) for j in range(w[k].shape[0])]
    for k in SHARDED:
        g = gathered[k]
        L = g.shape[0]
        if k == "w_pool_grp":
            W[k] = [jnp.transpose(g[i], (1, 0, 2, 3)).reshape(g.shape[2], -1, g.shape[4]) for i in range(L)]
        elif k == "w_ple_proj":
            W[k] = [jnp.transpose(g[i], (1, 0, 2)).reshape(g.shape[2], -1) for i in range(L)]
        elif k in COLUMN_SHARDED:
            W[k] = g
        else:
            W[k] = g.reshape(L, -1, g.shape[3])
    W["pool_scale"] = [_row(pool_scale_all[j]) for j in range(pool_scale_all.shape[0])]
    return W


def _shard_major(k, g):
    if k == "w_pool_grp":
        return jnp.stack([jnp.transpose(t.reshape(t.shape[0], N_CHIPS, t.shape[1] // N_CHIPS, t.shape[2]), (1, 0, 2, 3)) for t in g])
    if k == "w_ple_proj":
        return jnp.stack([jnp.transpose(t.reshape(t.shape[0], N_CHIPS, t.shape[1] // N_CHIPS), (1, 0, 2)) for t in g])
    if k in COLUMN_SHARDED:
        return g
    return g.reshape(g.shape[0], N_CHIPS, g.shape[1] // N_CHIPS, g.shape[2])


def kernel(x, p, norm_ffn1, w_ffn1_gu, w_ffn1_down, norm_mix, w_qkv, q_norm, k_norm, w_o, w_pool_in, w_pool_grp, pool_scale, norm_ffn2, w_ffn2_gu, w_ffn2_down, norm_ple, w_ple_gate, w_ple_proj, loss_target, m_norm_ffn1, m_w_ffn1_gu, m_w_ffn1_down, m_norm_mix, m_w_qkv, m_q_norm, m_k_norm, m_w_o, m_w_pool_in, m_w_pool_grp, m_pool_scale, m_norm_ffn2, m_w_ffn2_gu, m_w_ffn2_down, m_norm_ple, m_w_ple_gate, m_w_ple_proj, v_norm_ffn1, v_w_ffn1_gu, v_w_ffn1_down, v_norm_mix, v_w_qkv, v_q_norm, v_k_norm, v_w_o, v_w_pool_in, v_w_pool_grp, v_pool_scale, v_norm_ffn2, v_w_ffn2_gu, v_w_ffn2_down, v_norm_ple, v_w_ple_gate, v_w_ple_proj):
    w = dict(norm_ffn1=norm_ffn1, w_ffn1_gu=w_ffn1_gu, w_ffn1_down=w_ffn1_down, norm_mix=norm_mix, w_qkv=w_qkv, q_norm=q_norm,
             k_norm=k_norm, w_o=w_o, w_pool_in=w_pool_in, w_pool_grp=w_pool_grp, pool_scale=pool_scale, norm_ffn2=norm_ffn2,
             w_ffn2_gu=w_ffn2_gu, w_ffn2_down=w_ffn2_down, norm_ple=norm_ple, w_ple_gate=w_ple_gate, w_ple_proj=w_ple_proj)
    m = dict(norm_ffn1=m_norm_ffn1, w_ffn1_gu=m_w_ffn1_gu, w_ffn1_down=m_w_ffn1_down, norm_mix=m_norm_mix, w_qkv=m_w_qkv,
             q_norm=m_q_norm, k_norm=m_k_norm, w_o=m_w_o, w_pool_in=m_w_pool_in, w_pool_grp=m_w_pool_grp, pool_scale=m_pool_scale,
             norm_ffn2=m_norm_ffn2, w_ffn2_gu=m_w_ffn2_gu, w_ffn2_down=m_w_ffn2_down, norm_ple=m_norm_ple, w_ple_gate=m_w_ple_gate,
             w_ple_proj=m_w_ple_proj)
    v = dict(norm_ffn1=v_norm_ffn1, w_ffn1_gu=v_w_ffn1_gu, w_ffn1_down=v_w_ffn1_down, norm_mix=v_norm_mix, w_qkv=v_w_qkv,
             q_norm=v_q_norm, k_norm=v_k_norm, w_o=v_w_o, w_pool_in=v_w_pool_in, w_pool_grp=v_w_pool_grp, pool_scale=v_pool_scale,
             norm_ffn2=v_norm_ffn2, w_ffn2_gu=v_w_ffn2_gu, w_ffn2_down=v_w_ffn2_down, norm_ple=v_norm_ple, w_ple_gate=v_w_ple_gate,
             w_ple_proj=v_w_ple_proj)
    chip = 2 * lax.axis_index("x") + lax.axis_index("y")
    D = x.shape[-1]
    shard_cols = pool_scale.shape[1]

    core = lax.axis_index("c")
    shards = [w[k].astype(BF) for k in SHARDED]
    gathered = {k: lax.dynamic_update_slice_in_dim(g, s[:, None], chip, axis=1)
                for k, s, g in zip(SHARDED, shards, _gather_weights(shards, "gather_weights"))}
    scale_rows = jnp.zeros((8, shard_cols), F32).at[:pool_scale.shape[0]].set(pool_scale)
    scale_all = _gather_small(scale_rows, "gather_pool_scale")
    pool_scale_all = jnp.transpose(scale_all[::2, :pool_scale.shape[0]], (1, 0, 2)).reshape(pool_scale.shape[0], D)
    W = _whole_weights(w, gathered, pool_scale_all)

    sq, dx, G = _local_step(x[0], p[:, 0], loss_target[0], W)
    loss = lax.psum(0.5 / D * sq[0, 0], ("x", "y", "c"))

    partials = [_shard_major(k, G[k]) for k in SHARDED]
    theirs = _swap_partials(partials, "swap_partials")
    chip_sums = [_add_pair(lax.dynamic_slice_in_dim(g, core * t.shape[0], t.shape[0], axis=0), t, f"pair_{k}")
                 for k, g, t in zip(SHARDED, partials, theirs)]
    slots = [lax.dynamic_update_slice_in_dim(r, lax.dynamic_index_in_dim(s, chip, axis=1, keepdims=False)[None], chip, axis=0)
             for s, r in zip(chip_sums, _scatter_sums(chip_sums, "scatter_sums"))]
    halves = [_sum_slots(s, f"sum_{k}") for k, s in zip(SHARDED, slots)]
    grads = dict(zip(SHARDED, _join_halves(halves, "join_halves")))

    small = [G[k][i] for k in NORMS for i in range(len(G[k]))]
    small += [jnp.pad(jnp.concatenate(G[k], axis=1), ((0, 0), (0, D - len(G[k]) * HEAD_DIM))) for k in HEAD_GAINS]
    small += G["pool_scale"]
    small = jnp.concatenate(small + [jnp.zeros((SMALL_ROWS - len(small), D), F32)], axis=0)
    small = _sum_slots(_gather_small(small, "gather_small_grads"), "sum_small_grads")
    row = 0
    for k in NORMS:
        grads[k] = small[row:row + w[k].shape[0]]
        row += w[k].shape[0]
    for k in HEAD_GAINS:
        grads[k] = small[row, :w[k].size].reshape(w[k].shape)
        row += 1
    grads["pool_scale"] = lax.dynamic_slice_in_dim(small[row:row + pool_scale.shape[0]], chip * shard_cols, shard_cols, axis=1)

    delta, new_m, new_v = {}, {}, {}
    small_names = NORMS + HEAD_GAINS + ("pool_scale",)

    def pack(d):
        rows = [jnp.pad(d[k].reshape(-1, d[k].shape[-1]) if k in NORMS + ("pool_scale",) else d[k].reshape(1, -1),
                        ((0, 0), (0, D - (d[k].shape[-1] if k in NORMS + ("pool_scale",) else d[k].size))), constant_values=1.0)
                for k in small_names]
        n = sum(r.shape[0] for r in rows)
        return jnp.concatenate(rows + [jnp.ones((SMALL_ROWS - n, D), F32)], axis=0)

    packed = _adamw(pack(w), pack(grads), pack(m), pack(v), "adamw_small")
    row = 0
    for k in small_names:
        n = w[k].shape[0] if k in NORMS + ("pool_scale",) else 1
        width = w[k].shape[-1] if k in NORMS + ("pool_scale",) else w[k].size
        for dst, src in zip((delta, new_m, new_v), packed):
            dst[k] = src[row:row + n, :width].reshape(w[k].shape)
        row += n
    for k in SHARDED:
        delta[k], new_m[k], new_v[k] = _adamw(w[k], grads[k], m[k], v[k], f"adamw_{k}")

    return (loss, dx[None], *[grads[k] for k in WEIGHTS], *[delta[k] for k in WEIGHTS],
            *[new_m[k] for k in WEIGHTS], *[new_v[k] for k in WEIGHTS])
```

```python
import jax
import jax.numpy as jnp
from jax import lax
from jax.experimental import pallas as pl
from jax.experimental.pallas import tpu as pltpu

BF = jnp.bfloat16
F32 = jnp.float32

N_HEADS = 16
HEAD_DIM = 64
POOL_WINDOWS = (2, 4, 8, 16)
POOL_HALO = 16
EPS = 1e-6
ADAM_LR = 0.001
ADAM_B1 = 0.9
ADAM_B2 = 0.999
ADAM_EPS = 1e-08
ADAM_WD = 0.01
ADAM_STEP = 10

N_CHIPS = 4
N_DEV = 8
LANES = 128
VMEM_LIMIT = 56 * 1024 * 1024
ATT_TK = 256
ATT_TQ = 512
ATT_SLOTS = 2
MESH = pl.DeviceIdType.MESH

NN = (((1,), (0,)), ((), ()))
NT = (((1,), (1,)), ((), ()))
TN = (((0,), (0,)), ((), ()))


def _params(n_axes):
    return pltpu.CompilerParams(dimension_semantics=("arbitrary",) * n_axes, vmem_limit_bytes=VMEM_LIMIT)


def _dot(a, b, dims):
    return lax.dot_general(a, b, dims, preferred_element_type=F32)


def _layer(li, block, index_map):
    return pl.BlockSpec((None,) + tuple(block), lambda *g: (li,) + tuple(index_map(*g)))


def _mm(a, b, extra, out_shapes, *, dims, grid, a_spec, b_spec, extra_specs, out_specs, acc_shape, epilogue, name, into=None):
    nk = grid[-1]
    aliases = {}
    if into is not None:
        aliases = {2 + len(extra): 0}
        extra = [*extra, into]
        extra_specs = [*extra_specs, pl.BlockSpec(memory_space=pl.ANY)]
        out_shapes = [jax.ShapeDtypeStruct(into.shape, into.dtype), *out_shapes[1:]]
    n_extra = len(extra)
    n_out = len(out_shapes)

    def body(a_ref, b_ref, *rest):
        ex = rest[:n_extra]
        outs = rest[n_extra:n_extra + n_out]
        acc = rest[-1]
        k = pl.program_id(len(grid) - 1)

        @pl.when(k == 0)
        def _():
            acc[...] = jnp.zeros_like(acc)

        acc[...] += _dot(a_ref[...].astype(BF), b_ref[...].astype(BF), dims)
        first = pl.program_id(0) == 0

        @pl.when(k == nk - 1)
        def _():
            epilogue(acc[...], ex, outs, first)

    return pl.pallas_call(
        body, grid=grid, in_specs=[a_spec, b_spec, *extra_specs], out_specs=out_specs, out_shape=out_shapes,
        scratch_shapes=[pltpu.VMEM(acc_shape, F32)], compiler_params=_params(len(grid)), name=name,
        input_output_aliases=aliases,
    )(a, b, *extra)


def _store(dtype, scale=1.0):
    def ep(acc, ex, outs, first):
        outs[0][...] = (acc * scale).astype(dtype)
    return ep


def _residual(scale):
    def ep(acc, ex, outs, first):
        outs[0][...] = ex[0][...] + scale * acc
    return ep


def _rms_bwd_epilogue(acc, ex, outs, first):
    x_ref, g_ref, dx_ref = ex
    dxo_ref, dg_ref = outs
    x = x_ref[...]
    r = lax.rsqrt(jnp.mean(x * x, axis=-1, keepdims=True) + EPS)
    xh = x * r
    dxh = acc * g_ref[...]
    dxo_ref[...] = dx_ref[...] + r * (dxh - xh * jnp.mean(dxh * xh, axis=-1, keepdims=True))

    @pl.when(first)
    def _():
        dg_ref[...] = jnp.zeros_like(dg_ref)

    dg_ref[...] += jnp.sum(acc * xh, axis=0, keepdims=True)


def _mm_fwd(a, w, li, x, scale, tm, name):
    S, K = a.shape
    N = w.shape[2]
    return _mm(a, w, [x], [jax.ShapeDtypeStruct((S, N), F32)], dims=NN, grid=(S // tm, 1),
               a_spec=pl.BlockSpec((tm, K), lambda m, k: (m, 0)), b_spec=_layer(li, (K, N), lambda m, k: (0, 0)),
               extra_specs=[pl.BlockSpec((tm, N), lambda m, k: (m, 0))], out_specs=[pl.BlockSpec((tm, N), lambda m, k: (m, 0))],
               acc_shape=(tm, N), epilogue=_residual(scale), name=name)[0]


def _mm_plain(a, w, li, dtype, tm, name, dims=NN):
    S, K = a.shape
    N = w.shape[2] if dims == NN else w.shape[1]
    return _mm(a, w, [], [jax.ShapeDtypeStruct((S, N), dtype)], dims=dims, grid=(S // tm, 1),
               a_spec=pl.BlockSpec((tm, K), lambda m, k: (m, 0)), b_spec=_layer(li, w.shape[1:], lambda m, k: (0, 0)),
               extra_specs=[], out_specs=[pl.BlockSpec((tm, N), lambda m, k: (m, 0))],
               acc_shape=(tm, N), epilogue=_store(dtype), name=name)[0]


def _mm_cols(a, w, li, dtype, tm, name):
    S, K = a.shape
    _, nj, _, ns = w.shape
    return _mm(a, w, [], [jax.ShapeDtypeStruct((S, nj * ns), dtype)], dims=NN, grid=(S // tm, nj, 1),
               a_spec=pl.BlockSpec((tm, K), lambda m, j, k: (m, 0)), b_spec=_layer(li, (None, K, ns), lambda m, j, k: (j, 0, 0)),
               extra_specs=[], out_specs=[pl.BlockSpec((tm, ns), lambda m, j, k: (m, j))],
               acc_shape=(tm, ns), epilogue=_store(dtype), name=name)[0]


def _mm_wgrad(a, b, scale, into, li, tk, name, tn=None):
    S, M = a.shape
    N = b.shape[1]
    tn = N if tn is None else tn
    return _mm(a, b, [], [None], dims=TN, grid=(N // tn, S // tk),
               a_spec=pl.BlockSpec((tk, M), lambda n, k: (k, 0)), b_spec=pl.BlockSpec((tk, tn), lambda n, k: (k, n)),
               extra_specs=[], out_specs=[_layer(li, (M, tn), lambda n, k: (0, n))],
               acc_shape=(M, tn), epilogue=_store(BF, scale), name=name, into=into)[0]


def _mm_wgrad_cols(a, b, into, li, tk, name):
    S, M = a.shape
    _, nj, _, ns = into.shape
    return _mm(a, b, [], [None], dims=TN, grid=(nj, S // tk),
               a_spec=pl.BlockSpec((tk, M), lambda j, k: (k, 0)), b_spec=pl.BlockSpec((tk, ns), lambda j, k: (k, j)),
               extra_specs=[], out_specs=[_layer(li, (None, M, ns), lambda j, k: (j, 0, 0))],
               acc_shape=(M, ns), epilogue=_store(BF), name=name, into=into)[0]


def _mm_dx_norm(a, w, li, x, gain, dx, tm, name):
    S, K = a.shape
    N = w.shape[1]
    row = pl.BlockSpec((tm, N), lambda m, k: (m, 0))
    vec = pl.BlockSpec((1, N), lambda m, k: (0, 0))
    return _mm(a, w, [x, gain, dx], [jax.ShapeDtypeStruct((S, N), F32), jax.ShapeDtypeStruct((1, N), F32)], dims=NT,
               grid=(S // tm, 1), a_spec=pl.BlockSpec((tm, K), lambda m, k: (m, 0)), b_spec=_layer(li, (N, K), lambda m, k: (0, 0)),
               extra_specs=[row, vec, row], out_specs=[row, vec], acc_shape=(tm, N), epilogue=_rms_bwd_epilogue, name=name)


def _mm_dx_norm_cols(a, w, li, x, gain, dx, tm, name):
    S = a.shape[0]
    _, nj, N, ks = w.shape

    def body(a_ref, w_ref, x_ref, g_ref, dx_ref, dxo_ref, dg_ref):
        acc = _dot(a_ref[:, :ks], w_ref[0], NT)
        for j in range(1, nj):
            acc = acc + _dot(a_ref[:, j * ks:(j + 1) * ks], w_ref[j], NT)
        _rms_bwd_epilogue(acc, (x_ref, g_ref, dx_ref), (dxo_ref, dg_ref), pl.program_id(0) == 0)

    row = pl.BlockSpec((tm, N), lambda m: (m, 0))
    vec = pl.BlockSpec((1, N), lambda m: (0, 0))
    return pl.pallas_call(
        body, grid=(S // tm,),
        in_specs=[pl.BlockSpec((tm, nj * ks), lambda m: (m, 0)), _layer(li, (nj, N, ks), lambda m: (0, 0, 0)), row, vec, row],
        out_specs=[row, vec], out_shape=[jax.ShapeDtypeStruct((S, N), F32), jax.ShapeDtypeStruct((1, N), F32)],
        compiler_params=_params(1), name=name)(a, w, x, gain, dx)


def _rmsnorm(x, gain, tm, name):
    S, D = x.shape

    def body(x_ref, g_ref, h_ref):
        xv = x_ref[...]
        r = lax.rsqrt(jnp.mean(xv * xv, axis=-1, keepdims=True) + EPS)
        h_ref[...] = (xv * r * g_ref[...]).astype(BF)

    return pl.pallas_call(
        body, grid=(S // tm,), in_specs=[pl.BlockSpec((tm, D), lambda m: (m, 0)), pl.BlockSpec((1, D), lambda m: (0, 0))],
        out_specs=pl.BlockSpec((tm, D), lambda m: (m, 0)), out_shape=jax.ShapeDtypeStruct((S, D), BF),
        compiler_params=_params(1), name=name)(x, gain)


def _ffn_up(h, wgu, li, tm, name):
    S, D = h.shape
    ns = wgu.shape[3]
    half = wgu.shape[1] // 2

    def body(h_ref, wg_ref, wu_ref, g_ref, u_ref, act_ref):
        hv = h_ref[...]
        g = _dot(hv, wg_ref[...], NN)
        u = _dot(hv, wu_ref[...], NN)
        g_ref[...] = g.astype(BF)
        u_ref[...] = u.astype(BF)
        act_ref[...] = (g * jax.nn.sigmoid(g) * u).astype(BF)

    out = jax.ShapeDtypeStruct((S, half * ns), BF)
    tile = pl.BlockSpec((tm, ns), lambda j, m: (m, j))
    return pl.pallas_call(
        body, grid=(half, S // tm),
        in_specs=[pl.BlockSpec((tm, D), lambda j, m: (m, 0)), _layer(li, (None, D, ns), lambda j, m: (j, 0, 0)),
                  _layer(li, (None, D, ns), lambda j, m: (j + half, 0, 0))],
        out_specs=[tile, tile, tile], out_shape=[out, out, out], compiler_params=_params(2), name=name)(h, wgu, wgu)


def _ffn_dact(dx, wdown, li, g, u, tm, name):
    S, D = dx.shape
    F = wdown.shape[1]

    def ep(acc, ex, outs, first):
        gv = ex[0][...].astype(F32)
        uv = ex[1][...].astype(F32)
        da = 0.5 * acc
        sg = jax.nn.sigmoid(gv)
        outs[0][:, :F] = (da * uv * (sg * (1.0 + gv * (1.0 - sg)))).astype(BF)
        outs[0][:, F:] = (da * (gv * sg)).astype(BF)

    row = pl.BlockSpec((tm, F), lambda m, k: (m, 0))
    return _mm(dx, wdown, [g, u], [jax.ShapeDtypeStruct((S, 2 * F), BF)], dims=NT, grid=(S // tm, 1),
               a_spec=pl.BlockSpec((tm, D), lambda m, k: (m, 0)), b_spec=_layer(li, (F, D), lambda m, k: (0, 0)),
               extra_specs=[row, row], out_specs=[pl.BlockSpec((tm, 2 * F), lambda m, k: (m, 0))],
               acc_shape=(tm, F), epilogue=ep, name=name)[0]


def _ple_fwd(x, gain, wgate, li, p, wproj, tm, name):
    S, D = x.shape
    P = p.shape[1]

    def body(x_ref, g_ref, wg_ref, p_ref, wp_ref, xo_ref, hp_ref, gp_ref, pe_ref):
        xv = x_ref[...]
        r = lax.rsqrt(jnp.mean(xv * xv, axis=-1, keepdims=True) + EPS)
        hp = (xv * r * g_ref[...]).astype(BF)
        gp = _dot(hp, wg_ref[...], NN)
        pe = _dot(p_ref[...].astype(BF), wp_ref[...], NN)
        xo_ref[...] = xv + jax.nn.sigmoid(gp) * pe
        hp_ref[...] = hp
        gp_ref[...] = gp.astype(BF)
        pe_ref[...] = pe.astype(BF)

    row = pl.BlockSpec((tm, D), lambda m: (m, 0))
    return pl.pallas_call(
        body, grid=(S // tm,),
        in_specs=[row, pl.BlockSpec((1, D), lambda m: (0, 0)), _layer(li, (D, D), lambda m: (0, 0)),
                  pl.BlockSpec((tm, P), lambda m: (m, 0)), pl.BlockSpec((P, D), lambda m: (0, 0))],
        out_specs=[row, row, row, row],
        out_shape=[jax.ShapeDtypeStruct((S, D), F32)] + [jax.ShapeDtypeStruct((S, D), BF)] * 3,
        compiler_params=_params(1), name=name)(x, gain, wgate, p, wproj)


def _ple_bwd(dx, gp, pe, wgate, li, x, gain, tm, name):
    S, D = dx.shape

    def body(dx_ref, gp_ref, pe_ref, wg_ref, x_ref, g_ref, dxo_ref, dgp_ref, dpe_ref, dg_ref):
        dxv = dx_ref[...]
        sg = jax.nn.sigmoid(gp_ref[...].astype(F32))
        dgp = (dxv * pe_ref[...].astype(F32) * (sg * (1.0 - sg))).astype(BF)
        dgp_ref[...] = dgp
        dpe_ref[...] = (dxv * sg).astype(BF)
        dhp = _dot(dgp, wg_ref[...], NT)
        _rms_bwd_epilogue(dhp, (x_ref, g_ref, dx_ref), (dxo_ref, dg_ref), pl.program_id(0) == 0)

    row = pl.BlockSpec((tm, D), lambda m: (m, 0))
    vec = pl.BlockSpec((1, D), lambda m: (0, 0))
    return pl.pallas_call(
        body, grid=(S // tm,), in_specs=[row, row, row, _layer(li, (D, D), lambda m: (0, 0)), row, vec],
        out_specs=[row, row, row, vec],
        out_shape=[jax.ShapeDtypeStruct((S, D), F32), jax.ShapeDtypeStruct((S, D), BF), jax.ShapeDtypeStruct((S, D), BF),
                   jax.ShapeDtypeStruct((1, D), F32)],
        compiler_params=_params(1), name=name)(dx, gp, pe, wgate, x, gain)


def _loss_head(y, target, tm, name):
    S, D = y.shape

    def body(y_ref, t_ref, sq_ref, dy_ref):
        d = y_ref[...] - t_ref[...]
        dy_ref[...] = d * (1.0 / D)

        @pl.when(pl.program_id(0) == 0)
        def _():
            sq_ref[...] = jnp.zeros_like(sq_ref)

        sq_ref[...] += jnp.sum(d * d, keepdims=True)

    row = pl.BlockSpec((tm, D), lambda m: (m, 0))
    return pl.pallas_call(
        body, grid=(S // tm,), in_specs=[row, row], out_specs=[pl.BlockSpec((1, 1), lambda m: (0, 0)), row],
        out_shape=[jax.ShapeDtypeStruct((1, 1), F32), jax.ShapeDtypeStruct((S, D), F32)],
        compiler_params=_params(1), name=name)(y, target)


def _qk_norm(qkv, qgain, kgain, tm, name):
    S = qkv.shape[0]
    D = qkv.shape[1] // 3
    nb = D // LANES

    def norm2(t, gain):
        lo = lax.broadcasted_iota(jnp.int32, t.shape, 1) < HEAD_DIM
        sq = t * t
        s_lo = jnp.sum(jnp.where(lo, sq, 0.0), axis=1, keepdims=True)
        s_hi = jnp.sum(jnp.where(lo, 0.0, sq), axis=1, keepdims=True)
        r = lax.rsqrt(jnp.where(lo, s_lo, s_hi) * (1.0 / HEAD_DIM) + EPS)
        return t * r * gain

    def body(q_ref, k_ref, v_ref, qg_ref, kg_ref, qo_ref, ko_ref, vo_ref):
        qo_ref[...] = (norm2(q_ref[...], qg_ref[...]) * (HEAD_DIM ** -0.5)).astype(BF)
        ko_ref[...] = norm2(k_ref[...], kg_ref[...]).astype(BF)
        vo_ref[...] = v_ref[...].astype(BF)

    vec = pl.BlockSpec((1, LANES), lambda m, h: (0, 0))
    tile = pl.BlockSpec((tm, LANES), lambda m, h: (m, h))
    out = jax.ShapeDtypeStruct((S, D), BF)
    return pl.pallas_call(
        body, grid=(S // tm, nb),
        in_specs=[tile, pl.BlockSpec((tm, LANES), lambda m, h: (m, nb + h)), pl.BlockSpec((tm, LANES), lambda m, h: (m, 2 * nb + h)), vec, vec],
        out_specs=[tile, tile, tile], out_shape=[out, out, out], compiler_params=_params(2), name=name)(qkv, qkv, qkv, qgain, kgain)


def _qk_norm_bwd(qkv, dqs, dkn, dv, qgain, kgain, tm, name):
    S = qkv.shape[0]
    D = qkv.shape[1] // 3
    nb = D // LANES

    def norm2_bwd(t, gain, dn):
        lo = lax.broadcasted_iota(jnp.int32, t.shape, 1) < HEAD_DIM

        def headsum(val):
            s_lo = jnp.sum(jnp.where(lo, val, 0.0), axis=1, keepdims=True)
            s_hi = jnp.sum(jnp.where(lo, 0.0, val), axis=1, keepdims=True)
            return jnp.where(lo, s_lo, s_hi)

        r = lax.rsqrt(headsum(t * t) * (1.0 / HEAD_DIM) + EPS)
        th = t * r
        dth = dn * gain
        dt = r * (dth - th * (headsum(dth * th) * (1.0 / HEAD_DIM)))
        return dt, jnp.sum(dn * th, axis=0, keepdims=True)

    def body(q_ref, k_ref, dq_ref, dk_ref, dv_ref, qg_ref, kg_ref, dqo_ref, dko_ref, dvo_ref, dqg_ref, dkg_ref):
        dq, dqg = norm2_bwd(q_ref[...], qg_ref[...], dq_ref[...] * (HEAD_DIM ** -0.5))
        dk, dkg = norm2_bwd(k_ref[...], kg_ref[...], dk_ref[...])
        dqo_ref[...] = dq.astype(BF)
        dko_ref[...] = dk.astype(BF)
        dvo_ref[...] = dv_ref[...].astype(BF)

        @pl.when((pl.program_id(0) == 0) & (pl.program_id(1) == 0))
        def _():
            dqg_ref[...] = jnp.zeros_like(dqg_ref)
            dkg_ref[...] = jnp.zeros_like(dkg_ref)

        dqg_ref[...] += dqg
        dkg_ref[...] += dkg

    vec = pl.BlockSpec((1, LANES), lambda m, h: (0, 0))
    tile = pl.BlockSpec((tm, LANES), lambda m, h: (m, h))
    return pl.pallas_call(
        body, grid=(S // tm, nb),
        in_specs=[tile, pl.BlockSpec((tm, LANES), lambda m, h: (m, nb + h)), tile, tile, tile, vec, vec],
        out_specs=[tile, tile, tile, vec, vec],
        out_shape=[jax.ShapeDtypeStruct((S, D), BF)] * 3 + [jax.ShapeDtypeStruct((1, LANES), F32)] * 2,
        compiler_params=_params(2), name=name)(qkv, qkv, dqs, dkn, dv, qgain, kgain)


def _split_bf16(t):
    hi = t.astype(BF)
    return hi, (t - hi.astype(F32)).astype(BF)


HEADS_PER_BLOCK = LANES // HEAD_DIM


def _suffix_sums(t, from_here):
    hi, lo = _split_bf16(t)
    return _dot(hi, from_here, NN) + _dot(lo, from_here, NN)


def _attn_tiles(qs, ks, stays, valid, after):
    zs = [_dot(q, k, NT) for q, k in zip(qs, ks)]
    sps = [jnp.maximum(z, 0.0) + jnp.log(1.0 + jnp.exp(-jnp.abs(z))) for z in zs]
    if valid is not None:
        sps = [jnp.where(valid, sp, 0.0) for sp in sps]
    rights = [_dot(sp.astype(BF), after, NN) for sp in sps]
    ws = [jnp.exp((z - sp) - r + stay) for z, sp, r, stay in zip(zs, sps, rights, stays)]
    if valid is not None:
        ws = [jnp.where(valid, w, 0.0) for w in ws]
    return ws, [r[:, :1] + sp[:, :1] for r, sp in zip(rights, sps)]


def _attn_walk(qi, tq, tk, block, carry):
    n_diag = tq // tk
    row = lax.broadcasted_iota(jnp.int32, (tq, tk), 0)
    col = lax.broadcasted_iota(jnp.int32, (tq, tk), 1)
    for n, d in enumerate(reversed(range(n_diag))):
        carry = block(n, n_diag * qi + d, col < row - d * tk, carry)
    return lax.fori_loop(0, n_diag * qi, lambda i, c: block(n_diag + i, n_diag * qi - 1 - i, None, c), carry)


def _tile_slot(qi, kj, tq, tk):
    n_diag = tq // tk
    return n_diag * (qi * (qi + 1) // 2) + kj


def _key_order(tk):
    j = lax.broadcasted_iota(jnp.int32, (tk, tk), 0)
    s = lax.broadcasted_iota(jnp.int32, (tk, tk), 1)
    return (j > s).astype(BF), (j >= s).astype(BF)


def _attn_fwd(qs, kn, vb, name):
    S, D = qs.shape
    TQ, TK = ATT_TQ, ATT_TK
    heads = [slice(hh * HEAD_DIM, (hh + 1) * HEAD_DIM) for hh in range(HEADS_PER_BLOCK)]

    assert TQ // TK == ATT_SLOTS

    def body(q_ref, k_ref, v_ref, o_ref, tiles_ref, buf, sems):
        hp, qi = pl.program_id(0), pl.program_id(1)
        after, _ = _key_order(TK)
        q = [q_ref[:, lanes] for lanes in heads]

        def save(slot, kj):
            return pltpu.make_async_copy(buf.at[slot], tiles_ref.at[hp, _tile_slot(qi, kj, TQ, TK)], sems.at[slot])

        def block(n, kj, valid, carry):
            slot = n % ATT_SLOTS
            rows = pl.ds(pl.multiple_of(kj * TK, TK), TK)
            ws, totals = _attn_tiles(q, [k_ref[rows, lanes] for lanes in heads], [c[1] for c in carry], valid, after)
            wbs = [w.astype(BF) for w in ws]
            outs = [_dot(wb, v_ref[rows, lanes], NN) for wb, lanes in zip(wbs, heads)]
            if not isinstance(n, int):
                save(slot, kj).wait()
            for hh, wb in enumerate(wbs):
                buf[slot, hh] = wb
            save(slot, kj).start()
            return tuple((c[0] + o, c[1] - t) for c, o, t in zip(carry, outs, totals))

        carry = tuple((jnp.zeros((TQ, HEAD_DIM), F32), jnp.zeros((TQ, 1), F32)) for _ in heads)
        carry = _attn_walk(qi, TQ, TK, block, carry)
        for slot in range(ATT_SLOTS):
            save(slot, 0).wait()
        for hh, lanes in enumerate(heads):
            o_ref[:, lanes] = carry[hh][0]

    tile = pl.BlockSpec((TQ, LANES), lambda h, m: (m, h))
    full = pl.BlockSpec((S, LANES), lambda h, m: (0, h))
    n_tiles = _tile_slot(S // TQ, 0, TQ, TK)
    return pl.pallas_call(
        body, grid=(D // LANES, S // TQ), in_specs=[tile, full, full], out_specs=[tile, ANY],
        out_shape=[jax.ShapeDtypeStruct((S, D), F32), jax.ShapeDtypeStruct((D // LANES, n_tiles, HEADS_PER_BLOCK, TQ, TK), BF)],
        scratch_shapes=[pltpu.VMEM((ATT_SLOTS, HEADS_PER_BLOCK, TQ, TK), BF), pltpu.SemaphoreType.DMA((ATT_SLOTS,))],
        compiler_params=_params(2), name=name)(qs, kn, vb)


def _attn_bwd(qs, kn, vb, o, do, tiles, name):
    S, D = qs.shape
    TQ, TK = ATT_TQ, ATT_TK
    heads = [slice(hh * HEAD_DIM, (hh + 1) * HEAD_DIM) for hh in range(HEADS_PER_BLOCK)]

    n_diag = TQ // TK

    def body(q_ref, k_ref, v_ref, o_ref, do_ref, tiles_ref, dq_ref, dk_ref, dv_ref, buf, sems):
        hp, qi = pl.program_id(0), pl.program_id(1)

        @pl.when(qi == 0)
        def _():
            dk_ref[...] = jnp.zeros_like(dk_ref)
            dv_ref[...] = jnp.zeros_like(dv_ref)

        _, from_here = _key_order(TK)
        q = [q_ref[:, lanes] for lanes in heads]
        dout = [do_ref[:, lanes] for lanes in heads]
        n_blocks = n_diag * (qi + 1)

        def fetch(slot, n):
            return pltpu.make_async_copy(tiles_ref.at[hp, _tile_slot(qi, n_blocks - 1 - n, TQ, TK)], buf.at[slot], sems.at[slot])

        for slot in range(ATT_SLOTS):
            fetch(slot, slot).start()

        def block(n, kj, valid, carry):
            slot = n % ATT_SLOTS
            rows = pl.ds(pl.multiple_of(kj * TK, TK), TK)
            ks = [k_ref[rows, lanes] for lanes in heads]
            betas = [jax.nn.sigmoid(_dot(qh, k, NT)) for qh, k in zip(q, ks)]
            das = [_dot(d, v_ref[rows, lanes], NT) for d, lanes in zip(dout, heads)]
            fetch(slot, n).wait()
            wbs = [buf[slot, hh] for hh in range(HEADS_PER_BLOCK)]
            gs = [wb.astype(F32) * da for wb, da in zip(wbs, das)]
            g_sums = [_suffix_sums(g, from_here) for g in gs]
            dzs = [g - beta * (g + (c[1] - gsum)) for g, beta, c, gsum in zip(gs, betas, carry, g_sums)]
            if valid is not None:
                dzs = [jnp.where(valid, dz, 0.0) for dz in dzs]
            dzbs = [dz.astype(BF) for dz in dzs]
            for hh, lanes in enumerate(heads):
                dk_ref[rows, lanes] += _dot(dzbs[hh], q[hh], TN)
                dv_ref[rows, lanes] += _dot(wbs[hh], dout[hh], TN)
            new = tuple((c[0] + _dot(dzb, k, NN), c[1] - gsum[:, :1]) for c, dzb, k, gsum in zip(carry, dzbs, ks, g_sums))

            @pl.when(n + ATT_SLOTS < n_blocks)
            def _():
                fetch(slot, n + ATT_SLOTS).start()

            return new

        carry = tuple((jnp.zeros((TQ, HEAD_DIM), F32),
                       jnp.sum(dout[hh].astype(F32) * o_ref[:, lanes], axis=1, keepdims=True)) for hh, lanes in enumerate(heads))
        carry = _attn_walk(qi, TQ, TK, block, carry)
        for hh, lanes in enumerate(heads):
            dq_ref[:, lanes] = carry[hh][0]

    tile = pl.BlockSpec((TQ, LANES), lambda h, m: (m, h))
    full = pl.BlockSpec((S, LANES), lambda h, m: (0, h))
    out = jax.ShapeDtypeStruct((S, D), F32)
    return pl.pallas_call(
        body, grid=(D // LANES, S // TQ), in_specs=[tile, full, full, tile, tile, ANY], out_specs=[tile, full, full],
        out_shape=[out, out, out],
        scratch_shapes=[pltpu.VMEM((ATT_SLOTS, HEADS_PER_BLOCK, TQ, TK), BF), pltpu.SemaphoreType.DMA((ATT_SLOTS,))],
        compiler_params=_params(2), name=name)(qs, kn, vb, o, do, tiles)


def _pool_counts(T, first_row):
    pos = first_row + lax.broadcasted_iota(jnp.int32, (T, 1), 0)
    return [jnp.minimum(pos + 1, w).astype(F32) for w in POOL_WINDOWS]


def _pool_fwd(u, wgrp, scale, x, tm, name):
    S, D = u.shape
    G = len(POOL_WINDOWS)
    C = D // G
    H = POOL_HALO

    def body(u_ref, prev_ref, w_ref, s_ref, x_ref, xo_ref, pooled_ref):
        m = pl.program_id(0)
        prev = jnp.where(m == 0, 0.0, prev_ref[...])
        ext = jnp.concatenate([prev, u_ref[...]], axis=0)
        counts = _pool_counts(tm, m * tm)
        ys = []
        acc = ext
        shift = 1
        for gi, w in enumerate(POOL_WINDOWS):
            while shift < w:
                acc = acc + pltpu.roll(acc, shift, axis=0)
                shift *= 2
            cols = slice(gi * C, (gi + 1) * C)
            pooled = (acc[H:, cols] / counts[gi] - ext[H:, cols]).astype(BF)
            pooled_ref[:, cols] = pooled
            ys.append(_dot(pooled, w_ref[gi], NN))
        xo_ref[...] = x_ref[...] + jnp.concatenate(ys, axis=1) * s_ref[...]

    row = pl.BlockSpec((tm, D), lambda m: (m, 0))
    return pl.pallas_call(
        body, grid=(S // tm,),
        in_specs=[row, pl.BlockSpec((H, D), lambda m: (jnp.maximum(m * (tm // H) - 1, 0), 0)),
                  pl.BlockSpec((G, C, C), lambda m: (0, 0, 0)), pl.BlockSpec((1, D), lambda m: (0, 0)), row],
        out_specs=[row, row], out_shape=[jax.ShapeDtypeStruct((S, D), F32), jax.ShapeDtypeStruct((S, D), BF)],
        compiler_params=_params(1), name=name)(u, u, wgrp, scale, x)


def _pool_bwd_grp(dx, pooled, wgrp, scale, tm, name):
    S, D = dx.shape
    G = len(POOL_WINDOWS)
    C = D // G

    def body(dx_ref, pooled_ref, w_ref, s_ref, dp_ref, dw_ref, ds_ref, dw_acc):
        m = pl.program_id(0)

        @pl.when(m == 0)
        def _():
            dw_acc[...] = jnp.zeros_like(dw_acc)
            ds_ref[...] = jnp.zeros_like(ds_ref)

        dxv = dx_ref[...]
        dy = (dxv * s_ref[...]).astype(BF)
        ys = []
        for gi in range(G):
            cols = slice(gi * C, (gi + 1) * C)
            pg = pooled_ref[:, cols]
            ys.append(_dot(pg, w_ref[gi], NN))
            dw_acc[gi] += _dot(pg, dy[:, cols], TN)
            dp_ref[:, cols] = _dot(dy[:, cols], w_ref[gi], NT)
        ds_ref[...] += jnp.sum(dxv * jnp.concatenate(ys, axis=1), axis=0, keepdims=True)

        @pl.when(m == S // tm - 1)
        def _():
            dw_ref[...] = dw_acc[...].astype(BF)

    row = pl.BlockSpec((tm, D), lambda m: (m, 0))
    wspec = pl.BlockSpec((G, C, C), lambda m: (0, 0, 0))
    vec = pl.BlockSpec((1, D), lambda m: (0, 0))
    return pl.pallas_call(
        body, grid=(S // tm,), in_specs=[row, row, wspec, vec], out_specs=[row, wspec, vec],
        out_shape=[jax.ShapeDtypeStruct((S, D), F32), jax.ShapeDtypeStruct((G, C, C), BF), jax.ShapeDtypeStruct((1, D), F32)],
        scratch_shapes=[pltpu.VMEM((G, C, C), F32)], compiler_params=_params(1), name=name)(dx, pooled, wgrp, scale)


def _pool_bwd_window(dp, tm, name):
    S, D = dp.shape
    G = len(POOL_WINDOWS)
    C = D // G
    H = POOL_HALO
    last = S // tm - 1

    def body(dp_ref, next_ref, du_ref):
        m = pl.program_id(0)
        counts = _pool_counts(tm + H, m * tm)
        nxt = jnp.where(m == last, 0.0, next_ref[...])
        ext = jnp.concatenate([dp_ref[...], nxt], axis=0)
        for gi, w in enumerate(POOL_WINDOWS):
            cols = slice(gi * C, (gi + 1) * C)
            acc = ext[:, cols] / counts[gi]
            shift = 1
            while shift < w:
                acc = acc + pltpu.roll(acc, tm + H - shift, axis=0)
                shift *= 2
            du_ref[:, cols] = (acc[:tm] - ext[:tm, cols]).astype(BF)

    row = pl.BlockSpec((tm, D), lambda m: (m, 0))
    return pl.pallas_call(
        body, grid=(S // tm,),
        in_specs=[row, pl.BlockSpec((H, D), lambda m: (jnp.minimum((m + 1) * (tm // H), S // H - 1), 0))],
        out_specs=row, out_shape=jax.ShapeDtypeStruct((S, D), BF), compiler_params=_params(1), name=name)(dp, dp)


ELEMENTWISE_TILE_BYTES = 1 << 20


def _row_tile(rows, row_bytes):
    for cand in (512, 256, 128, 64, 32, 16, 8):
        if rows % cand == 0 and cand * row_bytes <= ELEMENTWISE_TILE_BYTES:
            return cand
    return rows


def _adamw(w, g, m, v, name):
    shape = w.shape
    cols = shape[-1]
    rows = w.size // cols
    tr = _row_tile(rows, cols * 4)

    def body(w_ref, g_ref, m_ref, v_ref, d_ref, mo_ref, vo_ref):
        gv = g_ref[...]
        mn = ADAM_B1 * m_ref[...] + (1.0 - ADAM_B1) * gv
        vn = ADAM_B2 * v_ref[...] + (1.0 - ADAM_B2) * jnp.square(gv)
        m_hat = mn / (1.0 - ADAM_B1 ** ADAM_STEP)
        v_hat = vn / (1.0 - ADAM_B2 ** ADAM_STEP)
        d_ref[...] = -ADAM_LR * (m_hat / (jnp.sqrt(v_hat) + ADAM_EPS) + ADAM_WD * w_ref[...])
        mo_ref[...] = mn
        vo_ref[...] = vn

    tile = pl.BlockSpec((tr, cols), lambda i: (i, 0))
    out = jax.ShapeDtypeStruct((rows, cols), F32)
    res = pl.pallas_call(
        body, grid=(rows // tr,), in_specs=[tile] * 4, out_specs=[tile] * 3, out_shape=[out] * 3,
        compiler_params=_params(1), name=name)(*[t.reshape(rows, cols) for t in (w, g, m, v)])
    return [t.reshape(shape) for t in res]


def _sum_slots(r, name):
    n = r.shape[0]
    shape = r.shape[1:]
    cols = shape[-1]
    rows = r.size // (n * cols)
    tr = _row_tile(rows, n * cols * r.dtype.itemsize)

    def body(r_ref, o_ref):
        acc = r_ref[0].astype(F32)
        for d in range(1, n):
            acc = acc + r_ref[d].astype(F32)
        o_ref[...] = acc

    return pl.pallas_call(
        body, grid=(rows // tr,), in_specs=[pl.BlockSpec((n, tr, cols), lambda i: (0, i, 0))],
        out_specs=pl.BlockSpec((tr, cols), lambda i: (i, 0)), out_shape=jax.ShapeDtypeStruct((rows, cols), F32),
        compiler_params=_params(1), name=name)(r.reshape(n, rows, cols)).reshape(shape)


def _add_pair(a, b, name):
    shape = a.shape
    cols = shape[-1]
    rows = a.size // cols
    tr = _row_tile(rows, cols * 4)

    def body(a_ref, b_ref, o_ref):
        o_ref[...] = (a_ref[...].astype(F32) + b_ref[...].astype(F32)).astype(BF)

    tile = pl.BlockSpec((tr, cols), lambda i: (i, 0))
    return pl.pallas_call(
        body, grid=(rows // tr,), in_specs=[tile, tile], out_specs=tile, out_shape=jax.ShapeDtypeStruct((rows, cols), BF),
        compiler_params=_params(1), name=name)(a.reshape(rows, cols), b.reshape(rows, cols)).reshape(shape)


ANY = pl.BlockSpec(memory_space=pl.ANY)


def _position():
    return lax.axis_index("x"), lax.axis_index("y"), lax.axis_index("c")


def _gather_small(t, name):
    rows, cols = t.shape

    def body(t_ref, o_ref, send_sems, recv_sems):
        x, y, c = _position()
        me = 4 * x + 2 * y + c
        o_ref[me] = t_ref[...]
        copies = []
        for k in range(1, N_DEV):
            peer = (x ^ (k >> 2), y ^ ((k >> 1) & 1), c ^ (k & 1))
            cp = pltpu.make_async_remote_copy(src_ref=t_ref, dst_ref=o_ref.at[me], send_sem=send_sems.at[k - 1],
                                              recv_sem=recv_sems.at[k - 1], device_id=peer, device_id_type=MESH)
            cp.start()
            copies.append((cp, 4 * peer[0] + 2 * peer[1] + peer[2]))
        for k, (cp, src) in enumerate(copies):
            pltpu.make_async_remote_copy(src_ref=t_ref, dst_ref=o_ref.at[src], send_sem=send_sems.at[k], recv_sem=recv_sems.at[k],
                                         device_id=(x, y, c), device_id_type=MESH).wait_recv()
        for cp, _ in copies:
            cp.wait_send()

    return pl.pallas_call(
        body, in_specs=[pl.BlockSpec(memory_space=pltpu.VMEM)], out_specs=pl.BlockSpec(memory_space=pltpu.VMEM),
        out_shape=jax.ShapeDtypeStruct((N_DEV, rows, cols), F32),
        scratch_shapes=[pltpu.SemaphoreType.DMA((N_DEV - 1,)), pltpu.SemaphoreType.DMA((N_DEV - 1,))], name=name)(t)


def _other_chips(x, y):
    return [(1 - x, y), (x, 1 - y), (1 - x, 1 - y)]


def _remote(src, dst, send_sems, recv_sems, k, to):
    return pltpu.make_async_remote_copy(src_ref=src, dst_ref=dst, send_sem=send_sems.at[k], recv_sem=recv_sems.at[k],
                                        device_id=to, device_id_type=MESH)


def _gather_weights(shards, name):
    n = len(shards)

    def body(*refs):
        ins, outs = refs[:n], refs[n:2 * n]
        ici_send, ici_recv, d2d_send, d2d_recv = refs[2 * n:]
        x, y, c = _position()
        me, sibling, chip = (x, y, c), (x, y, 1 - c), 2 * x + y
        chips = _other_chips(x, y)
        halves = [s.shape[0] // 2 for s in shards]
        sends = []
        for p in range(n):
            mine = pl.ds(c * halves[p], halves[p])
            for j, (px, py) in enumerate(chips):
                sends.append(_remote(ins[p].at[mine], outs[p].at[mine, chip], ici_send, ici_recv, (p, j), (px, py, c)))
                sends[-1].start()
        for p in range(n):
            mine = pl.ds(c * halves[p], halves[p])
            for j, (px, py) in enumerate(chips):
                landed = outs[p].at[mine, 2 * px + py]
                _remote(ins[p].at[mine], landed, ici_send, ici_recv, (p, j), me).wait_recv()
                sends.append(_remote(landed, landed, d2d_send, d2d_recv, (p, j), sibling))
                sends[-1].start()
        for p in range(n):
            theirs = pl.ds((1 - c) * halves[p], halves[p])
            for j, (px, py) in enumerate(chips):
                passed = outs[p].at[theirs, 2 * px + py]
                _remote(passed, passed, d2d_send, d2d_recv, (p, j), me).wait_recv()
        for cp in sends:
            cp.wait_send()

    return pl.pallas_call(
        body, in_specs=[ANY] * n, out_specs=[ANY] * n,
        out_shape=[jax.ShapeDtypeStruct((s.shape[0], N_CHIPS) + s.shape[1:], s.dtype) for s in shards],
        scratch_shapes=[pltpu.SemaphoreType.DMA((n, 3))] * 4, name=name)(*shards)


def _swap_partials(grads, name):
    n = len(grads)

    def body(*refs):
        ins, outs = refs[:n], refs[n:2 * n]
        send_sems, recv_sems = refs[2 * n:]
        x, y, c = _position()
        copies = []
        for p in range(n):
            half = grads[p].shape[0] // 2
            copies.append(_remote(ins[p].at[pl.ds((1 - c) * half, half)], outs[p], send_sems, recv_sems, p, (x, y, 1 - c)))
            copies[-1].start()
        for cp in copies:
            cp.wait()

    return pl.pallas_call(
        body, in_specs=[ANY] * n, out_specs=[ANY] * n,
        out_shape=[jax.ShapeDtypeStruct((g.shape[0] // 2,) + g.shape[1:], g.dtype) for g in grads],
        scratch_shapes=[pltpu.SemaphoreType.DMA((n,))] * 2, name=name)(*grads)


def _scatter_sums(sums, name):
    n = len(sums)

    def body(*refs):
        ins, outs = refs[:n], refs[n:2 * n]
        send_sems, recv_sems = refs[2 * n:]
        x, y, c = _position()
        chip = 2 * x + y
        chips = _other_chips(x, y)
        sends = []
        for p in range(n):
            layers = pl.ds(0, sums[p].shape[0])
            for j, (px, py) in enumerate(chips):
                sends.append(_remote(ins[p].at[layers, 2 * px + py], outs[p].at[chip], send_sems, recv_sems, (p, j), (px, py, c)))
                sends[-1].start()
        for p in range(n):
            layers = pl.ds(0, sums[p].shape[0])
            for j, (px, py) in enumerate(chips):
                _remote(ins[p].at[layers, chip], outs[p].at[2 * px + py], send_sems, recv_sems, (p, j), (x, y, c)).wait_recv()
        for cp in sends:
            cp.wait_send()

    return pl.pallas_call(
        body, in_specs=[ANY] * n, out_specs=[ANY] * n,
        out_shape=[jax.ShapeDtypeStruct((N_CHIPS, s.shape[0]) + s.shape[2:], s.dtype) for s in sums],
        scratch_shapes=[pltpu.SemaphoreType.DMA((n, 3))] * 2, name=name)(*sums)


def _join_halves(halves, name):
    n = len(halves)
    flat = [h.reshape(-1, h.shape[-1]) for h in halves]

    def body(*refs):
        ins, outs = refs[:n], refs[n:2 * n]
        send_sems, recv_sems = refs[2 * n:]
        x, y, c = _position()
        copies = []
        for p in range(n):
            cp = pltpu.make_async_remote_copy(src_ref=ins[p], dst_ref=outs[p], send_sem=send_sems.at[p], recv_sem=recv_sems.at[p],
                                              device_id=(x, y, 1 - c), device_id_type=MESH)
            cp.start()
            copies.append(cp)
        for cp in copies:
            cp.wait()

    theirs = pl.pallas_call(
        body, in_specs=[ANY] * n, out_specs=[ANY] * n, out_shape=[jax.ShapeDtypeStruct(f.shape, f.dtype) for f in flat],
        scratch_shapes=[pltpu.SemaphoreType.DMA((n,)), pltpu.SemaphoreType.DMA((n,))], name=name)(*flat)
    south = lax.axis_index("c") == 0
    return [jnp.concatenate([jnp.where(south, h, t.reshape(h.shape)), jnp.where(south, t.reshape(h.shape), h)], axis=0)
            for h, t in zip(halves, theirs)]


TM = 512


def _row(v):
    return v.reshape(1, -1)


def _ffn_forward(x, gain, wgu, wdown, li, tag):
    h = _rmsnorm(x, gain, TM, f"norm_{tag}")
    g, u, act = _ffn_up(h, wgu, li, TM, f"ffn_up_{tag}")
    xo = _mm_fwd(act, wdown, li, x, 0.5, TM, f"ffn_down_{tag}")
    return xo, (x, h, g, u, act)


def _ffn_backward(dx, saved, gain, wgu, wdown, d_wgu, d_wdown, li, tag):
    x, h, g, u, act = saved
    dgu = _ffn_dact(dx, wdown, li, g, u, 256, f"ffn_dact_{tag}")
    d_wdown = _mm_wgrad(act, dx, 0.5, d_wdown, li, TM, f"ffn_dwdown_{tag}", tn=512)
    d_wgu = _mm_wgrad_cols(h, dgu, d_wgu, li, TM, f"ffn_dwgu_{tag}")
    dx, d_gain = _mm_dx_norm_cols(dgu, wgu, li, x, gain, dx, 256, f"ffn_dx_{tag}")
    return dx, d_gain, d_wgu, d_wdown


STACKED = ("w_ffn1_gu", "w_ffn1_down", "w_qkv", "w_o", "w_pool_in", "w_ffn2_gu", "w_ffn2_down", "w_ple_gate")


def _local_step(x, p, target, W):
    depth = p.shape[0]
    saved = []
    for i in range(depth):
        j = i // 2
        s = {}
        x, s["ffn1"] = _ffn_forward(x, W["norm_ffn1"][i], W["w_ffn1_gu"], W["w_ffn1_down"], i, f"a{i}")
        s["x_mix"] = x
        hm = _rmsnorm(x, W["norm_mix"][i], TM, f"norm_mix{i}")
        s["hm"] = hm
        if i % 2 == 0:
            qkv = _mm_cols(hm, W["w_qkv"], j, F32, TM, f"qkv{i}")
            qs, kn, vb = _qk_norm(qkv, W["q_norm"][j], W["k_norm"][j], TM, f"qk_norm{i}")
            o, tiles = _attn_fwd(qs, kn, vb, f"attn_fwd{i}")
            x = _mm_fwd(o, W["w_o"], j, x, 1.0, TM, f"attn_out{i}")
            s["mix"] = (qkv, qs, kn, vb, o, tiles)
        else:
            u = _mm_plain(hm, W["w_pool_in"], j, F32, TM, f"pool_in{i}")
            x, pooled = _pool_fwd(u, W["w_pool_grp"][j], W["pool_scale"][j], x, TM, f"pool_fwd{i}")
            s["mix"] = (pooled,)
        x, s["ffn2"] = _ffn_forward(x, W["norm_ffn2"][i], W["w_ffn2_gu"], W["w_ffn2_down"], i, f"b{i}")
        s["x_ple"] = x
        x, hp, gp, pe = _ple_fwd(x, W["norm_ple"][i], W["w_ple_gate"], i, p[i], W["w_ple_proj"][i], TM, f"ple_fwd{i}")
        s["ple"] = (hp, gp, pe)
        saved.append(s)

    sq, dx = _loss_head(x, target, TM, "loss_head")
    G = {k: (lax.empty(v.shape, BF) if k in STACKED else [None] * len(v)) for k, v in W.items()}
    for i in reversed(range(depth)):
        j = i // 2
        s = saved[i]
        hp, gp, pe = s["ple"]
        dx, dgp, dpe, G["norm_ple"][i] = _ple_bwd(dx, gp, pe, W["w_ple_gate"], i, s["x_ple"], W["norm_ple"][i], TM, f"ple_bwd{i}")
        G["w_ple_gate"] = _mm_wgrad(hp, dgp, 1.0, G["w_ple_gate"], i, TM, f"ple_dwgate{i}")
        G["w_ple_proj"][i] = _mm_wgrad(p[i], dpe, 1.0, lax.empty((1,) + W["w_ple_proj"][i].shape, BF), 0, TM, f"ple_dwproj{i}")[0]
        dx, G["norm_ffn2"][i], G["w_ffn2_gu"], G["w_ffn2_down"] = _ffn_backward(
            dx, s["ffn2"], W["norm_ffn2"][i], W["w_ffn2_gu"], W["w_ffn2_down"], G["w_ffn2_gu"], G["w_ffn2_down"], i, f"b{i}")
        hm = s["hm"]
        if i % 2 == 0:
            qkv, qs, kn, vb, o, tiles = s["mix"]
            G["w_o"] = _mm_wgrad(o, dx, 1.0, G["w_o"], j, TM, f"attn_dwo{i}")
            do = _mm_plain(dx, W["w_o"], j, BF, TM, f"attn_do{i}", dims=NT)
            dqs, dkn, dv = _attn_bwd(qs, kn, vb, o, do, tiles, f"attn_bwd{i}")
            dq, dk, dvb, dqg, dkg = _qk_norm_bwd(qkv, dqs, dkn, dv, W["q_norm"][j], W["k_norm"][j], TM, f"qk_norm_bwd{i}")
            dqkv = jnp.concatenate([dq, dk, dvb], axis=1)
            G["q_norm"][j] = dqg[:, :HEAD_DIM] + dqg[:, HEAD_DIM:]
            G["k_norm"][j] = dkg[:, :HEAD_DIM] + dkg[:, HEAD_DIM:]
            G["w_qkv"] = _mm_wgrad_cols(hm, dqkv, G["w_qkv"], j, TM, f"attn_dwqkv{i}")
            dx, G["norm_mix"][i] = _mm_dx_norm_cols(dqkv, W["w_qkv"], j, s["x_mix"], W["norm_mix"][i], dx, TM, f"attn_dx{i}")
        else:
            (pooled,) = s["mix"]
            dp, G["w_pool_grp"][j], G["pool_scale"][j] = _pool_bwd_grp(dx, pooled, W["w_pool_grp"][j], W["pool_scale"][j], TM, f"pool_bwd_grp{i}")
            du = _pool_bwd_window(dp, TM, f"pool_bwd_win{i}")
            G["w_pool_in"] = _mm_wgrad(hm, du, 1.0, G["w_pool_in"], j, TM, f"pool_dwin{i}")
            dx, G["norm_mix"][i] = _mm_dx_norm(du, W["w_pool_in"], j, s["x_mix"], W["norm_mix"][i], dx, TM, f"pool_dx{i}")
        dx, G["norm_ffn1"][i], G["w_ffn1_gu"], G["w_ffn1_down"] = _ffn_backward(
            dx, s["ffn1"], W["norm_ffn1"][i], W["w_ffn1_gu"], W["w_ffn1_down"], G["w_ffn1_gu"], G["w_ffn1_down"], i, f"a{i}")
    return sq, dx, G


SHARDED = ("w_ffn1_gu", "w_ffn1_down", "w_qkv", "w_o", "w_pool_in", "w_pool_grp", "w_ffn2_gu", "w_ffn2_down", "w_ple_gate", "w_ple_proj")
COLUMN_SHARDED = ("w_ffn1_gu", "w_qkv", "w_ffn2_gu", "w_ple_proj")
NORMS = ("norm_ffn1", "norm_mix", "norm_ffn2", "norm_ple")
HEAD_GAINS = ("q_norm", "k_norm")
WEIGHTS = ("norm_ffn1", "w_ffn1_gu", "w_ffn1_down", "norm_mix", "w_qkv", "q_norm", "k_norm", "w_o", "w_pool_in", "w_pool_grp",
           "pool_scale", "norm_ffn2", "w_ffn2_gu", "w_ffn2_down", "norm_ple", "w_ple_gate", "w_ple_proj")
SMALL_ROWS = 24


def _whole_weights(w, gathered, pool_scale_all):
    W = {}
    for k in NORMS:
        W[k] = [_row(w[k][i]) for i in range(w[k].shape[0])]
    for k in HEAD_GAINS:
        W[k] = [_row(jnp.tile(w[k][j], LANES // HEAD_DIM)) for j in range(w[k].shape[0])]
    for k in SHARDED:
        g = gathered[k]
        L = g.shape[0]
        if k == "w_pool_grp":
            W[k] = [jnp.transpose(g[i], (1, 0, 2, 3)).reshape(g.shape[2], -1, g.shape[4]) for i in range(L)]
        elif k == "w_ple_proj":
            W[k] = [jnp.transpose(g[i], (1, 0, 2)).reshape(g.shape[2], -1) for i in range(L)]
        elif k in COLUMN_SHARDED:
            W[k] = g
        else:
            W[k] = g.reshape(L, -1, g.shape[3])
    W["pool_scale"] = [_row(pool_scale_all[j]) for j in range(pool_scale_all.shape[0])]
    return W


def _shard_major(k, g):
    if k == "w_pool_grp":
        return jnp.stack([jnp.transpose(t.reshape(t.shape[0], N_CHIPS, t.shape[1] // N_CHIPS, t.shape[2]), (1, 0, 2, 3)) for t in g])
    if k == "w_ple_proj":
        return jnp.stack([jnp.transpose(t.reshape(t.shape[0], N_CHIPS, t.shape[1] // N_CHIPS), (1, 0, 2)) for t in g])
    if k in COLUMN_SHARDED:
        return g
    return g.reshape(g.shape[0], N_CHIPS, g.shape[1] // N_CHIPS, g.shape[2])


def kernel(x, p, norm_ffn1, w_ffn1_gu, w_ffn1_down, norm_mix, w_qkv, q_norm, k_norm, w_o, w_pool_in, w_pool_grp, pool_scale, norm_ffn2, w_ffn2_gu, w_ffn2_down, norm_ple, w_ple_gate, w_ple_proj, loss_target, m_norm_ffn1, m_w_ffn1_gu, m_w_ffn1_down, m_norm_mix, m_w_qkv, m_q_norm, m_k_norm, m_w_o, m_w_pool_in, m_w_pool_grp, m_pool_scale, m_norm_ffn2, m_w_ffn2_gu, m_w_ffn2_down, m_norm_ple, m_w_ple_gate, m_w_ple_proj, v_norm_ffn1, v_w_ffn1_gu, v_w_ffn1_down, v_norm_mix, v_w_qkv, v_q_norm, v_k_norm, v_w_o, v_w_pool_in, v_w_pool_grp, v_pool_scale, v_norm_ffn2, v_w_ffn2_gu, v_w_ffn2_down, v_norm_ple, v_w_ple_gate, v_w_ple_proj):
    w = dict(norm_ffn1=norm_ffn1, w_ffn1_gu=w_ffn1_gu, w_ffn1_down=w_ffn1_down, norm_mix=norm_mix, w_qkv=w_qkv, q_norm=q_norm,
             k_norm=k_norm, w_o=w_o, w_pool_in=w_pool_in, w_pool_grp=w_pool_grp, pool_scale=pool_scale, norm_ffn2=norm_ffn2,
             w_ffn2_gu=w_ffn2_gu, w_ffn2_down=w_ffn2_down, norm_ple=norm_ple, w_ple_gate=w_ple_gate, w_ple_proj=w_ple_proj)
    m = dict(norm_ffn1=m_norm_ffn1, w_ffn1_gu=m_w_ffn1_gu, w_ffn1_down=m_w_ffn1_down, norm_mix=m_norm_mix, w_qkv=m_w_qkv,
             q_norm=m_q_norm, k_norm=m_k_norm, w_o=m_w_o, w_pool_in=m_w_pool_in, w_pool_grp=m_w_pool_grp, pool_scale=m_pool_scale,
             norm_ffn2=m_norm_ffn2, w_ffn2_gu=m_w_ffn2_gu, w_ffn2_down=m_w_ffn2_down, norm_ple=m_norm_ple, w_ple_gate=m_w_ple_gate,
             w_ple_proj=m_w_ple_proj)
    v = dict(norm_ffn1=v_norm_ffn1, w_ffn1_gu=v_w_ffn1_gu, w_ffn1_down=v_w_ffn1_down, norm_mix=v_norm_mix, w_qkv=v_w_qkv,
             q_norm=v_q_norm, k_norm=v_k_norm, w_o=v_w_o, w_pool_in=v_w_pool_in, w_pool_grp=v_w_pool_grp, pool_scale=v_pool_scale,
             norm_ffn2=v_norm_ffn2, w_ffn2_gu=v_w_ffn2_gu, w_ffn2_down=v_w_ffn2_down, norm_ple=v_norm_ple, w_ple_gate=v_w_ple_gate,
             w_ple_proj=v_w_ple_proj)
    chip = 2 * lax.axis_index("x") + lax.axis_index("y")
    D = x.shape[-1]
    shard_cols = pool_scale.shape[1]

    core = lax.axis_index("c")
    shards = [w[k].astype(BF) for k in SHARDED]
    gathered = {k: lax.dynamic_update_slice_in_dim(g, s[:, None], chip, axis=1)
                for k, s, g in zip(SHARDED, shards, _gather_weights(shards, "gather_weights"))}
    scale_rows = jnp.zeros((8, shard_cols), F32).at[:pool_scale.shape[0]].set(pool_scale)
    scale_all = _gather_small(scale_rows, "gather_pool_scale")
    pool_scale_all = jnp.transpose(scale_all[::2, :pool_scale.shape[0]], (1, 0, 2)).reshape(pool_scale.shape[0], D)
    W = _whole_weights(w, gathered, pool_scale_all)

    sq, dx, G = _local_step(x[0], p[:, 0], loss_target[0], W)
    loss = lax.psum(0.5 / D * sq[0, 0], ("x", "y", "c"))

    partials = [_shard_major(k, G[k]) for k in SHARDED]
    theirs = _swap_partials(partials, "swap_partials")
    chip_sums = [_add_pair(lax.dynamic_slice_in_dim(g, core * t.shape[0], t.shape[0], axis=0), t, f"pair_{k}")
                 for k, g, t in zip(SHARDED, partials, theirs)]
    slots = [lax.dynamic_update_slice_in_dim(r, lax.dynamic_index_in_dim(s, chip, axis=1, keepdims=False)[None], chip, axis=0)
             for s, r in zip(chip_sums, _scatter_sums(chip_sums, "scatter_sums"))]
    halves = [_sum_slots(s, f"sum_{k}") for k, s in zip(SHARDED, slots)]
    grads = dict(zip(SHARDED, _join_halves(halves, "join_halves")))

    small = [G[k][i] for k in NORMS for i in range(len(G[k]))]
    small += [jnp.pad(jnp.concatenate(G[k], axis=1), ((0, 0), (0, D - len(G[k]) * HEAD_DIM))) for k in HEAD_GAINS]
    small += G["pool_scale"]
    small = jnp.concatenate(small + [jnp.zeros((SMALL_ROWS - len(small), D), F32)], axis=0)
    small = _sum_slots(_gather_small(small, "gather_small_grads"), "sum_small_grads")
    row = 0
    for k in NORMS:
        grads[k] = small[row:row + w[k].shape[0]]
        row += w[k].shape[0]
    for k in HEAD_GAINS:
        grads[k] = small[row, :w[k].size].reshape(w[k].shape)
        row += 1
    grads["pool_scale"] = lax.dynamic_slice_in_dim(small[row:row + pool_scale.shape[0]], chip * shard_cols, shard_cols, axis=1)

    delta, new_m, new_v = {}, {}, {}
    small_names = NORMS + HEAD_GAINS + ("pool_scale",)

    def pack(d):
        rows = [jnp.pad(d[k].reshape(-1, d[k].shape[-1]) if k in NORMS + ("pool_scale",) else d[k].reshape(1, -1),
                        ((0, 0), (0, D - (d[k].shape[-1] if k in NORMS + ("pool_scale",) else d[k].size))), constant_values=1.0)
                for k in small_names]
        n = sum(r.shape[0] for r in rows)
        return jnp.concatenate(rows + [jnp.ones((SMALL_ROWS - n, D), F32)], axis=0)

    packed = _adamw(pack(w), pack(grads), pack(m), pack(v), "adamw_small")
    row = 0
    for k in small_names:
        n = w[k].shape[0] if k in NORMS + ("pool_scale",) else 1
        width = w[k].shape[-1] if k in NORMS + ("pool_scale",) else w[k].size
        for dst, src in zip((delta, new_m, new_v), packed):
            dst[k] = src[row:row + n, :width].reshape(w[k].shape)
        row += n
    for k in SHARDED:
        delta[k], new_m[k], new_v[k] = _adamw(w[k], grads[k], m[k], v[k], f"adamw_{k}")

    return (loss, dx[None], *[grads[k] for k in WEIGHTS], *[delta[k] for k in WEIGHTS],
            *[new_m[k] for k in WEIGHTS], *[new_v[k] for k in WEIGHTS])
```

```python
import jax
import jax.numpy as jnp
from jax import lax
from jax.experimental import pallas as pl
from jax.experimental.pallas import tpu as pltpu

BF = jnp.bfloat16
F32 = jnp.float32

N_HEADS = 16
HEAD_DIM = 64
POOL_WINDOWS = (2, 4, 8, 16)
POOL_HALO = 16
EPS = 1e-6
ADAM_LR = 0.001
ADAM_B1 = 0.9
ADAM_B2 = 0.999
ADAM_EPS = 1e-08
ADAM_WD = 0.01
ADAM_STEP = 10

N_CHIPS = 4
N_DEV = 8
LANES = 128
VMEM_LIMIT = 56 * 1024 * 1024
ATT_TK = 256
ATT_TQ = 1024
ATT_SLOTS = 3
MESH = pl.DeviceIdType.MESH

NN = (((1,), (0,)), ((), ()))
NT = (((1,), (1,)), ((), ()))
TN = (((0,), (0,)), ((), ()))


def _params(n_axes):
    return pltpu.CompilerParams(dimension_semantics=("arbitrary",) * n_axes, vmem_limit_bytes=VMEM_LIMIT)


def _dot(a, b, dims):
    return lax.dot_general(a, b, dims, preferred_element_type=F32)


def _layer(li, block, index_map):
    return pl.BlockSpec((None,) + tuple(block), lambda *g: (li,) + tuple(index_map(*g)))


def _mm(a, b, extra, out_shapes, *, dims, grid, a_spec, b_spec, extra_specs, out_specs, acc_shape, epilogue, name, into=None):
    nk = grid[-1]
    aliases = {}
    if into is not None:
        aliases = {2 + len(extra): 0}
        extra = [*extra, into]
        extra_specs = [*extra_specs, pl.BlockSpec(memory_space=pl.ANY)]
        out_shapes = [jax.ShapeDtypeStruct(into.shape, into.dtype), *out_shapes[1:]]
    n_extra = len(extra)
    n_out = len(out_shapes)

    def body(a_ref, b_ref, *rest):
        ex = rest[:n_extra]
        outs = rest[n_extra:n_extra + n_out]
        acc = rest[-1]
        k = pl.program_id(len(grid) - 1)

        @pl.when(k == 0)
        def _():
            acc[...] = jnp.zeros_like(acc)

        acc[...] += _dot(a_ref[...].astype(BF), b_ref[...].astype(BF), dims)
        first = pl.program_id(0) == 0

        @pl.when(k == nk - 1)
        def _():
            epilogue(acc[...], ex, outs, first)

    return pl.pallas_call(
        body, grid=grid, in_specs=[a_spec, b_spec, *extra_specs], out_specs=out_specs, out_shape=out_shapes,
        scratch_shapes=[pltpu.VMEM(acc_shape, F32)], compiler_params=_params(len(grid)), name=name,
        input_output_aliases=aliases,
    )(a, b, *extra)


def _store(dtype, scale=1.0):
    def ep(acc, ex, outs, first):
        outs[0][...] = (acc * scale).astype(dtype)
    return ep


def _residual(scale):
    def ep(acc, ex, outs, first):
        outs[0][...] = ex[0][...] + scale * acc
    return ep


def _rms_bwd_epilogue(acc, ex, outs, first):
    x_ref, g_ref, dx_ref = ex
    dxo_ref, dg_ref = outs
    x = x_ref[...]
    r = lax.rsqrt(jnp.mean(x * x, axis=-1, keepdims=True) + EPS)
    xh = x * r
    dxh = acc * g_ref[...]
    dxo_ref[...] = dx_ref[...] + r * (dxh - xh * jnp.mean(dxh * xh, axis=-1, keepdims=True))

    @pl.when(first)
    def _():
        dg_ref[...] = jnp.zeros_like(dg_ref)

    dg_ref[...] += jnp.sum(acc * xh, axis=0, keepdims=True)


def _mm_fwd(a, w, li, x, scale, tm, name):
    S, K = a.shape
    N = w.shape[2]
    return _mm(a, w, [x], [jax.ShapeDtypeStruct((S, N), F32)], dims=NN, grid=(S // tm, 1),
               a_spec=pl.BlockSpec((tm, K), lambda m, k: (m, 0)), b_spec=_layer(li, (K, N), lambda m, k: (0, 0)),
               extra_specs=[pl.BlockSpec((tm, N), lambda m, k: (m, 0))], out_specs=[pl.BlockSpec((tm, N), lambda m, k: (m, 0))],
               acc_shape=(tm, N), epilogue=_residual(scale), name=name)[0]


def _mm_plain(a, w, li, dtype, tm, name, dims=NN):
    S, K = a.shape
    N = w.shape[2] if dims == NN else w.shape[1]
    return _mm(a, w, [], [jax.ShapeDtypeStruct((S, N), dtype)], dims=dims, grid=(S // tm, 1),
               a_spec=pl.BlockSpec((tm, K), lambda m, k: (m, 0)), b_spec=_layer(li, w.shape[1:], lambda m, k: (0, 0)),
               extra_specs=[], out_specs=[pl.BlockSpec((tm, N), lambda m, k: (m, 0))],
               acc_shape=(tm, N), epilogue=_store(dtype), name=name)[0]


def _mm_cols(a, w, li, dtype, tm, name):
    S, K = a.shape
    _, nj, _, ns = w.shape

    def body(a_ref, w_ref, o_ref):
        av = a_ref[...]
        for j in range(nj):
            o_ref[:, j * ns:(j + 1) * ns] = _dot(av, w_ref[j], NN).astype(dtype)

    return pl.pallas_call(
        body, grid=(S // tm,), in_specs=[pl.BlockSpec((tm, K), lambda m: (m, 0)), _layer(li, (nj, K, ns), lambda m: (0, 0, 0))],
        out_specs=pl.BlockSpec((tm, nj * ns), lambda m: (m, 0)), out_shape=jax.ShapeDtypeStruct((S, nj * ns), dtype),
        compiler_params=_params(1), name=name)(a, w)


def _mm_wgrad(a, b, scale, into, li, tk, name, tn=None):
    S, M = a.shape
    N = b.shape[1]
    tn = N if tn is None else tn
    return _mm(a, b, [], [None], dims=TN, grid=(N // tn, S // tk),
               a_spec=pl.BlockSpec((tk, M), lambda n, k: (k, 0)), b_spec=pl.BlockSpec((tk, tn), lambda n, k: (k, n)),
               extra_specs=[], out_specs=[_layer(li, (M, tn), lambda n, k: (0, n))],
               acc_shape=(M, tn), epilogue=_store(BF, scale), name=name, into=into)[0]


def _mm_wgrad_cols(a, b, into, li, tk, name):
    S, M = a.shape
    _, nj, _, ns = into.shape
    return _mm(a, b, [], [None], dims=TN, grid=(nj, S // tk),
               a_spec=pl.BlockSpec((tk, M), lambda j, k: (k, 0)), b_spec=pl.BlockSpec((tk, ns), lambda j, k: (k, j)),
               extra_specs=[], out_specs=[_layer(li, (None, M, ns), lambda j, k: (j, 0, 0))],
               acc_shape=(M, ns), epilogue=_store(BF), name=name, into=into)[0]


def _mm_dx_norm(a, w, li, x, gain, dx, tm, name):
    S, K = a.shape
    N = w.shape[1]
    row = pl.BlockSpec((tm, N), lambda m, k: (m, 0))
    vec = pl.BlockSpec((1, N), lambda m, k: (0, 0))
    return _mm(a, w, [x, gain, dx], [jax.ShapeDtypeStruct((S, N), F32), jax.ShapeDtypeStruct((1, N), F32)], dims=NT,
               grid=(S // tm, 1), a_spec=pl.BlockSpec((tm, K), lambda m, k: (m, 0)), b_spec=_layer(li, (N, K), lambda m, k: (0, 0)),
               extra_specs=[row, vec, row], out_specs=[row, vec], acc_shape=(tm, N), epilogue=_rms_bwd_epilogue, name=name)


def _mm_dx_norm_cols(a, w, li, x, gain, dx, tm, name):
    S = a.shape[0]
    _, nj, N, ks = w.shape

    def body(a_ref, w_ref, x_ref, g_ref, dx_ref, dxo_ref, dg_ref):
        acc = _dot(a_ref[:, :ks], w_ref[0], NT)
        for j in range(1, nj):
            acc = acc + _dot(a_ref[:, j * ks:(j + 1) * ks], w_ref[j], NT)
        _rms_bwd_epilogue(acc, (x_ref, g_ref, dx_ref), (dxo_ref, dg_ref), pl.program_id(0) == 0)

    row = pl.BlockSpec((tm, N), lambda m: (m, 0))
    vec = pl.BlockSpec((1, N), lambda m: (0, 0))
    return pl.pallas_call(
        body, grid=(S // tm,),
        in_specs=[pl.BlockSpec((tm, nj * ks), lambda m: (m, 0)), _layer(li, (nj, N, ks), lambda m: (0, 0, 0)), row, vec, row],
        out_specs=[row, vec], out_shape=[jax.ShapeDtypeStruct((S, N), F32), jax.ShapeDtypeStruct((1, N), F32)],
        compiler_params=_params(1), name=name)(a, w, x, gain, dx)


def _rmsnorm(x, gain, tm, name):
    S, D = x.shape

    def body(x_ref, g_ref, h_ref):
        xv = x_ref[...]
        r = lax.rsqrt(jnp.mean(xv * xv, axis=-1, keepdims=True) + EPS)
        h_ref[...] = (xv * r * g_ref[...]).astype(BF)

    return pl.pallas_call(
        body, grid=(S // tm,), in_specs=[pl.BlockSpec((tm, D), lambda m: (m, 0)), pl.BlockSpec((1, D), lambda m: (0, 0))],
        out_specs=pl.BlockSpec((tm, D), lambda m: (m, 0)), out_shape=jax.ShapeDtypeStruct((S, D), BF),
        compiler_params=_params(1), name=name)(x, gain)


def _ffn_up(h, wgu, li, tm, name):
    S, D = h.shape
    ns = wgu.shape[3]
    half = wgu.shape[1] // 2

    def body(h_ref, wg_ref, wu_ref, g_ref, u_ref, act_ref):
        hv = h_ref[...]
        g = _dot(hv, wg_ref[...], NN)
        u = _dot(hv, wu_ref[...], NN)
        g_ref[...] = g.astype(BF)
        u_ref[...] = u.astype(BF)
        act_ref[...] = (g * jax.nn.sigmoid(g) * u).astype(BF)

    out = jax.ShapeDtypeStruct((S, half * ns), BF)
    tile = pl.BlockSpec((tm, ns), lambda j, m: (m, j))
    return pl.pallas_call(
        body, grid=(half, S // tm),
        in_specs=[pl.BlockSpec((tm, D), lambda j, m: (m, 0)), _layer(li, (None, D, ns), lambda j, m: (j, 0, 0)),
                  _layer(li, (None, D, ns), lambda j, m: (j + half, 0, 0))],
        out_specs=[tile, tile, tile], out_shape=[out, out, out], compiler_params=_params(2), name=name)(h, wgu, wgu)


def _ffn_dact(dx, wdown, li, g, u, tm, name):
    S, D = dx.shape
    F = wdown.shape[1]

    def ep(acc, ex, outs, first):
        gv = ex[0][...].astype(F32)
        uv = ex[1][...].astype(F32)
        da = 0.5 * acc
        sg = jax.nn.sigmoid(gv)
        outs[0][:, :F] = (da * uv * (sg * (1.0 + gv * (1.0 - sg)))).astype(BF)
        outs[0][:, F:] = (da * (gv * sg)).astype(BF)

    row = pl.BlockSpec((tm, F), lambda m, k: (m, 0))
    return _mm(dx, wdown, [g, u], [jax.ShapeDtypeStruct((S, 2 * F), BF)], dims=NT, grid=(S // tm, 1),
               a_spec=pl.BlockSpec((tm, D), lambda m, k: (m, 0)), b_spec=_layer(li, (F, D), lambda m, k: (0, 0)),
               extra_specs=[row, row], out_specs=[pl.BlockSpec((tm, 2 * F), lambda m, k: (m, 0))],
               acc_shape=(tm, F), epilogue=ep, name=name)[0]


def _ple_fwd(x, gain, wgate, li, p, wproj, tm, name):
    S, D = x.shape
    P = p.shape[1]

    def body(x_ref, g_ref, wg_ref, p_ref, wp_ref, xo_ref, hp_ref, gp_ref, pe_ref):
        xv = x_ref[...]
        r = lax.rsqrt(jnp.mean(xv * xv, axis=-1, keepdims=True) + EPS)
        hp = (xv * r * g_ref[...]).astype(BF)
        gp = _dot(hp, wg_ref[...], NN)
        pe = _dot(p_ref[...].astype(BF), wp_ref[...], NN)
        xo_ref[...] = xv + jax.nn.sigmoid(gp) * pe
        hp_ref[...] = hp
        gp_ref[...] = gp.astype(BF)
        pe_ref[...] = pe.astype(BF)

    row = pl.BlockSpec((tm, D), lambda m: (m, 0))
    return pl.pallas_call(
        body, grid=(S // tm,),
        in_specs=[row, pl.BlockSpec((1, D), lambda m: (0, 0)), _layer(li, (D, D), lambda m: (0, 0)),
                  pl.BlockSpec((tm, P), lambda m: (m, 0)), pl.BlockSpec((P, D), lambda m: (0, 0))],
        out_specs=[row, row, row, row],
        out_shape=[jax.ShapeDtypeStruct((S, D), F32)] + [jax.ShapeDtypeStruct((S, D), BF)] * 3,
        compiler_params=_params(1), name=name)(x, gain, wgate, p, wproj)


def _ple_bwd(dx, gp, pe, wgate, li, x, gain, tm, name):
    S, D = dx.shape

    def body(dx_ref, gp_ref, pe_ref, wg_ref, x_ref, g_ref, dxo_ref, dgp_ref, dpe_ref, dg_ref):
        dxv = dx_ref[...]
        sg = jax.nn.sigmoid(gp_ref[...].astype(F32))
        dgp = (dxv * pe_ref[...].astype(F32) * (sg * (1.0 - sg))).astype(BF)
        dgp_ref[...] = dgp
        dpe_ref[...] = (dxv * sg).astype(BF)
        dhp = _dot(dgp, wg_ref[...], NT)
        _rms_bwd_epilogue(dhp, (x_ref, g_ref, dx_ref), (dxo_ref, dg_ref), pl.program_id(0) == 0)

    row = pl.BlockSpec((tm, D), lambda m: (m, 0))
    vec = pl.BlockSpec((1, D), lambda m: (0, 0))
    return pl.pallas_call(
        body, grid=(S // tm,), in_specs=[row, row, row, _layer(li, (D, D), lambda m: (0, 0)), row, vec],
        out_specs=[row, row, row, vec],
        out_shape=[jax.ShapeDtypeStruct((S, D), F32), jax.ShapeDtypeStruct((S, D), BF), jax.ShapeDtypeStruct((S, D), BF),
                   jax.ShapeDtypeStruct((1, D), F32)],
        compiler_params=_params(1), name=name)(dx, gp, pe, wgate, x, gain)


def _loss_head(y, target, tm, name):
    S, D = y.shape

    def body(y_ref, t_ref, sq_ref, dy_ref):
        d = y_ref[...] - t_ref[...]
        dy_ref[...] = d * (1.0 / D)

        @pl.when(pl.program_id(0) == 0)
        def _():
            sq_ref[...] = jnp.zeros_like(sq_ref)

        sq_ref[...] += jnp.sum(d * d, keepdims=True)

    row = pl.BlockSpec((tm, D), lambda m: (m, 0))
    return pl.pallas_call(
        body, grid=(S // tm,), in_specs=[row, row], out_specs=[pl.BlockSpec((1, 1), lambda m: (0, 0)), row],
        out_shape=[jax.ShapeDtypeStruct((1, 1), F32), jax.ShapeDtypeStruct((S, D), F32)],
        compiler_params=_params(1), name=name)(y, target)


def _qk_norm(qkv, qgain, kgain, tm, name):
    S = qkv.shape[0]
    D = qkv.shape[1] // 3
    nb = D // LANES

    def norm2(t, gain):
        lo = lax.broadcasted_iota(jnp.int32, t.shape, 1) < HEAD_DIM
        sq = t * t
        s_lo = jnp.sum(jnp.where(lo, sq, 0.0), axis=1, keepdims=True)
        s_hi = jnp.sum(jnp.where(lo, 0.0, sq), axis=1, keepdims=True)
        r = lax.rsqrt(jnp.where(lo, s_lo, s_hi) * (1.0 / HEAD_DIM) + EPS)
        return t * r * gain

    def body(q_ref, k_ref, v_ref, qg_ref, kg_ref, qo_ref, ko_ref, vo_ref):
        qo_ref[...] = (norm2(q_ref[...], qg_ref[...]) * (HEAD_DIM ** -0.5)).astype(BF)
        ko_ref[...] = norm2(k_ref[...], kg_ref[...]).astype(BF)
        vo_ref[...] = v_ref[...].astype(BF)

    vec = pl.BlockSpec((1, LANES), lambda m, h: (0, 0))
    tile = pl.BlockSpec((tm, LANES), lambda m, h: (m, h))
    out = jax.ShapeDtypeStruct((S, D), BF)
    return pl.pallas_call(
        body, grid=(S // tm, nb),
        in_specs=[tile, pl.BlockSpec((tm, LANES), lambda m, h: (m, nb + h)), pl.BlockSpec((tm, LANES), lambda m, h: (m, 2 * nb + h)), vec, vec],
        out_specs=[tile, tile, tile], out_shape=[out, out, out], compiler_params=_params(2), name=name)(qkv, qkv, qkv, qgain, kgain)


def _qk_norm_bwd(qkv, dqs, dkn, dv, qgain, kgain, tm, name):
    S = qkv.shape[0]
    D = qkv.shape[1] // 3
    nb = D // LANES

    def norm2_bwd(t, gain, dn):
        lo = lax.broadcasted_iota(jnp.int32, t.shape, 1) < HEAD_DIM

        def headsum(val):
            s_lo = jnp.sum(jnp.where(lo, val, 0.0), axis=1, keepdims=True)
            s_hi = jnp.sum(jnp.where(lo, 0.0, val), axis=1, keepdims=True)
            return jnp.where(lo, s_lo, s_hi)

        r = lax.rsqrt(headsum(t * t) * (1.0 / HEAD_DIM) + EPS)
        th = t * r
        dth = dn * gain
        dt = r * (dth - th * (headsum(dth * th) * (1.0 / HEAD_DIM)))
        return dt, jnp.sum(dn * th, axis=0, keepdims=True)

    def body(q_ref, k_ref, dq_ref, dk_ref, dv_ref, qg_ref, kg_ref, dqo_ref, dko_ref, dvo_ref, dqg_ref, dkg_ref):
        dq, dqg = norm2_bwd(q_ref[...], qg_ref[...], dq_ref[...] * (HEAD_DIM ** -0.5))
        dk, dkg = norm2_bwd(k_ref[...], kg_ref[...], dk_ref[...])
        dqo_ref[...] = dq.astype(BF)
        dko_ref[...] = dk.astype(BF)
        dvo_ref[...] = dv_ref[...].astype(BF)

        @pl.when((pl.program_id(0) == 0) & (pl.program_id(1) == 0))
        def _():
            dqg_ref[...] = jnp.zeros_like(dqg_ref)
            dkg_ref[...] = jnp.zeros_like(dkg_ref)

        dqg_ref[...] += dqg
        dkg_ref[...] += dkg

    vec = pl.BlockSpec((1, LANES), lambda m, h: (0, 0))
    tile = pl.BlockSpec((tm, LANES), lambda m, h: (m, h))
    return pl.pallas_call(
        body, grid=(S // tm, nb),
        in_specs=[tile, pl.BlockSpec((tm, LANES), lambda m, h: (m, nb + h)), tile, tile, tile, vec, vec],
        out_specs=[tile, tile, tile, vec, vec],
        out_shape=[jax.ShapeDtypeStruct((S, D), BF)] * 3 + [jax.ShapeDtypeStruct((1, LANES), F32)] * 2,
        compiler_params=_params(2), name=name)(qkv, qkv, dqs, dkn, dv, qgain, kgain)


HEADS_PER_BLOCK = LANES // HEAD_DIM


def _key_order(tk, left):
    j = lax.broadcasted_iota(jnp.int32, (tk, tk), 0)
    s = lax.broadcasted_iota(jnp.int32, (tk, tk), 1)
    return (j < s if left else j > s).astype(BF)


def _attn_tiles(qs, ks, stays, valid, after):
    zs = [_dot(q, k, NT) for q, k in zip(qs, ks)]
    sps = [jnp.maximum(z, 0.0) + jnp.log(1.0 + jnp.exp(-jnp.abs(z))) for z in zs]
    if valid is not None:
        sps = [jnp.where(valid, sp, 0.0) for sp in sps]
    rights = [_dot(sp.astype(BF), after, NN) for sp in sps]
    ws = [jnp.exp((z - sp) - r + stay) for z, sp, r, stay in zip(zs, sps, rights, stays)]
    if valid is not None:
        ws = [jnp.where(valid, w, 0.0) for w in ws]
    return ws, [r[:, :1] + sp[:, :1] for r, sp in zip(rights, sps)]


def _attn_walk(qi, tq, tk, block, carry):
    n_diag = tq // tk
    row = lax.broadcasted_iota(jnp.int32, (tq, tk), 0)
    col = lax.broadcasted_iota(jnp.int32, (tq, tk), 1)
    for n, d in enumerate(reversed(range(n_diag))):
        carry = block(n, n_diag * qi + d, col < row - d * tk, carry)
    return lax.fori_loop(0, n_diag * qi, lambda i, c: block(n_diag + i, n_diag * qi - 1 - i, None, c), carry)


def _tile_slot(qi, kj, tq, tk):
    n_diag = tq // tk
    return n_diag * (qi * (qi + 1) // 2) + kj


def _attn_fwd(qs, kn, vb, name):
    S, D = qs.shape
    TQ, TK = ATT_TQ, ATT_TK
    heads = [slice(hh * HEAD_DIM, (hh + 1) * HEAD_DIM) for hh in range(HEADS_PER_BLOCK)]

    def body(q_ref, k_ref, v_ref, o_ref, tiles_ref, buf, sems):
        hp, qi = pl.program_id(0), pl.program_id(1)
        after = _key_order(TK, left=False)
        q = [q_ref[:, lanes] for lanes in heads]
        n_blocks = (TQ // TK) * (qi + 1)

        def save(slot, kj):
            return pltpu.make_async_copy(buf.at[slot], tiles_ref.at[hp, _tile_slot(qi, kj, TQ, TK)], sems.at[slot])

        def block(n, kj, valid, carry):
            slot = n % ATT_SLOTS
            rows = pl.ds(pl.multiple_of(kj * TK, TK), TK)
            ws, totals = _attn_tiles(q, [k_ref[rows, lanes] for lanes in heads], [c[1] for c in carry], valid, after)
            wbs = [w.astype(BF) for w in ws]
            outs = [_dot(wb, v_ref[rows, lanes], NN) for wb, lanes in zip(wbs, heads)]

            @pl.when(n >= ATT_SLOTS)
            def _():
                save(slot, kj).wait()

            for hh, wb in enumerate(wbs):
                buf[slot, hh] = wb
            save(slot, kj).start()
            return tuple((c[0] + o, c[1] - t) for c, o, t in zip(carry, outs, totals))

        carry = tuple((jnp.zeros((TQ, HEAD_DIM), F32), jnp.zeros((TQ, 1), F32)) for _ in heads)
        carry = _attn_walk(qi, TQ, TK, block, carry)
        for slot in range(ATT_SLOTS):
            @pl.when(slot < n_blocks)
            def _():
                save(slot, 0).wait()
        for hh, lanes in enumerate(heads):
            o_ref[:, lanes] = carry[hh][0].astype(BF)

    tile = pl.BlockSpec((TQ, LANES), lambda h, m: (m, h))
    full = pl.BlockSpec((S, LANES), lambda h, m: (0, h))
    n_tiles = _tile_slot(S // TQ, 0, TQ, TK)
    return pl.pallas_call(
        body, grid=(D // LANES, S // TQ), in_specs=[tile, full, full], out_specs=[tile, ANY],
        out_shape=[jax.ShapeDtypeStruct((S, D), BF), jax.ShapeDtypeStruct((D // LANES, n_tiles, HEADS_PER_BLOCK, TQ, TK), BF)],
        scratch_shapes=[pltpu.VMEM((ATT_SLOTS, HEADS_PER_BLOCK, TQ, TK), BF), pltpu.SemaphoreType.DMA((ATT_SLOTS,))],
        compiler_params=_params(2), name=name)(qs, kn, vb)


def _attn_bwd(qs, kn, vb, do, tiles, name):
    S, D = qs.shape
    TQ, TK = ATT_TQ, ATT_TK
    heads = [slice(hh * HEAD_DIM, (hh + 1) * HEAD_DIM) for hh in range(HEADS_PER_BLOCK)]
    n_diag = TQ // TK

    def body(q_ref, k_ref, v_ref, do_ref, tiles_ref, dq_ref, dk_ref, dv_ref, buf, sems):
        hp, qi = pl.program_id(0), pl.program_id(1)

        @pl.when(qi == 0)
        def _():
            dk_ref[...] = jnp.zeros_like(dk_ref)
            dv_ref[...] = jnp.zeros_like(dv_ref)

        before = _key_order(TK, left=True)
        q = [q_ref[:, lanes] for lanes in heads]
        dout = [do_ref[:, lanes] for lanes in heads]
        n_rest = n_diag * qi
        n_blocks = n_rest + n_diag

        def fetch(slot, kj):
            return pltpu.make_async_copy(tiles_ref.at[hp, _tile_slot(qi, kj, TQ, TK)], buf.at[slot], sems.at[slot])

        for slot in range(ATT_SLOTS):
            @pl.when(slot < n_blocks)
            def _():
                fetch(slot, slot).start()

        def block(kj, valid, carry):
            slot = kj % ATT_SLOTS
            rows = pl.ds(pl.multiple_of(kj * TK, TK), TK)
            ks = [k_ref[rows, lanes] for lanes in heads]
            betas = [jax.nn.sigmoid(_dot(qh, k, NT)) for qh, k in zip(q, ks)]
            das = [_dot(d, v_ref[rows, lanes], NT) for d, lanes in zip(dout, heads)]
            fetch(slot, kj).wait()
            wbs = [buf[slot, hh] for hh in range(HEADS_PER_BLOCK)]
            gs = [wb.astype(F32) * da for wb, da in zip(wbs, das)]
            lefts = [_dot(g.astype(BF), before, NN) for g in gs]
            dzs = [g - beta * (g + (c[1] + left)) for g, beta, c, left in zip(gs, betas, carry, lefts)]
            if valid is not None:
                dzs = [jnp.where(valid, dz, 0.0) for dz in dzs]
            dzbs = [dz.astype(BF) for dz in dzs]
            for hh, lanes in enumerate(heads):
                dk_ref[rows, lanes] += _dot(dzbs[hh], q[hh], TN)
                dv_ref[rows, lanes] += _dot(wbs[hh], dout[hh], TN)
            new = tuple((c[0] + _dot(dzb, k, NN), c[1] + jnp.sum(g, axis=1, keepdims=True))
                        for c, dzb, k, g in zip(carry, dzbs, ks, gs))

            @pl.when(kj + ATT_SLOTS < n_blocks)
            def _():
                fetch(slot, kj + ATT_SLOTS).start()

            return new

        carry = tuple((jnp.zeros((TQ, HEAD_DIM), F32), jnp.zeros((TQ, 1), F32)) for _ in heads)
        carry = lax.fori_loop(0, n_rest, lambda i, c: block(i, None, c), carry)
        row = lax.broadcasted_iota(jnp.int32, (TQ, TK), 0)
        col = lax.broadcasted_iota(jnp.int32, (TQ, TK), 1)
        for d in range(n_diag):
            carry = block(n_rest + d, col < row - d * TK, carry)
        for hh, lanes in enumerate(heads):
            dq_ref[:, lanes] = carry[hh][0]

    tile = pl.BlockSpec((TQ, LANES), lambda h, m: (m, h))
    full = pl.BlockSpec((S, LANES), lambda h, m: (0, h))
    out = jax.ShapeDtypeStruct((S, D), F32)
    return pl.pallas_call(
        body, grid=(D // LANES, S // TQ), in_specs=[tile, full, full, tile, ANY], out_specs=[tile, full, full],
        out_shape=[out, out, out],
        scratch_shapes=[pltpu.VMEM((ATT_SLOTS, HEADS_PER_BLOCK, TQ, TK), BF), pltpu.SemaphoreType.DMA((ATT_SLOTS,))],
        compiler_params=_params(2), name=name)(qs, kn, vb, do, tiles)


def _pool_counts(T, first_row):
    pos = first_row + lax.broadcasted_iota(jnp.int32, (T, 1), 0)
    return [jnp.minimum(pos + 1, w).astype(F32) for w in POOL_WINDOWS]


def _pool_fwd(u, wgrp, scale, x, tm, name):
    S, D = u.shape
    G = len(POOL_WINDOWS)
    C = D // G
    H = POOL_HALO

    def body(u_ref, prev_ref, w_ref, s_ref, x_ref, xo_ref, pooled_ref):
        m = pl.program_id(0)
        prev = jnp.where(m == 0, 0.0, prev_ref[...])
        ext = jnp.concatenate([prev, u_ref[...]], axis=0)
        counts = _pool_counts(tm, m * tm)
        ys = []
        acc = ext
        shift = 1
        for gi, w in enumerate(POOL_WINDOWS):
            while shift < w:
                acc = acc + pltpu.roll(acc, shift, axis=0)
                shift *= 2
            cols = slice(gi * C, (gi + 1) * C)
            pooled = (acc[H:, cols] / counts[gi] - ext[H:, cols]).astype(BF)
            pooled_ref[:, cols] = pooled
            ys.append(_dot(pooled, w_ref[gi], NN))
        xo_ref[...] = x_ref[...] + jnp.concatenate(ys, axis=1) * s_ref[...]

    row = pl.BlockSpec((tm, D), lambda m: (m, 0))
    return pl.pallas_call(
        body, grid=(S // tm,),
        in_specs=[row, pl.BlockSpec((H, D), lambda m: (jnp.maximum(m * (tm // H) - 1, 0), 0)),
                  pl.BlockSpec((G, C, C), lambda m: (0, 0, 0)), pl.BlockSpec((1, D), lambda m: (0, 0)), row],
        out_specs=[row, row], out_shape=[jax.ShapeDtypeStruct((S, D), F32), jax.ShapeDtypeStruct((S, D), BF)],
        compiler_params=_params(1), name=name)(u, u, wgrp, scale, x)


def _pool_bwd_grp(dx, pooled, wgrp, scale, tm, name):
    S, D = dx.shape
    G = len(POOL_WINDOWS)
    C = D // G

    def body(dx_ref, pooled_ref, w_ref, s_ref, dp_ref, dw_ref, ds_ref, dw_acc):
        m = pl.program_id(0)

        @pl.when(m == 0)
        def _():
            dw_acc[...] = jnp.zeros_like(dw_acc)
            ds_ref[...] = jnp.zeros_like(ds_ref)

        dxv = dx_ref[...]
        dy = (dxv * s_ref[...]).astype(BF)
        ys = []
        for gi in range(G):
            cols = slice(gi * C, (gi + 1) * C)
            pg = pooled_ref[:, cols]
            ys.append(_dot(pg, w_ref[gi], NN))
            dw_acc[gi] += _dot(pg, dy[:, cols], TN)
            dp_ref[:, cols] = _dot(dy[:, cols], w_ref[gi], NT)
        ds_ref[...] += jnp.sum(dxv * jnp.concatenate(ys, axis=1), axis=0, keepdims=True)

        @pl.when(m == S // tm - 1)
        def _():
            dw_ref[...] = dw_acc[...].astype(BF)

    row = pl.BlockSpec((tm, D), lambda m: (m, 0))
    wspec = pl.BlockSpec((G, C, C), lambda m: (0, 0, 0))
    vec = pl.BlockSpec((1, D), lambda m: (0, 0))
    return pl.pallas_call(
        body, grid=(S // tm,), in_specs=[row, row, wspec, vec], out_specs=[row, wspec, vec],
        out_shape=[jax.ShapeDtypeStruct((S, D), F32), jax.ShapeDtypeStruct((G, C, C), BF), jax.ShapeDtypeStruct((1, D), F32)],
        scratch_shapes=[pltpu.VMEM((G, C, C), F32)], compiler_params=_params(1), name=name)(dx, pooled, wgrp, scale)


def _pool_bwd_window(dp, tm, name):
    S, D = dp.shape
    G = len(POOL_WINDOWS)
    C = D // G
    H = POOL_HALO
    last = S // tm - 1

    def body(dp_ref, next_ref, du_ref):
        m = pl.program_id(0)
        counts = _pool_counts(tm + H, m * tm)
        nxt = jnp.where(m == last, 0.0, next_ref[...])
        ext = jnp.concatenate([dp_ref[...], nxt], axis=0)
        for gi, w in enumerate(POOL_WINDOWS):
            cols = slice(gi * C, (gi + 1) * C)
            acc = ext[:, cols] / counts[gi]
            shift = 1
            while shift < w:
                acc = acc + pltpu.roll(acc, tm + H - shift, axis=0)
                shift *= 2
            du_ref[:, cols] = (acc[:tm] - ext[:tm, cols]).astype(BF)

    row = pl.BlockSpec((tm, D), lambda m: (m, 0))
    return pl.pallas_call(
        body, grid=(S // tm,),
        in_specs=[row, pl.BlockSpec((H, D), lambda m: (jnp.minimum((m + 1) * (tm // H), S // H - 1), 0))],
        out_specs=row, out_shape=jax.ShapeDtypeStruct((S, D), BF), compiler_params=_params(1), name=name)(dp, dp)


ELEMENTWISE_TILE_BYTES = 1 << 20


def _row_tile(rows, row_bytes):
    for cand in (512, 256, 128, 64, 32, 16, 8):
        if rows % cand == 0 and cand * row_bytes <= ELEMENTWISE_TILE_BYTES:
            return cand
    return rows


def _adamw(w, g, m, v, name):
    shape = w.shape
    cols = shape[-1]
    rows = w.size // cols
    tr = _row_tile(rows, cols * 4)

    def body(w_ref, g_ref, m_ref, v_ref, d_ref, mo_ref, vo_ref):
        gv = g_ref[...]
        mn = ADAM_B1 * m_ref[...] + (1.0 - ADAM_B1) * gv
        vn = ADAM_B2 * v_ref[...] + (1.0 - ADAM_B2) * jnp.square(gv)
        m_hat = mn / (1.0 - ADAM_B1 ** ADAM_STEP)
        v_hat = vn / (1.0 - ADAM_B2 ** ADAM_STEP)
        d_ref[...] = -ADAM_LR * (m_hat / (jnp.sqrt(v_hat) + ADAM_EPS) + ADAM_WD * w_ref[...])
        mo_ref[...] = mn
        vo_ref[...] = vn

    tile = pl.BlockSpec((tr, cols), lambda i: (i, 0))
    out = jax.ShapeDtypeStruct((rows, cols), F32)
    res = pl.pallas_call(
        body, grid=(rows // tr,), in_specs=[tile] * 4, out_specs=[tile] * 3, out_shape=[out] * 3,
        compiler_params=_params(1), name=name)(*[t.reshape(rows, cols) for t in (w, g, m, v)])
    return [t.reshape(shape) for t in res]


def _sum_slots(r, name):
    n = r.shape[0]
    shape = r.shape[1:]
    cols = shape[-1]
    rows = r.size // (n * cols)
    tr = _row_tile(rows, n * cols * r.dtype.itemsize)

    def body(r_ref, o_ref):
        acc = r_ref[0].astype(F32)
        for d in range(1, n):
            acc = acc + r_ref[d].astype(F32)
        o_ref[...] = acc

    return pl.pallas_call(
        body, grid=(rows // tr,), in_specs=[pl.BlockSpec((n, tr, cols), lambda i: (0, i, 0))],
        out_specs=pl.BlockSpec((tr, cols), lambda i: (i, 0)), out_shape=jax.ShapeDtypeStruct((rows, cols), F32),
        compiler_params=_params(1), name=name)(r.reshape(n, rows, cols)).reshape(shape)


def _add_pair(a, b, name):
    shape = a.shape
    cols = shape[-1]
    rows = a.size // cols
    tr = _row_tile(rows, cols * 4)

    def body(a_ref, b_ref, o_ref):
        o_ref[...] = (a_ref[...].astype(F32) + b_ref[...].astype(F32)).astype(BF)

    tile = pl.BlockSpec((tr, cols), lambda i: (i, 0))
    return pl.pallas_call(
        body, grid=(rows // tr,), in_specs=[tile, tile], out_specs=tile, out_shape=jax.ShapeDtypeStruct((rows, cols), BF),
        compiler_params=_params(1), name=name)(a.reshape(rows, cols), b.reshape(rows, cols)).reshape(shape)


ANY = pl.BlockSpec(memory_space=pl.ANY)


def _position():
    return lax.axis_index("x"), lax.axis_index("y"), lax.axis_index("c")


def _gather_small(t, name):
    rows, cols = t.shape

    def body(t_ref, o_ref, send_sems, recv_sems):
        x, y, c = _position()
        me = 4 * x + 2 * y + c
        o_ref[me] = t_ref[...]
        copies = []
        for k in range(1, N_DEV):
            peer = (x ^ (k >> 2), y ^ ((k >> 1) & 1), c ^ (k & 1))
            cp = pltpu.make_async_remote_copy(src_ref=t_ref, dst_ref=o_ref.at[me], send_sem=send_sems.at[k - 1],
                                              recv_sem=recv_sems.at[k - 1], device_id=peer, device_id_type=MESH)
            cp.start()
            copies.append((cp, 4 * peer[0] + 2 * peer[1] + peer[2]))
        for k, (cp, src) in enumerate(copies):
            pltpu.make_async_remote_copy(src_ref=t_ref, dst_ref=o_ref.at[src], send_sem=send_sems.at[k], recv_sem=recv_sems.at[k],
                                         device_id=(x, y, c), device_id_type=MESH).wait_recv()
        for cp, _ in copies:
            cp.wait_send()

    return pl.pallas_call(
        body, in_specs=[pl.BlockSpec(memory_space=pltpu.VMEM)], out_specs=pl.BlockSpec(memory_space=pltpu.VMEM),
        out_shape=jax.ShapeDtypeStruct((N_DEV, rows, cols), F32),
        scratch_shapes=[pltpu.SemaphoreType.DMA((N_DEV - 1,)), pltpu.SemaphoreType.DMA((N_DEV - 1,))], name=name)(t)


def _other_chips(x, y):
    return [(1 - x, y), (x, 1 - y), (1 - x, 1 - y)]


def _remote(src, dst, send_sems, recv_sems, k, to):
    return pltpu.make_async_remote_copy(src_ref=src, dst_ref=dst, send_sem=send_sems.at[k], recv_sem=recv_sems.at[k],
                                        device_id=to, device_id_type=MESH)


def _gather_weights(shards, name):
    n = len(shards)

    def body(*refs):
        ins, outs = refs[:n], refs[n:2 * n]
        ici_send, ici_recv, d2d_send, d2d_recv = refs[2 * n:]
        x, y, c = _position()
        me, sibling, chip = (x, y, c), (x, y, 1 - c), 2 * x + y
        chips = _other_chips(x, y)
        halves = [s.shape[0] // 2 for s in shards]
        sends = []
        for p in range(n):
            mine = pl.ds(c * halves[p], halves[p])
            for j, (px, py) in enumerate(chips):
                sends.append(_remote(ins[p].at[mine], outs[p].at[mine, chip], ici_send, ici_recv, (p, j), (px, py, c)))
                sends[-1].start()
        for p in range(n):
            mine = pl.ds(c * halves[p], halves[p])
            for j, (px, py) in enumerate(chips):
                landed = outs[p].at[mine, 2 * px + py]
                _remote(ins[p].at[mine], landed, ici_send, ici_recv, (p, j), me).wait_recv()
                sends.append(_remote(landed, landed, d2d_send, d2d_recv, (p, j), sibling))
                sends[-1].start()
        for p in range(n):
            theirs = pl.ds((1 - c) * halves[p], halves[p])
            for j, (px, py) in enumerate(chips):
                passed = outs[p].at[theirs, 2 * px + py]
                _remote(passed, passed, d2d_send, d2d_recv, (p, j), me).wait_recv()
        for cp in sends:
            cp.wait_send()

    return pl.pallas_call(
        body, in_specs=[ANY] * n, out_specs=[ANY] * n,
        out_shape=[jax.ShapeDtypeStruct((s.shape[0], N_CHIPS) + s.shape[1:], s.dtype) for s in shards],
        scratch_shapes=[pltpu.SemaphoreType.DMA((n, 3))] * 4, name=name)(*shards)


def _swap_partials(grads, name):
    n = len(grads)

    def body(*refs):
        ins, outs = refs[:n], refs[n:2 * n]
        send_sems, recv_sems = refs[2 * n:]
        x, y, c = _position()
        copies = []
        for p in range(n):
            half = grads[p].shape[0] // 2
            copies.append(_remote(ins[p].at[pl.ds((1 - c) * half, half)], outs[p], send_sems, recv_sems, p, (x, y, 1 - c)))
            copies[-1].start()
        for cp in copies:
            cp.wait()

    return pl.pallas_call(
        body, in_specs=[ANY] * n, out_specs=[ANY] * n,
        out_shape=[jax.ShapeDtypeStruct((g.shape[0] // 2,) + g.shape[1:], g.dtype) for g in grads],
        scratch_shapes=[pltpu.SemaphoreType.DMA((n,))] * 2, name=name)(*grads)


def _scatter_sums(sums, name):
    n = len(sums)

    def body(*refs):
        ins, outs = refs[:n], refs[n:2 * n]
        send_sems, recv_sems = refs[2 * n:]
        x, y, c = _position()
        chip = 2 * x + y
        chips = _other_chips(x, y)
        sends = []
        for p in range(n):
            layers = pl.ds(0, sums[p].shape[0])
            for j, (px, py) in enumerate(chips):
                sends.append(_remote(ins[p].at[layers, 2 * px + py], outs[p].at[chip], send_sems, recv_sems, (p, j), (px, py, c)))
                sends[-1].start()
        for p in range(n):
            layers = pl.ds(0, sums[p].shape[0])
            for j, (px, py) in enumerate(chips):
                _remote(ins[p].at[layers, chip], outs[p].at[2 * px + py], send_sems, recv_sems, (p, j), (x, y, c)).wait_recv()
        for cp in sends:
            cp.wait_send()

    return pl.pallas_call(
        body, in_specs=[ANY] * n, out_specs=[ANY] * n,
        out_shape=[jax.ShapeDtypeStruct((N_CHIPS, s.shape[0]) + s.shape[2:], s.dtype) for s in sums],
        scratch_shapes=[pltpu.SemaphoreType.DMA((n, 3))] * 2, name=name)(*sums)


def _join_halves(halves, name):
    n = len(halves)
    flat = [h.reshape(-1, h.shape[-1]) for h in halves]

    def body(*refs):
        ins, outs = refs[:n], refs[n:2 * n]
        send_sems, recv_sems = refs[2 * n:]
        x, y, c = _position()
        copies = []
        for p in range(n):
            cp = pltpu.make_async_remote_copy(src_ref=ins[p], dst_ref=outs[p], send_sem=send_sems.at[p], recv_sem=recv_sems.at[p],
                                              device_id=(x, y, 1 - c), device_id_type=MESH)
            cp.start()
            copies.append(cp)
        for cp in copies:
            cp.wait()

    theirs = pl.pallas_call(
        body, in_specs=[ANY] * n, out_specs=[ANY] * n, out_shape=[jax.ShapeDtypeStruct(f.shape, f.dtype) for f in flat],
        scratch_shapes=[pltpu.SemaphoreType.DMA((n,)), pltpu.SemaphoreType.DMA((n,))], name=name)(*flat)
    south = lax.axis_index("c") == 0
    return [jnp.concatenate([jnp.where(south, h, t.reshape(h.shape)), jnp.where(south, t.reshape(h.shape), h)], axis=0)
            for h, t in zip(halves, theirs)]


TM = 512


def _row(v):
    return v.reshape(1, -1)


def _ffn_forward(x, gain, wgu, wdown, li, tag):
    h = _rmsnorm(x, gain, TM, f"norm_{tag}")
    g, u, act = _ffn_up(h, wgu, li, TM, f"ffn_up_{tag}")
    xo = _mm_fwd(act, wdown, li, x, 0.5, TM, f"ffn_down_{tag}")
    return xo, (x, h, g, u, act)


def _ffn_backward(dx, saved, gain, wgu, wdown, d_wgu, d_wdown, li, tag):
    x, h, g, u, act = saved
    dgu = _ffn_dact(dx, wdown, li, g, u, 256, f"ffn_dact_{tag}")
    d_wdown = _mm_wgrad(act, dx, 0.5, d_wdown, li, TM, f"ffn_dwdown_{tag}", tn=512)
    d_wgu = _mm_wgrad_cols(h, dgu, d_wgu, li, TM, f"ffn_dwgu_{tag}")
    dx, d_gain = _mm_dx_norm_cols(dgu, wgu, li, x, gain, dx, 256, f"ffn_dx_{tag}")
    return dx, d_gain, d_wgu, d_wdown


STACKED = ("w_ffn1_gu", "w_ffn1_down", "w_qkv", "w_o", "w_pool_in", "w_ffn2_gu", "w_ffn2_down", "w_ple_gate")


def _local_step(x, p, target, W):
    depth = p.shape[0]
    saved = []
    for i in range(depth):
        j = i // 2
        s = {}
        x, s["ffn1"] = _ffn_forward(x, W["norm_ffn1"][i], W["w_ffn1_gu"], W["w_ffn1_down"], i, f"a{i}")
        s["x_mix"] = x
        hm = _rmsnorm(x, W["norm_mix"][i], TM, f"norm_mix{i}")
        s["hm"] = hm
        if i % 2 == 0:
            qkv = _mm_cols(hm, W["w_qkv"], j, F32, TM, f"qkv{i}")
            qs, kn, vb = _qk_norm(qkv, W["q_norm"][j], W["k_norm"][j], TM, f"qk_norm{i}")
            o, tiles = _attn_fwd(qs, kn, vb, f"attn_fwd{i}")
            x = _mm_fwd(o, W["w_o"], j, x, 1.0, TM, f"attn_out{i}")
            s["mix"] = (qkv, qs, kn, vb, o, tiles)
        else:
            u = _mm_plain(hm, W["w_pool_in"], j, F32, TM, f"pool_in{i}")
            x, pooled = _pool_fwd(u, W["w_pool_grp"][j], W["pool_scale"][j], x, TM, f"pool_fwd{i}")
            s["mix"] = (pooled,)
        x, s["ffn2"] = _ffn_forward(x, W["norm_ffn2"][i], W["w_ffn2_gu"], W["w_ffn2_down"], i, f"b{i}")
        s["x_ple"] = x
        x, hp, gp, pe = _ple_fwd(x, W["norm_ple"][i], W["w_ple_gate"], i, p[i], W["w_ple_proj"][i], TM, f"ple_fwd{i}")
        s["ple"] = (hp, gp, pe)
        saved.append(s)

    sq, dx = _loss_head(x, target, TM, "loss_head")
    G = {k: (lax.empty(v.shape, BF) if k in STACKED else [None] * len(v)) for k, v in W.items()}
    for i in reversed(range(depth)):
        j = i // 2
        s = saved[i]
        hp, gp, pe = s["ple"]
        dx, dgp, dpe, G["norm_ple"][i] = _ple_bwd(dx, gp, pe, W["w_ple_gate"], i, s["x_ple"], W["norm_ple"][i], TM, f"ple_bwd{i}")
        G["w_ple_gate"] = _mm_wgrad(hp, dgp, 1.0, G["w_ple_gate"], i, TM, f"ple_dwgate{i}")
        G["w_ple_proj"][i] = _mm_wgrad(p[i], dpe, 1.0, lax.empty((1,) + W["w_ple_proj"][i].shape, BF), 0, TM, f"ple_dwproj{i}")[0]
        dx, G["norm_ffn2"][i], G["w_ffn2_gu"], G["w_ffn2_down"] = _ffn_backward(
            dx, s["ffn2"], W["norm_ffn2"][i], W["w_ffn2_gu"], W["w_ffn2_down"], G["w_ffn2_gu"], G["w_ffn2_down"], i, f"b{i}")
        hm = s["hm"]
        if i % 2 == 0:
            qkv, qs, kn, vb, o, tiles = s["mix"]
            G["w_o"] = _mm_wgrad(o, dx, 1.0, G["w_o"], j, TM, f"attn_dwo{i}")
            do = _mm_plain(dx, W["w_o"], j, BF, TM, f"attn_do{i}", dims=NT)
            dqs, dkn, dv = _attn_bwd(qs, kn, vb, do, tiles, f"attn_bwd{i}")
            dq, dk, dvb, dqg, dkg = _qk_norm_bwd(qkv, dqs, dkn, dv, W["q_norm"][j], W["k_norm"][j], TM, f"qk_norm_bwd{i}")
            dqkv = jnp.concatenate([dq, dk, dvb], axis=1)
            G["q_norm"][j] = dqg[:, :HEAD_DIM] + dqg[:, HEAD_DIM:]
            G["k_norm"][j] = dkg[:, :HEAD_DIM] + dkg[:, HEAD_DIM:]
            G["w_qkv"] = _mm_wgrad_cols(hm, dqkv, G["w_qkv"], j, TM, f"attn_dwqkv{i}")
            dx, G["norm_mix"][i] = _mm_dx_norm_cols(dqkv, W["w_qkv"], j, s["x_mix"], W["norm_mix"][i], dx, TM, f"attn_dx{i}")
        else:
            (pooled,) = s["mix"]
            dp, G["w_pool_grp"][j], G["pool_scale"][j] = _pool_bwd_grp(dx, pooled, W["w_pool_grp"][j], W["pool_scale"][j], TM, f"pool_bwd_grp{i}")
            du = _pool_bwd_window(dp, TM, f"pool_bwd_win{i}")
            G["w_pool_in"] = _mm_wgrad(hm, du, 1.0, G["w_pool_in"], j, TM, f"pool_dwin{i}")
            dx, G["norm_mix"][i] = _mm_dx_norm(du, W["w_pool_in"], j, s["x_mix"], W["norm_mix"][i], dx, TM, f"pool_dx{i}")
        dx, G["norm_ffn1"][i], G["w_ffn1_gu"], G["w_ffn1_down"] = _ffn_backward(
            dx, s["ffn1"], W["norm_ffn1"][i], W["w_ffn1_gu"], W["w_ffn1_down"], G["w_ffn1_gu"], G["w_ffn1_down"], i, f"a{i}")
    return sq, dx, G


SHARDED = ("w_ffn1_gu", "w_ffn1_down", "w_qkv", "w_o", "w_pool_in", "w_pool_grp", "w_ffn2_gu", "w_ffn2_down", "w_ple_gate", "w_ple_proj")
COLUMN_SHARDED = ("w_ffn1_gu", "w_qkv", "w_ffn2_gu", "w_ple_proj")
NORMS = ("norm_ffn1", "norm_mix", "norm_ffn2", "norm_ple")
HEAD_GAINS = ("q_norm", "k_norm")
WEIGHTS = ("norm_ffn1", "w_ffn1_gu", "w_ffn1_down", "norm_mix", "w_qkv", "q_norm", "k_norm", "w_o", "w_pool_in", "w_pool_grp",
           "pool_scale", "norm_ffn2", "w_ffn2_gu", "w_ffn2_down", "norm_ple", "w_ple_gate", "w_ple_proj")
SMALL_ROWS = 24


def _whole_weights(w, gathered, pool_scale_all):
    W = {}
    for k in NORMS:
        W[k] = [_row(w[k][i]) for i in range(w[k].shape[0])]
    for k in HEAD_GAINS:
        W[k] = [_row(jnp.tile(w[k][j], LANES // HEAD_DIM)) for j in range(w[k].shape[0])]
    for k in SHARDED:
        g = gathered[k]
        L = g.shape[0]
        if k == "w_pool_grp":
            W[k] = [jnp.transpose(g[i], (1, 0, 2, 3)).reshape(g.shape[2], -1, g.shape[4]) for i in range(L)]
        elif k == "w_ple_proj":
            W[k] = [jnp.transpose(g[i], (1, 0, 2)).reshape(g.shape[2], -1) for i in range(L)]
        elif k in COLUMN_SHARDED:
            W[k] = g
        else:
            W[k] = g.reshape(L, -1, g.shape[3])
    W["pool_scale"] = [_row(pool_scale_all[j]) for j in range(pool_scale_all.shape[0])]
    return W


def _shard_major(k, g):
    if k == "w_pool_grp":
        return jnp.stack([jnp.transpose(t.reshape(t.shape[0], N_CHIPS, t.shape[1] // N_CHIPS, t.shape[2]), (1, 0, 2, 3)) for t in g])
    if k == "w_ple_proj":
        return jnp.stack([jnp.transpose(t.reshape(t.shape[0], N_CHIPS, t.shape[1] // N_CHIPS), (1, 0, 2)) for t in g])
    if k in COLUMN_SHARDED:
        return g
    return g.reshape(g.shape[0], N_CHIPS, g.shape[1] // N_CHIPS, g.shape[2])


def kernel(x, p, norm_ffn1, w_ffn1_gu, w_ffn1_down, norm_mix, w_qkv, q_norm, k_norm, w_o, w_pool_in, w_pool_grp, pool_scale, norm_ffn2, w_ffn2_gu, w_ffn2_down, norm_ple, w_ple_gate, w_ple_proj, loss_target, m_norm_ffn1, m_w_ffn1_gu, m_w_ffn1_down, m_norm_mix, m_w_qkv, m_q_norm, m_k_norm, m_w_o, m_w_pool_in, m_w_pool_grp, m_pool_scale, m_norm_ffn2, m_w_ffn2_gu, m_w_ffn2_down, m_norm_ple, m_w_ple_gate, m_w_ple_proj, v_norm_ffn1, v_w_ffn1_gu, v_w_ffn1_down, v_norm_mix, v_w_qkv, v_q_norm, v_k_norm, v_w_o, v_w_pool_in, v_w_pool_grp, v_pool_scale, v_norm_ffn2, v_w_ffn2_gu, v_w_ffn2_down, v_norm_ple, v_w_ple_gate, v_w_ple_proj):
    w = dict(norm_ffn1=norm_ffn1, w_ffn1_gu=w_ffn1_gu, w_ffn1_down=w_ffn1_down, norm_mix=norm_mix, w_qkv=w_qkv, q_norm=q_norm,
             k_norm=k_norm, w_o=w_o, w_pool_in=w_pool_in, w_pool_grp=w_pool_grp, pool_scale=pool_scale, norm_ffn2=norm_ffn2,
             w_ffn2_gu=w_ffn2_gu, w_ffn2_down=w_ffn2_down, norm_ple=norm_ple, w_ple_gate=w_ple_gate, w_ple_proj=w_ple_proj)
    m = dict(norm_ffn1=m_norm_ffn1, w_ffn1_gu=m_w_ffn1_gu, w_ffn1_down=m_w_ffn1_down, norm_mix=m_norm_mix, w_qkv=m_w_qkv,
             q_norm=m_q_norm, k_norm=m_k_norm, w_o=m_w_o, w_pool_in=m_w_pool_in, w_pool_grp=m_w_pool_grp, pool_scale=m_pool_scale,
             norm_ffn2=m_norm_ffn2, w_ffn2_gu=m_w_ffn2_gu, w_ffn2_down=m_w_ffn2_down, norm_ple=m_norm_ple, w_ple_gate=m_w_ple_gate,
             w_ple_proj=m_w_ple_proj)
    v = dict(norm_ffn1=v_norm_ffn1, w_ffn1_gu=v_w_ffn1_gu, w_ffn1_down=v_w_ffn1_down, norm_mix=v_norm_mix, w_qkv=v_w_qkv,
             q_norm=v_q_norm, k_norm=v_k_norm, w_o=v_w_o, w_pool_in=v_w_pool_in, w_pool_grp=v_w_pool_grp, pool_scale=v_pool_scale,
             norm_ffn2=v_norm_ffn2, w_ffn2_gu=v_w_ffn2_gu, w_ffn2_down=v_w_ffn2_down, norm_ple=v_norm_ple, w_ple_gate=v_w_ple_gate,
             w_ple_proj=v_w_ple_proj)
    chip = 2 * lax.axis_index("x") + lax.axis_index("y")
    D = x.shape[-1]
    shard_cols = pool_scale.shape[1]

    core = lax.axis_index("c")
    shards = [w[k].astype(BF) for k in SHARDED]
    gathered = {k: lax.dynamic_update_slice_in_dim(g, s[:, None], chip, axis=1)
                for k, s, g in zip(SHARDED, shards, _gather_weights(shards, "gather_weights"))}
    scale_rows = jnp.zeros((8, shard_cols), F32).at[:pool_scale.shape[0]].set(pool_scale)
    scale_all = _gather_small(scale_rows, "gather_pool_scale")
    pool_scale_all = jnp.transpose(scale_all[::2, :pool_scale.shape[0]], (1, 0, 2)).reshape(pool_scale.shape[0], D)
    W = _whole_weights(w, gathered, pool_scale_all)

    sq, dx, G = _local_step(x[0], p[:, 0], loss_target[0], W)
    loss = lax.psum(0.5 / D * sq[0, 0], ("x", "y", "c"))

    partials = [_shard_major(k, G[k]) for k in SHARDED]
    theirs = _swap_partials(partials, "swap_partials")
    chip_sums = [_add_pair(lax.dynamic_slice_in_dim(g, core * t.shape[0], t.shape[0], axis=0), t, f"pair_{k}")
                 for k, g, t in zip(SHARDED, partials, theirs)]
    slots = [lax.dynamic_update_slice_in_dim(r, lax.dynamic_index_in_dim(s, chip, axis=1, keepdims=False)[None], chip, axis=0)
             for s, r in zip(chip_sums, _scatter_sums(chip_sums, "scatter_sums"))]
    halves = [_sum_slots(s, f"sum_{k}") for k, s in zip(SHARDED, slots)]
    grads = dict(zip(SHARDED, _join_halves(halves, "join_halves")))

    small = [G[k][i] for k in NORMS for i in range(len(G[k]))]
    small += [jnp.pad(jnp.concatenate(G[k], axis=1), ((0, 0), (0, D - len(G[k]) * HEAD_DIM))) for k in HEAD_GAINS]
    small += G["pool_scale"]
    small = jnp.concatenate(small + [jnp.zeros((SMALL_ROWS - len(small), D), F32)], axis=0)
    small = _sum_slots(_gather_small(small, "gather_small_grads"), "sum_small_grads")
    row = 0
    for k in NORMS:
        grads[k] = small[row:row + w[k].shape[0]]
        row += w[k].shape[0]
    for k in HEAD_GAINS:
        grads[k] = small[row, :w[k].size].reshape(w[k].shape)
        row += 1
    grads["pool_scale"] = lax.dynamic_slice_in_dim(small[row:row + pool_scale.shape[0]], chip * shard_cols, shard_cols, axis=1)

    delta, new_m, new_v = {}, {}, {}
    small_names = NORMS + HEAD_GAINS + ("pool_scale",)

    def pack(d):
        rows = [jnp.pad(d[k].reshape(-1, d[k].shape[-1]) if k in NORMS + ("pool_scale",) else d[k].reshape(1, -1),
                        ((0, 0), (0, D - (d[k].shape[-1] if k in NORMS + ("pool_scale",) else d[k].size))), constant_values=1.0)
                for k in small_names]
        n = sum(r.shape[0] for r in rows)
        return jnp.concatenate(rows + [jnp.ones((SMALL_ROWS - n, D), F32)], axis=0)

    packed = _adamw(pack(w), pack(grads), pack(m), pack(v), "adamw_small")
    row = 0
    for k in small_names:
        n = w[k].shape[0] if k in NORMS + ("pool_scale",) else 1
        width = w[k].shape[-1] if k in NORMS + ("pool_scale",) else w[k].size
        for dst, src in zip((delta, new_m, new_v), packed):
            dst[k] = src[row:row + n, :width].reshape(w[k].shape)
        row += n
    for k in SHARDED:
        delta[k], new_m[k], new_v[k] = _adamw(w[k], grads[k], m[k], v[k], f"adamw_{k}")

    return (loss, dx[None], *[grads[k] for k in WEIGHTS], *[delta[k] for k in WEIGHTS],
            *[new_m[k] for k in WEIGHTS], *[new_v[k] for k in WEIGHTS])
```

```python
import jax
import jax.numpy as jnp
from jax import lax
from jax.experimental import pallas as pl
from jax.experimental.pallas import tpu as pltpu

BF = jnp.bfloat16
F32 = jnp.float32

N_HEADS = 16
HEAD_DIM = 64
POOL_WINDOWS = (2, 4, 8, 16)
POOL_HALO = 16
EPS = 1e-6
ADAM_LR = 0.001
ADAM_B1 = 0.9
ADAM_B2 = 0.999
ADAM_EPS = 1e-08
ADAM_WD = 0.01
ADAM_STEP = 10

N_CHIPS = 4
N_DEV = 8
LANES = 128
VMEM_LIMIT = 56 * 1024 * 1024
ATT_TK = 256
ATT_TQ = 1024
ATT_SLOTS = 3
MESH = pl.DeviceIdType.MESH

NN = (((1,), (0,)), ((), ()))
NT = (((1,), (1,)), ((), ()))
TN = (((0,), (0,)), ((), ()))


def _params(n_axes):
    return pltpu.CompilerParams(dimension_semantics=("arbitrary",) * n_axes, vmem_limit_bytes=VMEM_LIMIT)


def _dot(a, b, dims):
    return lax.dot_general(a, b, dims, preferred_element_type=F32)


def _layer(li, block, index_map):
    return pl.BlockSpec((None,) + tuple(block), lambda *g: (li,) + tuple(index_map(*g)))


def _mm(a, b, extra, out_shapes, *, dims, grid, a_spec, b_spec, extra_specs, out_specs, acc_shape, epilogue, name, into=None):
    nk = grid[-1]
    aliases = {}
    if into is not None:
        aliases = {2 + len(extra): 0}
        extra = [*extra, into]
        extra_specs = [*extra_specs, pl.BlockSpec(memory_space=pl.ANY)]
        out_shapes = [jax.ShapeDtypeStruct(into.shape, into.dtype), *out_shapes[1:]]
    n_extra = len(extra)
    n_out = len(out_shapes)

    def body(a_ref, b_ref, *rest):
        ex = rest[:n_extra]
        outs = rest[n_extra:n_extra + n_out]
        acc = rest[-1]
        k = pl.program_id(len(grid) - 1)

        @pl.when(k == 0)
        def _():
            acc[...] = jnp.zeros_like(acc)

        acc[...] += _dot(a_ref[...].astype(BF), b_ref[...].astype(BF), dims)
        first = pl.program_id(0) == 0

        @pl.when(k == nk - 1)
        def _():
            epilogue(acc[...], ex, outs, first)

    return pl.pallas_call(
        body, grid=grid, in_specs=[a_spec, b_spec, *extra_specs], out_specs=out_specs, out_shape=out_shapes,
        scratch_shapes=[pltpu.VMEM(acc_shape, F32)], compiler_params=_params(len(grid)), name=name,
        input_output_aliases=aliases,
    )(a, b, *extra)


def _store(dtype, scale=1.0):
    def ep(acc, ex, outs, first):
        outs[0][...] = (acc * scale).astype(dtype)
    return ep


def _residual(scale):
    def ep(acc, ex, outs, first):
        outs[0][...] = ex[0][...] + scale * acc
    return ep


def _rms_bwd_epilogue(acc, ex, outs, first):
    x_ref, g_ref, dx_ref = ex
    dxo_ref, dg_ref = outs
    x = x_ref[...]
    r = lax.rsqrt(jnp.mean(x * x, axis=-1, keepdims=True) + EPS)
    xh = x * r
    dxh = acc * g_ref[...]
    dxo_ref[...] = dx_ref[...] + r * (dxh - xh * jnp.mean(dxh * xh, axis=-1, keepdims=True))

    @pl.when(first)
    def _():
        dg_ref[...] = jnp.zeros_like(dg_ref)

    dg_ref[...] += jnp.sum(acc * xh, axis=0, keepdims=True)


def _mm_fwd(a, w, li, x, scale, tm, name):
    S, K = a.shape
    N = w.shape[2]
    return _mm(a, w, [x], [jax.ShapeDtypeStruct((S, N), F32)], dims=NN, grid=(S // tm, 1),
               a_spec=pl.BlockSpec((tm, K), lambda m, k: (m, 0)), b_spec=_layer(li, (K, N), lambda m, k: (0, 0)),
               extra_specs=[pl.BlockSpec((tm, N), lambda m, k: (m, 0))], out_specs=[pl.BlockSpec((tm, N), lambda m, k: (m, 0))],
               acc_shape=(tm, N), epilogue=_residual(scale), name=name)[0]


def _mm_plain(a, w, li, dtype, tm, name, dims=NN):
    S, K = a.shape
    N = w.shape[2] if dims == NN else w.shape[1]
    return _mm(a, w, [], [jax.ShapeDtypeStruct((S, N), dtype)], dims=dims, grid=(S // tm, 1),
               a_spec=pl.BlockSpec((tm, K), lambda m, k: (m, 0)), b_spec=_layer(li, w.shape[1:], lambda m, k: (0, 0)),
               extra_specs=[], out_specs=[pl.BlockSpec((tm, N), lambda m, k: (m, 0))],
               acc_shape=(tm, N), epilogue=_store(dtype), name=name)[0]


def _mm_cols(a, w, li, dtype, tm, name):
    S, K = a.shape
    _, nj, _, ns = w.shape

    def body(a_ref, w_ref, o_ref):
        av = a_ref[...]
        for j in range(nj):
            o_ref[:, j * ns:(j + 1) * ns] = _dot(av, w_ref[j], NN).astype(dtype)

    return pl.pallas_call(
        body, grid=(S // tm,), in_specs=[pl.BlockSpec((tm, K), lambda m: (m, 0)), _layer(li, (nj, K, ns), lambda m: (0, 0, 0))],
        out_specs=pl.BlockSpec((tm, nj * ns), lambda m: (m, 0)), out_shape=jax.ShapeDtypeStruct((S, nj * ns), dtype),
        compiler_params=_params(1), name=name)(a, w)


def _mm_wgrad(a, b, scale, into, li, tk, name, tn=None):
    S, M = a.shape
    N = b.shape[1]
    tn = N if tn is None else tn
    return _mm(a, b, [], [None], dims=TN, grid=(N // tn, S // tk),
               a_spec=pl.BlockSpec((tk, M), lambda n, k: (k, 0)), b_spec=pl.BlockSpec((tk, tn), lambda n, k: (k, n)),
               extra_specs=[], out_specs=[_layer(li, (M, tn), lambda n, k: (0, n))],
               acc_shape=(M, tn), epilogue=_store(BF, scale), name=name, into=into)[0]


def _mm_wgrad_cols(a, b, into, li, tk, name):
    S, M = a.shape
    _, nj, _, ns = into.shape
    return _mm(a, b, [], [None], dims=TN, grid=(nj, S // tk),
               a_spec=pl.BlockSpec((tk, M), lambda j, k: (k, 0)), b_spec=pl.BlockSpec((tk, ns), lambda j, k: (k, j)),
               extra_specs=[], out_specs=[_layer(li, (None, M, ns), lambda j, k: (j, 0, 0))],
               acc_shape=(M, ns), epilogue=_store(BF), name=name, into=into)[0]


def _mm_dx_norm(a, w, li, x, gain, dx, tm, name):
    S, K = a.shape
    N = w.shape[1]
    row = pl.BlockSpec((tm, N), lambda m, k: (m, 0))
    vec = pl.BlockSpec((1, N), lambda m, k: (0, 0))
    return _mm(a, w, [x, gain, dx], [jax.ShapeDtypeStruct((S, N), F32), jax.ShapeDtypeStruct((1, N), F32)], dims=NT,
               grid=(S // tm, 1), a_spec=pl.BlockSpec((tm, K), lambda m, k: (m, 0)), b_spec=_layer(li, (N, K), lambda m, k: (0, 0)),
               extra_specs=[row, vec, row], out_specs=[row, vec], acc_shape=(tm, N), epilogue=_rms_bwd_epilogue, name=name)


def _mm_dx_norm_cols(a, w, li, x, gain, dx, tm, name):
    S = a.shape[0]
    _, nj, N, ks = w.shape

    def body(a_ref, w_ref, x_ref, g_ref, dx_ref, dxo_ref, dg_ref):
        acc = _dot(a_ref[:, :ks], w_ref[0], NT)
        for j in range(1, nj):
            acc = acc + _dot(a_ref[:, j * ks:(j + 1) * ks], w_ref[j], NT)
        _rms_bwd_epilogue(acc, (x_ref, g_ref, dx_ref), (dxo_ref, dg_ref), pl.program_id(0) == 0)

    row = pl.BlockSpec((tm, N), lambda m: (m, 0))
    vec = pl.BlockSpec((1, N), lambda m: (0, 0))
    return pl.pallas_call(
        body, grid=(S // tm,),
        in_specs=[pl.BlockSpec((tm, nj * ks), lambda m: (m, 0)), _layer(li, (nj, N, ks), lambda m: (0, 0, 0)), row, vec, row],
        out_specs=[row, vec], out_shape=[jax.ShapeDtypeStruct((S, N), F32), jax.ShapeDtypeStruct((1, N), F32)],
        compiler_params=_params(1), name=name)(a, w, x, gain, dx)


def _rmsnorm(x, gain, tm, name):
    S, D = x.shape

    def body(x_ref, g_ref, h_ref):
        xv = x_ref[...]
        r = lax.rsqrt(jnp.mean(xv * xv, axis=-1, keepdims=True) + EPS)
        h_ref[...] = (xv * r * g_ref[...]).astype(BF)

    return pl.pallas_call(
        body, grid=(S // tm,), in_specs=[pl.BlockSpec((tm, D), lambda m: (m, 0)), pl.BlockSpec((1, D), lambda m: (0, 0))],
        out_specs=pl.BlockSpec((tm, D), lambda m: (m, 0)), out_shape=jax.ShapeDtypeStruct((S, D), BF),
        compiler_params=_params(1), name=name)(x, gain)


def _ffn_up(h, wgu, li, tm, name):
    S, D = h.shape
    ns = wgu.shape[3]
    half = wgu.shape[1] // 2

    def body(h_ref, wg_ref, wu_ref, g_ref, u_ref, act_ref):
        hv = h_ref[...]
        g = _dot(hv, wg_ref[...], NN)
        u = _dot(hv, wu_ref[...], NN)
        g_ref[...] = g.astype(BF)
        u_ref[...] = u.astype(BF)
        act_ref[...] = (g * jax.nn.sigmoid(g) * u).astype(BF)

    out = jax.ShapeDtypeStruct((S, half * ns), BF)
    tile = pl.BlockSpec((tm, ns), lambda j, m: (m, j))
    return pl.pallas_call(
        body, grid=(half, S // tm),
        in_specs=[pl.BlockSpec((tm, D), lambda j, m: (m, 0)), _layer(li, (None, D, ns), lambda j, m: (j, 0, 0)),
                  _layer(li, (None, D, ns), lambda j, m: (j + half, 0, 0))],
        out_specs=[tile, tile, tile], out_shape=[out, out, out], compiler_params=_params(2), name=name)(h, wgu, wgu)


def _ffn_dact(dx, wdown, li, g, u, tm, name):
    S, D = dx.shape
    F = wdown.shape[1]

    def ep(acc, ex, outs, first):
        gv = ex[0][...].astype(F32)
        uv = ex[1][...].astype(F32)
        da = 0.5 * acc
        sg = jax.nn.sigmoid(gv)
        outs[0][:, :F] = (da * uv * (sg * (1.0 + gv * (1.0 - sg)))).astype(BF)
        outs[0][:, F:] = (da * (gv * sg)).astype(BF)

    row = pl.BlockSpec((tm, F), lambda m, k: (m, 0))
    return _mm(dx, wdown, [g, u], [jax.ShapeDtypeStruct((S, 2 * F), BF)], dims=NT, grid=(S // tm, 1),
               a_spec=pl.BlockSpec((tm, D), lambda m, k: (m, 0)), b_spec=_layer(li, (F, D), lambda m, k: (0, 0)),
               extra_specs=[row, row], out_specs=[pl.BlockSpec((tm, 2 * F), lambda m, k: (m, 0))],
               acc_shape=(tm, F), epilogue=ep, name=name)[0]


def _ple_fwd(x, gain, wgate, li, p, wproj, tm, name):
    S, D = x.shape
    P = p.shape[1]

    def body(x_ref, g_ref, wg_ref, p_ref, wp_ref, xo_ref, hp_ref, gp_ref, pe_ref):
        xv = x_ref[...]
        r = lax.rsqrt(jnp.mean(xv * xv, axis=-1, keepdims=True) + EPS)
        hp = (xv * r * g_ref[...]).astype(BF)
        gp = _dot(hp, wg_ref[...], NN)
        pe = _dot(p_ref[...].astype(BF), wp_ref[...], NN)
        xo_ref[...] = xv + jax.nn.sigmoid(gp) * pe
        hp_ref[...] = hp
        gp_ref[...] = gp.astype(BF)
        pe_ref[...] = pe.astype(BF)

    row = pl.BlockSpec((tm, D), lambda m: (m, 0))
    return pl.pallas_call(
        body, grid=(S // tm,),
        in_specs=[row, pl.BlockSpec((1, D), lambda m: (0, 0)), _layer(li, (D, D), lambda m: (0, 0)),
                  pl.BlockSpec((tm, P), lambda m: (m, 0)), pl.BlockSpec((P, D), lambda m: (0, 0))],
        out_specs=[row, row, row, row],
        out_shape=[jax.ShapeDtypeStruct((S, D), F32)] + [jax.ShapeDtypeStruct((S, D), BF)] * 3,
        compiler_params=_params(1), name=name)(x, gain, wgate, p, wproj)


def _ple_bwd(dx, gp, pe, wgate, li, x, gain, tm, name):
    S, D = dx.shape

    def body(dx_ref, gp_ref, pe_ref, wg_ref, x_ref, g_ref, dxo_ref, dgp_ref, dpe_ref, dg_ref):
        dxv = dx_ref[...]
        sg = jax.nn.sigmoid(gp_ref[...].astype(F32))
        dgp = (dxv * pe_ref[...].astype(F32) * (sg * (1.0 - sg))).astype(BF)
        dgp_ref[...] = dgp
        dpe_ref[...] = (dxv * sg).astype(BF)
        dhp = _dot(dgp, wg_ref[...], NT)
        _rms_bwd_epilogue(dhp, (x_ref, g_ref, dx_ref), (dxo_ref, dg_ref), pl.program_id(0) == 0)

    row = pl.BlockSpec((tm, D), lambda m: (m, 0))
    vec = pl.BlockSpec((1, D), lambda m: (0, 0))
    return pl.pallas_call(
        body, grid=(S // tm,), in_specs=[row, row, row, _layer(li, (D, D), lambda m: (0, 0)), row, vec],
        out_specs=[row, row, row, vec],
        out_shape=[jax.ShapeDtypeStruct((S, D), F32), jax.ShapeDtypeStruct((S, D), BF), jax.ShapeDtypeStruct((S, D), BF),
                   jax.ShapeDtypeStruct((1, D), F32)],
        compiler_params=_params(1), name=name)(dx, gp, pe, wgate, x, gain)


def _loss_head(y, target, tm, name):
    S, D = y.shape

    def body(y_ref, t_ref, sq_ref, dy_ref):
        d = y_ref[...] - t_ref[...]
        dy_ref[...] = d * (1.0 / D)

        @pl.when(pl.program_id(0) == 0)
        def _():
            sq_ref[...] = jnp.zeros_like(sq_ref)

        sq_ref[...] += jnp.sum(d * d, keepdims=True)

    row = pl.BlockSpec((tm, D), lambda m: (m, 0))
    return pl.pallas_call(
        body, grid=(S // tm,), in_specs=[row, row], out_specs=[pl.BlockSpec((1, 1), lambda m: (0, 0)), row],
        out_shape=[jax.ShapeDtypeStruct((1, 1), F32), jax.ShapeDtypeStruct((S, D), F32)],
        compiler_params=_params(1), name=name)(y, target)


def _qk_norm(qkv, qgain, kgain, tm, name):
    S = qkv.shape[0]
    D = qkv.shape[1] // 3
    nb = D // LANES

    def norm2(t, gain):
        lo = lax.broadcasted_iota(jnp.int32, t.shape, 1) < HEAD_DIM
        sq = t * t
        s_lo = jnp.sum(jnp.where(lo, sq, 0.0), axis=1, keepdims=True)
        s_hi = jnp.sum(jnp.where(lo, 0.0, sq), axis=1, keepdims=True)
        r = lax.rsqrt(jnp.where(lo, s_lo, s_hi) * (1.0 / HEAD_DIM) + EPS)
        return t * r * gain

    def body(q_ref, k_ref, v_ref, qg_ref, kg_ref, qo_ref, ko_ref, vo_ref):
        qo_ref[...] = (norm2(q_ref[...], qg_ref[...]) * (HEAD_DIM ** -0.5)).astype(BF)
        ko_ref[...] = norm2(k_ref[...], kg_ref[...]).astype(BF)
        vo_ref[...] = v_ref[...].astype(BF)

    vec = pl.BlockSpec((1, LANES), lambda m, h: (0, 0))
    tile = pl.BlockSpec((tm, LANES), lambda m, h: (m, h))
    out = jax.ShapeDtypeStruct((S, D), BF)
    return pl.pallas_call(
        body, grid=(S // tm, nb),
        in_specs=[tile, pl.BlockSpec((tm, LANES), lambda m, h: (m, nb + h)), pl.BlockSpec((tm, LANES), lambda m, h: (m, 2 * nb + h)), vec, vec],
        out_specs=[tile, tile, tile], out_shape=[out, out, out], compiler_params=_params(2), name=name)(qkv, qkv, qkv, qgain, kgain)


def _qk_norm_bwd(qkv, dqs, dkn, dv, qgain, kgain, tm, name):
    S = qkv.shape[0]
    D = qkv.shape[1] // 3
    nb = D // LANES

    def norm2_bwd(t, gain, dn):
        lo = lax.broadcasted_iota(jnp.int32, t.shape, 1) < HEAD_DIM

        def headsum(val):
            s_lo = jnp.sum(jnp.where(lo, val, 0.0), axis=1, keepdims=True)
            s_hi = jnp.sum(jnp.where(lo, 0.0, val), axis=1, keepdims=True)
            return jnp.where(lo, s_lo, s_hi)

        r = lax.rsqrt(headsum(t * t) * (1.0 / HEAD_DIM) + EPS)
        th = t * r
        dth = dn * gain
        dt = r * (dth - th * (headsum(dth * th) * (1.0 / HEAD_DIM)))
        return dt, jnp.sum(dn * th, axis=0, keepdims=True)

    def body(q_ref, k_ref, dq_ref, dk_ref, dv_ref, qg_ref, kg_ref, dqo_ref, dko_ref, dvo_ref, dqg_ref, dkg_ref):
        dq, dqg = norm2_bwd(q_ref[...], qg_ref[...], dq_ref[...] * (HEAD_DIM ** -0.5))
        dk, dkg = norm2_bwd(k_ref[...], kg_ref[...], dk_ref[...])
        dqo_ref[...] = dq.astype(BF)
        dko_ref[...] = dk.astype(BF)
        dvo_ref[...] = dv_ref[...].astype(BF)

        @pl.when((pl.program_id(0) == 0) & (pl.program_id(1) == 0))
        def _():
            dqg_ref[...] = jnp.zeros_like(dqg_ref)
            dkg_ref[...] = jnp.zeros_like(dkg_ref)

        dqg_ref[...] += dqg
        dkg_ref[...] += dkg

    vec = pl.BlockSpec((1, LANES), lambda m, h: (0, 0))
    tile = pl.BlockSpec((tm, LANES), lambda m, h: (m, h))
    return pl.pallas_call(
        body, grid=(S // tm, nb),
        in_specs=[tile, pl.BlockSpec((tm, LANES), lambda m, h: (m, nb + h)), tile, tile, tile, vec, vec],
        out_specs=[tile, tile, tile, vec, vec],
        out_shape=[jax.ShapeDtypeStruct((S, D), BF)] * 3 + [jax.ShapeDtypeStruct((1, LANES), F32)] * 2,
        compiler_params=_params(2), name=name)(qkv, qkv, dqs, dkn, dv, qgain, kgain)


HEADS_PER_BLOCK = LANES // HEAD_DIM


def _key_order(tk, left):
    j = lax.broadcasted_iota(jnp.int32, (tk, tk), 0)
    s = lax.broadcasted_iota(jnp.int32, (tk, tk), 1)
    return (j < s if left else j > s).astype(BF)


def _attn_tiles(qs, ks, stays, valid, after):
    zs = [_dot(q, k, NT) for q, k in zip(qs, ks)]
    sps = [jnp.maximum(z, 0.0) + jnp.log(1.0 + jnp.exp(-jnp.abs(z))) for z in zs]
    if valid is not None:
        sps = [jnp.where(valid, sp, 0.0) for sp in sps]
    rights = [_dot(sp.astype(BF), after, NN) for sp in sps]
    ws = [jnp.exp((z - sp) - r + stay) for z, sp, r, stay in zip(zs, sps, rights, stays)]
    if valid is not None:
        ws = [jnp.where(valid, w, 0.0) for w in ws]
    return ws, [r[:, :1] + sp[:, :1] for r, sp in zip(rights, sps)]


def _below(r0, t):
    return t if r0 == 0 else t[r0:]


def _with_below(r0, t, part):
    return part if r0 == 0 else jnp.concatenate([t[:r0], part], axis=0)


def _diagonal_mask(tq, tk, r0):
    return lax.broadcasted_iota(jnp.int32, (tq - r0, tk), 1) < lax.broadcasted_iota(jnp.int32, (tq - r0, tk), 0)


def _attn_walk(qi, tq, tk, block, carry):
    n_diag = tq // tk
    for n, d in enumerate(reversed(range(n_diag))):
        carry = block(n, n_diag * qi + d, d * tk, _diagonal_mask(tq, tk, d * tk), carry)
    return lax.fori_loop(0, n_diag * qi, lambda i, c: block(n_diag + i, n_diag * qi - 1 - i, 0, None, c), carry)


def _tile_slot(qi, kj, tq, tk):
    n_diag = tq // tk
    return n_diag * (qi * (qi + 1) // 2) + kj


def _attn_fwd(qs, kn, vb, name):
    S, D = qs.shape
    TQ, TK = ATT_TQ, ATT_TK
    heads = [slice(hh * HEAD_DIM, (hh + 1) * HEAD_DIM) for hh in range(HEADS_PER_BLOCK)]

    def body(q_ref, k_ref, v_ref, o_ref, tiles_ref, buf, sems):
        hp, qi = pl.program_id(0), pl.program_id(1)
        after = _key_order(TK, left=False)
        q = [q_ref[:, lanes] for lanes in heads]
        n_blocks = (TQ // TK) * (qi + 1)

        def save(slot, kj):
            return pltpu.make_async_copy(buf.at[slot], tiles_ref.at[hp, _tile_slot(qi, kj, TQ, TK)], sems.at[slot])

        def block(n, kj, r0, valid, carry):
            slot = n % ATT_SLOTS
            rows = pl.ds(pl.multiple_of(kj * TK, TK), TK)
            stays = [_below(r0, c[1]) for c in carry]
            ws, totals = _attn_tiles([_below(r0, qh) for qh in q], [k_ref[rows, lanes] for lanes in heads], stays, valid, after)
            wbs = [w.astype(BF) for w in ws]
            outs = [_dot(wb, v_ref[rows, lanes], NN) for wb, lanes in zip(wbs, heads)]

            @pl.when(n >= ATT_SLOTS)
            def _():
                save(slot, kj).wait()

            for hh, wb in enumerate(wbs):
                if r0:
                    buf[slot, hh, :r0] = jnp.zeros((r0, TK), BF)
                buf[slot, hh, r0:] = wb
            save(slot, kj).start()
            return tuple((_with_below(r0, c[0], _below(r0, c[0]) + o), _with_below(r0, c[1], stay - t))
                         for c, o, stay, t in zip(carry, outs, stays, totals))

        carry = tuple((jnp.zeros((TQ, HEAD_DIM), F32), jnp.zeros((TQ, 1), F32)) for _ in heads)
        carry = _attn_walk(qi, TQ, TK, block, carry)
        for slot in range(ATT_SLOTS):
            @pl.when(slot < n_blocks)
            def _():
                save(slot, 0).wait()
        for hh, lanes in enumerate(heads):
            o_ref[:, lanes] = carry[hh][0].astype(BF)

    tile = pl.BlockSpec((TQ, LANES), lambda h, m: (m, h))
    full = pl.BlockSpec((S, LANES), lambda h, m: (0, h))
    n_tiles = _tile_slot(S // TQ, 0, TQ, TK)
    return pl.pallas_call(
        body, grid=(D // LANES, S // TQ), in_specs=[tile, full, full], out_specs=[tile, ANY],
        out_shape=[jax.ShapeDtypeStruct((S, D), BF), jax.ShapeDtypeStruct((D // LANES, n_tiles, HEADS_PER_BLOCK, TQ, TK), BF)],
        scratch_shapes=[pltpu.VMEM((ATT_SLOTS, HEADS_PER_BLOCK, TQ, TK), BF), pltpu.SemaphoreType.DMA((ATT_SLOTS,))],
        compiler_params=_params(2), name=name)(qs, kn, vb)


def _attn_bwd(qs, kn, vb, do, tiles, name):
    S, D = qs.shape
    TQ, TK = ATT_TQ, ATT_TK
    heads = [slice(hh * HEAD_DIM, (hh + 1) * HEAD_DIM) for hh in range(HEADS_PER_BLOCK)]
    n_diag = TQ // TK

    def body(q_ref, k_ref, v_ref, do_ref, tiles_ref, dq_ref, dk_ref, dv_ref, buf, sems):
        hp, qi = pl.program_id(0), pl.program_id(1)

        @pl.when(qi == 0)
        def _():
            dk_ref[...] = jnp.zeros_like(dk_ref)
            dv_ref[...] = jnp.zeros_like(dv_ref)

        before = _key_order(TK, left=True)
        q = [q_ref[:, lanes] for lanes in heads]
        dout = [do_ref[:, lanes] for lanes in heads]
        n_rest = n_diag * qi
        n_blocks = n_rest + n_diag

        def fetch(slot, kj):
            return pltpu.make_async_copy(tiles_ref.at[hp, _tile_slot(qi, kj, TQ, TK)], buf.at[slot], sems.at[slot])

        for slot in range(ATT_SLOTS):
            @pl.when(slot < n_blocks)
            def _():
                fetch(slot, slot).start()

        def block(kj, r0, valid, carry):
            slot = kj % ATT_SLOTS
            rows = pl.ds(pl.multiple_of(kj * TK, TK), TK)
            ks = [k_ref[rows, lanes] for lanes in heads]
            qs_, douts = [_below(r0, qh) for qh in q], [_below(r0, d) for d in dout]
            betas = [jax.nn.sigmoid(_dot(qh, k, NT)) for qh, k in zip(qs_, ks)]
            das = [_dot(d, v_ref[rows, lanes], NT) for d, lanes in zip(douts, heads)]
            fetch(slot, kj).wait()
            wbs = [buf[slot, hh, r0:] for hh in range(HEADS_PER_BLOCK)]
            gs = [wb.astype(F32) * da for wb, da in zip(wbs, das)]
            lefts = [_dot(g.astype(BF), before, NN) for g in gs]
            dzs = [g - beta * (g + (_below(r0, c[1]) + left)) for g, beta, c, left in zip(gs, betas, carry, lefts)]
            if valid is not None:
                dzs = [jnp.where(valid, dz, 0.0) for dz in dzs]
            dzbs = [dz.astype(BF) for dz in dzs]
            for hh, lanes in enumerate(heads):
                dk_ref[rows, lanes] += _dot(dzbs[hh], qs_[hh], TN)
                dv_ref[rows, lanes] += _dot(wbs[hh], douts[hh], TN)
            new = tuple((_with_below(r0, c[0], _below(r0, c[0]) + _dot(dzb, k, NN)),
                         _with_below(r0, c[1], _below(r0, c[1]) + jnp.sum(g, axis=1, keepdims=True)))
                        for c, dzb, k, g in zip(carry, dzbs, ks, gs))

            @pl.when(kj + ATT_SLOTS < n_blocks)
            def _():
                fetch(slot, kj + ATT_SLOTS).start()

            return new

        carry = tuple((jnp.zeros((TQ, HEAD_DIM), F32), jnp.zeros((TQ, 1), F32)) for _ in heads)
        carry = lax.fori_loop(0, n_rest, lambda i, c: block(i, 0, None, c), carry)
        for d in range(n_diag):
            carry = block(n_rest + d, d * TK, _diagonal_mask(TQ, TK, d * TK), carry)
        for hh, lanes in enumerate(heads):
            dq_ref[:, lanes] = carry[hh][0]

    tile = pl.BlockSpec((TQ, LANES), lambda h, m: (m, h))
    full = pl.BlockSpec((S, LANES), lambda h, m: (0, h))
    out = jax.ShapeDtypeStruct((S, D), F32)
    return pl.pallas_call(
        body, grid=(D // LANES, S // TQ), in_specs=[tile, full, full, tile, ANY], out_specs=[tile, full, full],
        out_shape=[out, out, out],
        scratch_shapes=[pltpu.VMEM((ATT_SLOTS, HEADS_PER_BLOCK, TQ, TK), BF), pltpu.SemaphoreType.DMA((ATT_SLOTS,))],
        compiler_params=_params(2), name=name)(qs, kn, vb, do, tiles)


def _pool_counts(T, first_row):
    pos = first_row + lax.broadcasted_iota(jnp.int32, (T, 1), 0)
    return [jnp.minimum(pos + 1, w).astype(F32) for w in POOL_WINDOWS]


def _pool_fwd(u, wgrp, scale, x, tm, name):
    S, D = u.shape
    G = len(POOL_WINDOWS)
    C = D // G
    H = POOL_HALO

    def body(u_ref, prev_ref, w_ref, s_ref, x_ref, xo_ref, pooled_ref):
        m = pl.program_id(0)
        prev = jnp.where(m == 0, 0.0, prev_ref[...])
        ext = jnp.concatenate([prev, u_ref[...]], axis=0)
        counts = _pool_counts(tm, m * tm)
        ys = []
        acc = ext
        shift = 1
        for gi, w in enumerate(POOL_WINDOWS):
            while shift < w:
                acc = acc + pltpu.roll(acc, shift, axis=0)
                shift *= 2
            cols = slice(gi * C, (gi + 1) * C)
            pooled = (acc[H:, cols] / counts[gi] - ext[H:, cols]).astype(BF)
            pooled_ref[:, cols] = pooled
            ys.append(_dot(pooled, w_ref[gi], NN))
        xo_ref[...] = x_ref[...] + jnp.concatenate(ys, axis=1) * s_ref[...]

    row = pl.BlockSpec((tm, D), lambda m: (m, 0))
    return pl.pallas_call(
        body, grid=(S // tm,),
        in_specs=[row, pl.BlockSpec((H, D), lambda m: (jnp.maximum(m * (tm // H) - 1, 0), 0)),
                  pl.BlockSpec((G, C, C), lambda m: (0, 0, 0)), pl.BlockSpec((1, D), lambda m: (0, 0)), row],
        out_specs=[row, row], out_shape=[jax.ShapeDtypeStruct((S, D), F32), jax.ShapeDtypeStruct((S, D), BF)],
        compiler_params=_params(1), name=name)(u, u, wgrp, scale, x)


def _pool_bwd_grp(dx, pooled, wgrp, scale, tm, name):
    S, D = dx.shape
    G = len(POOL_WINDOWS)
    C = D // G

    def body(dx_ref, pooled_ref, w_ref, s_ref, dp_ref, dw_ref, ds_ref, dw_acc):
        m = pl.program_id(0)

        @pl.when(m == 0)
        def _():
            dw_acc[...] = jnp.zeros_like(dw_acc)
            ds_ref[...] = jnp.zeros_like(ds_ref)

        dxv = dx_ref[...]
        dy = (dxv * s_ref[...]).astype(BF)
        ys = []
        for gi in range(G):
            cols = slice(gi * C, (gi + 1) * C)
            pg = pooled_ref[:, cols]
            ys.append(_dot(pg, w_ref[gi], NN))
            dw_acc[gi] += _dot(pg, dy[:, cols], TN)
            dp_ref[:, cols] = _dot(dy[:, cols], w_ref[gi], NT)
        ds_ref[...] += jnp.sum(dxv * jnp.concatenate(ys, axis=1), axis=0, keepdims=True)

        @pl.when(m == S // tm - 1)
        def _():
            dw_ref[...] = dw_acc[...].astype(BF)

    row = pl.BlockSpec((tm, D), lambda m: (m, 0))
    wspec = pl.BlockSpec((G, C, C), lambda m: (0, 0, 0))
    vec = pl.BlockSpec((1, D), lambda m: (0, 0))
    return pl.pallas_call(
        body, grid=(S // tm,), in_specs=[row, row, wspec, vec], out_specs=[row, wspec, vec],
        out_shape=[jax.ShapeDtypeStruct((S, D), F32), jax.ShapeDtypeStruct((G, C, C), BF), jax.ShapeDtypeStruct((1, D), F32)],
        scratch_shapes=[pltpu.VMEM((G, C, C), F32)], compiler_params=_params(1), name=name)(dx, pooled, wgrp, scale)


def _pool_bwd_window(dp, tm, name):
    S, D = dp.shape
    G = len(POOL_WINDOWS)
    C = D // G
    H = POOL_HALO
    last = S // tm - 1

    def body(dp_ref, next_ref, du_ref):
        m = pl.program_id(0)
        counts = _pool_counts(tm + H, m * tm)
        nxt = jnp.where(m == last, 0.0, next_ref[...])
        ext = jnp.concatenate([dp_ref[...], nxt], axis=0)
        for gi, w in enumerate(POOL_WINDOWS):
            cols = slice(gi * C, (gi + 1) * C)
            acc = ext[:, cols] / counts[gi]
            shift = 1
            while shift < w:
                acc = acc + pltpu.roll(acc, tm + H - shift, axis=0)
                shift *= 2
            du_ref[:, cols] = (acc[:tm] - ext[:tm, cols]).astype(BF)

    row = pl.BlockSpec((tm, D), lambda m: (m, 0))
    return pl.pallas_call(
        body, grid=(S // tm,),
        in_specs=[row, pl.BlockSpec((H, D), lambda m: (jnp.minimum((m + 1) * (tm // H), S // H - 1), 0))],
        out_specs=row, out_shape=jax.ShapeDtypeStruct((S, D), BF), compiler_params=_params(1), name=name)(dp, dp)


ELEMENTWISE_TILE_BYTES = 1 << 20


def _row_tile(rows, row_bytes):
    for cand in (512, 256, 128, 64, 32, 16, 8):
        if rows % cand == 0 and cand * row_bytes <= ELEMENTWISE_TILE_BYTES:
            return cand
    return rows


def _adamw(w, g, m, v, name):
    shape = w.shape
    cols = shape[-1]
    rows = w.size // cols
    tr = _row_tile(rows, cols * 4)

    def body(w_ref, g_ref, m_ref, v_ref, d_ref, mo_ref, vo_ref):
        gv = g_ref[...]
        mn = ADAM_B1 * m_ref[...] + (1.0 - ADAM_B1) * gv
        vn = ADAM_B2 * v_ref[...] + (1.0 - ADAM_B2) * jnp.square(gv)
        m_hat = mn / (1.0 - ADAM_B1 ** ADAM_STEP)
        v_hat = vn / (1.0 - ADAM_B2 ** ADAM_STEP)
        d_ref[...] = -ADAM_LR * (m_hat / (jnp.sqrt(v_hat) + ADAM_EPS) + ADAM_WD * w_ref[...])
        mo_ref[...] = mn
        vo_ref[...] = vn

    tile = pl.BlockSpec((tr, cols), lambda i: (i, 0))
    out = jax.ShapeDtypeStruct((rows, cols), F32)
    res = pl.pallas_call(
        body, grid=(rows // tr,), in_specs=[tile] * 4, out_specs=[tile] * 3, out_shape=[out] * 3,
        compiler_params=_params(1), name=name)(*[t.reshape(rows, cols) for t in (w, g, m, v)])
    return [t.reshape(shape) for t in res]


def _sum_slots(r, name):
    n = r.shape[0]
    shape = r.shape[1:]
    cols = shape[-1]
    rows = r.size // (n * cols)
    tr = _row_tile(rows, n * cols * r.dtype.itemsize)

    def body(r_ref, o_ref):
        acc = r_ref[0].astype(F32)
        for d in range(1, n):
            acc = acc + r_ref[d].astype(F32)
        o_ref[...] = acc

    return pl.pallas_call(
        body, grid=(rows // tr,), in_specs=[pl.BlockSpec((n, tr, cols), lambda i: (0, i, 0))],
        out_specs=pl.BlockSpec((tr, cols), lambda i: (i, 0)), out_shape=jax.ShapeDtypeStruct((rows, cols), F32),
        compiler_params=_params(1), name=name)(r.reshape(n, rows, cols)).reshape(shape)


def _add_pair(a, b, name):
    shape = a.shape
    cols = shape[-1]
    rows = a.size // cols
    tr = _row_tile(rows, cols * 4)

    def body(a_ref, b_ref, o_ref):
        o_ref[...] = (a_ref[...].astype(F32) + b_ref[...].astype(F32)).astype(BF)

    tile = pl.BlockSpec((tr, cols), lambda i: (i, 0))
    return pl.pallas_call(
        body, grid=(rows // tr,), in_specs=[tile, tile], out_specs=tile, out_shape=jax.ShapeDtypeStruct((rows, cols), BF),
        compiler_params=_params(1), name=name)(a.reshape(rows, cols), b.reshape(rows, cols)).reshape(shape)


ANY = pl.BlockSpec(memory_space=pl.ANY)


def _position():
    return lax.axis_index("x"), lax.axis_index("y"), lax.axis_index("c")


def _gather_small(t, name):
    rows, cols = t.shape

    def body(t_ref, o_ref, send_sems, recv_sems):
        x, y, c = _position()
        me = 4 * x + 2 * y + c
        o_ref[me] = t_ref[...]
        copies = []
        for k in range(1, N_DEV):
            peer = (x ^ (k >> 2), y ^ ((k >> 1) & 1), c ^ (k & 1))
            cp = pltpu.make_async_remote_copy(src_ref=t_ref, dst_ref=o_ref.at[me], send_sem=send_sems.at[k - 1],
                                              recv_sem=recv_sems.at[k - 1], device_id=peer, device_id_type=MESH)
            cp.start()
            copies.append((cp, 4 * peer[0] + 2 * peer[1] + peer[2]))
        for k, (cp, src) in enumerate(copies):
            pltpu.make_async_remote_copy(src_ref=t_ref, dst_ref=o_ref.at[src], send_sem=send_sems.at[k], recv_sem=recv_sems.at[k],
                                         device_id=(x, y, c), device_id_type=MESH).wait_recv()
        for cp, _ in copies:
            cp.wait_send()

    return pl.pallas_call(
        body, in_specs=[pl.BlockSpec(memory_space=pltpu.VMEM)], out_specs=pl.BlockSpec(memory_space=pltpu.VMEM),
        out_shape=jax.ShapeDtypeStruct((N_DEV, rows, cols), F32),
        scratch_shapes=[pltpu.SemaphoreType.DMA((N_DEV - 1,)), pltpu.SemaphoreType.DMA((N_DEV - 1,))], name=name)(t)


def _other_chips(x, y):
    return [(1 - x, y), (x, 1 - y), (1 - x, 1 - y)]


def _remote(src, dst, send_sems, recv_sems, k, to):
    return pltpu.make_async_remote_copy(src_ref=src, dst_ref=dst, send_sem=send_sems.at[k], recv_sem=recv_sems.at[k],
                                        device_id=to, device_id_type=MESH)


def _gather_weights(shards, name):
    n = len(shards)

    def body(*refs):
        ins, outs = refs[:n], refs[n:2 * n]
        ici_send, ici_recv, d2d_send, d2d_recv = refs[2 * n:]
        x, y, c = _position()
        me, sibling, chip = (x, y, c), (x, y, 1 - c), 2 * x + y
        chips = _other_chips(x, y)
        halves = [s.shape[0] // 2 for s in shards]
        sends = []
        for p in range(n):
            mine = pl.ds(c * halves[p], halves[p])
            for j, (px, py) in enumerate(chips):
                sends.append(_remote(ins[p].at[mine], outs[p].at[mine, chip], ici_send, ici_recv, (p, j), (px, py, c)))
                sends[-1].start()
        for p in range(n):
            mine = pl.ds(c * halves[p], halves[p])
            for j, (px, py) in enumerate(chips):
                landed = outs[p].at[mine, 2 * px + py]
                _remote(ins[p].at[mine], landed, ici_send, ici_recv, (p, j), me).wait_recv()
                sends.append(_remote(landed, landed, d2d_send, d2d_recv, (p, j), sibling))
                sends[-1].start()
        for p in range(n):
            theirs = pl.ds((1 - c) * halves[p], halves[p])
            for j, (px, py) in enumerate(chips):
                passed = outs[p].at[theirs, 2 * px + py]
                _remote(passed, passed, d2d_send, d2d_recv, (p, j), me).wait_recv()
        for cp in sends:
            cp.wait_send()

    return pl.pallas_call(
        body, in_specs=[ANY] * n, out_specs=[ANY] * n,
        out_shape=[jax.ShapeDtypeStruct((s.shape[0], N_CHIPS) + s.shape[1:], s.dtype) for s in shards],
        scratch_shapes=[pltpu.SemaphoreType.DMA((n, 3))] * 4, name=name)(*shards)


def _swap_partials(grads, name):
    n = len(grads)

    def body(*refs):
        ins, outs = refs[:n], refs[n:2 * n]
        send_sems, recv_sems = refs[2 * n:]
        x, y, c = _position()
        copies = []
        for p in range(n):
            half = grads[p].shape[0] // 2
            copies.append(_remote(ins[p].at[pl.ds((1 - c) * half, half)], outs[p], send_sems, recv_sems, p, (x, y, 1 - c)))
            copies[-1].start()
        for cp in copies:
            cp.wait()

    return pl.pallas_call(
        body, in_specs=[ANY] * n, out_specs=[ANY] * n,
        out_shape=[jax.ShapeDtypeStruct((g.shape[0] // 2,) + g.shape[1:], g.dtype) for g in grads],
        scratch_shapes=[pltpu.SemaphoreType.DMA((n,))] * 2, name=name)(*grads)


def _scatter_sums(sums, name):
    n = len(sums)

    def body(*refs):
        ins, outs = refs[:n], refs[n:2 * n]
        send_sems, recv_sems = refs[2 * n:]
        x, y, c = _position()
        chip = 2 * x + y
        chips = _other_chips(x, y)
        sends = []
        for p in range(n):
            layers = pl.ds(0, sums[p].shape[0])
            for j, (px, py) in enumerate(chips):
                sends.append(_remote(ins[p].at[layers, 2 * px + py], outs[p].at[chip], send_sems, recv_sems, (p, j), (px, py, c)))
                sends[-1].start()
        for p in range(n):
            layers = pl.ds(0, sums[p].shape[0])
            for j, (px, py) in enumerate(chips):
                _remote(ins[p].at[layers, chip], outs[p].at[2 * px + py], send_sems, recv_sems, (p, j), (x, y, c)).wait_recv()
        for cp in sends:
            cp.wait_send()

    return pl.pallas_call(
        body, in_specs=[ANY] * n, out_specs=[ANY] * n,
        out_shape=[jax.ShapeDtypeStruct((N_CHIPS, s.shape[0]) + s.shape[2:], s.dtype) for s in sums],
        scratch_shapes=[pltpu.SemaphoreType.DMA((n, 3))] * 2, name=name)(*sums)


def _join_halves(halves, name):
    n = len(halves)
    flat = [h.reshape(-1, h.shape[-1]) for h in halves]

    def body(*refs):
        ins, outs = refs[:n], refs[n:2 * n]
        send_sems, recv_sems = refs[2 * n:]
        x, y, c = _position()
        copies = []
        for p in range(n):
            cp = pltpu.make_async_remote_copy(src_ref=ins[p], dst_ref=outs[p], send_sem=send_sems.at[p], recv_sem=recv_sems.at[p],
                                              device_id=(x, y, 1 - c), device_id_type=MESH)
            cp.start()
            copies.append(cp)
        for cp in copies:
            cp.wait()

    theirs = pl.pallas_call(
        body, in_specs=[ANY] * n, out_specs=[ANY] * n, out_shape=[jax.ShapeDtypeStruct(f.shape, f.dtype) for f in flat],
        scratch_shapes=[pltpu.SemaphoreType.DMA((n,)), pltpu.SemaphoreType.DMA((n,))], name=name)(*flat)
    south = lax.axis_index("c") == 0
    return [jnp.concatenate([jnp.where(south, h, t.reshape(h.shape)), jnp.where(south, t.reshape(h.shape), h)], axis=0)
            for h, t in zip(halves, theirs)]


TM = 512


def _row(v):
    return v.reshape(1, -1)


def _ffn_forward(x, gain, wgu, wdown, li, tag):
    h = _rmsnorm(x, gain, TM, f"norm_{tag}")
    g, u, act = _ffn_up(h, wgu, li, TM, f"ffn_up_{tag}")
    xo = _mm_fwd(act, wdown, li, x, 0.5, TM, f"ffn_down_{tag}")
    return xo, (x, h, g, u, act)


def _ffn_backward(dx, saved, gain, wgu, wdown, d_wgu, d_wdown, li, tag):
    x, h, g, u, act = saved
    dgu = _ffn_dact(dx, wdown, li, g, u, 256, f"ffn_dact_{tag}")
    d_wdown = _mm_wgrad(act, dx, 0.5, d_wdown, li, 2 * TM, f"ffn_dwdown_{tag}", tn=512)
    d_wgu = _mm_wgrad_cols(h, dgu, d_wgu, li, 2 * TM, f"ffn_dwgu_{tag}")
    dx, d_gain = _mm_dx_norm_cols(dgu, wgu, li, x, gain, dx, 256, f"ffn_dx_{tag}")
    return dx, d_gain, d_wgu, d_wdown


STACKED = ("w_ffn1_gu", "w_ffn1_down", "w_qkv", "w_o", "w_pool_in", "w_ffn2_gu", "w_ffn2_down", "w_ple_gate")


def _local_step(x, p, target, W):
    depth = p.shape[0]
    saved = []
    for i in range(depth):
        j = i // 2
        s = {}
        x, s["ffn1"] = _ffn_forward(x, W["norm_ffn1"][i], W["w_ffn1_gu"], W["w_ffn1_down"], i, f"a{i}")
        s["x_mix"] = x
        hm = _rmsnorm(x, W["norm_mix"][i], TM, f"norm_mix{i}")
        s["hm"] = hm
        if i % 2 == 0:
            qkv = _mm_cols(hm, W["w_qkv"], j, F32, TM, f"qkv{i}")
            qs, kn, vb = _qk_norm(qkv, W["q_norm"][j], W["k_norm"][j], TM, f"qk_norm{i}")
            o, tiles = _attn_fwd(qs, kn, vb, f"attn_fwd{i}")
            x = _mm_fwd(o, W["w_o"], j, x, 1.0, TM, f"attn_out{i}")
            s["mix"] = (qkv, qs, kn, vb, o, tiles)
        else:
            u = _mm_plain(hm, W["w_pool_in"], j, F32, TM, f"pool_in{i}")
            x, pooled = _pool_fwd(u, W["w_pool_grp"][j], W["pool_scale"][j], x, TM, f"pool_fwd{i}")
            s["mix"] = (pooled,)
        x, s["ffn2"] = _ffn_forward(x, W["norm_ffn2"][i], W["w_ffn2_gu"], W["w_ffn2_down"], i, f"b{i}")
        s["x_ple"] = x
        x, hp, gp, pe = _ple_fwd(x, W["norm_ple"][i], W["w_ple_gate"], i, p[i], W["w_ple_proj"][i], TM, f"ple_fwd{i}")
        s["ple"] = (hp, gp, pe)
        saved.append(s)

    sq, dx = _loss_head(x, target, TM, "loss_head")
    G = {k: (lax.empty(v.shape, BF) if k in STACKED else [None] * len(v)) for k, v in W.items()}
    for i in reversed(range(depth)):
        j = i // 2
        s = saved[i]
        hp, gp, pe = s["ple"]
        dx, dgp, dpe, G["norm_ple"][i] = _ple_bwd(dx, gp, pe, W["w_ple_gate"], i, s["x_ple"], W["norm_ple"][i], TM, f"ple_bwd{i}")
        G["w_ple_gate"] = _mm_wgrad(hp, dgp, 1.0, G["w_ple_gate"], i, TM, f"ple_dwgate{i}")
        G["w_ple_proj"][i] = _mm_wgrad(p[i], dpe, 1.0, lax.empty((1,) + W["w_ple_proj"][i].shape, BF), 0, TM, f"ple_dwproj{i}")[0]
        dx, G["norm_ffn2"][i], G["w_ffn2_gu"], G["w_ffn2_down"] = _ffn_backward(
            dx, s["ffn2"], W["norm_ffn2"][i], W["w_ffn2_gu"], W["w_ffn2_down"], G["w_ffn2_gu"], G["w_ffn2_down"], i, f"b{i}")
        hm = s["hm"]
        if i % 2 == 0:
            qkv, qs, kn, vb, o, tiles = s["mix"]
            G["w_o"] = _mm_wgrad(o, dx, 1.0, G["w_o"], j, TM, f"attn_dwo{i}")
            do = _mm_plain(dx, W["w_o"], j, BF, TM, f"attn_do{i}", dims=NT)
            dqs, dkn, dv = _attn_bwd(qs, kn, vb, do, tiles, f"attn_bwd{i}")
            dq, dk, dvb, dqg, dkg = _qk_norm_bwd(qkv, dqs, dkn, dv, W["q_norm"][j], W["k_norm"][j], TM, f"qk_norm_bwd{i}")
            dqkv = jnp.concatenate([dq, dk, dvb], axis=1)
            G["q_norm"][j] = dqg[:, :HEAD_DIM] + dqg[:, HEAD_DIM:]
            G["k_norm"][j] = dkg[:, :HEAD_DIM] + dkg[:, HEAD_DIM:]
            G["w_qkv"] = _mm_wgrad_cols(hm, dqkv, G["w_qkv"], j, TM, f"attn_dwqkv{i}")
            dx, G["norm_mix"][i] = _mm_dx_norm_cols(dqkv, W["w_qkv"], j, s["x_mix"], W["norm_mix"][i], dx, TM, f"attn_dx{i}")
        else:
            (pooled,) = s["mix"]
            dp, G["w_pool_grp"][j], G["pool_scale"][j] = _pool_bwd_grp(dx, pooled, W["w_pool_grp"][j], W["pool_scale"][j], TM, f"pool_bwd_grp{i}")
            du = _pool_bwd_window(dp, TM, f"pool_bwd_win{i}")
            G["w_pool_in"] = _mm_wgrad(hm, du, 1.0, G["w_pool_in"], j, TM, f"pool_dwin{i}")
            dx, G["norm_mix"][i] = _mm_dx_norm(du, W["w_pool_in"], j, s["x_mix"], W["norm_mix"][i], dx, TM, f"pool_dx{i}")
        dx, G["norm_ffn1"][i], G["w_ffn1_gu"], G["w_ffn1_down"] = _ffn_backward(
            dx, s["ffn1"], W["norm_ffn1"][i], W["w_ffn1_gu"], W["w_ffn1_down"], G["w_ffn1_gu"], G["w_ffn1_down"], i, f"a{i}")
    return sq, dx, G


SHARDED = ("w_ffn1_gu", "w_ffn1_down", "w_qkv", "w_o", "w_pool_in", "w_pool_grp", "w_ffn2_gu", "w_ffn2_down", "w_ple_gate", "w_ple_proj")
COLUMN_SHARDED = ("w_ffn1_gu", "w_qkv", "w_ffn2_gu", "w_ple_proj")
NORMS = ("norm_ffn1", "norm_mix", "norm_ffn2", "norm_ple")
HEAD_GAINS = ("q_norm", "k_norm")
WEIGHTS = ("norm_ffn1", "w_ffn1_gu", "w_ffn1_down", "norm_mix", "w_qkv", "q_norm", "k_norm", "w_o", "w_pool_in", "w_pool_grp",
           "pool_scale", "norm_ffn2", "w_ffn2_gu", "w_ffn2_down", "norm_ple", "w_ple_gate", "w_ple_proj")
SMALL_ROWS = 24


def _whole_weights(w, gathered, pool_scale_all):
    W = {}
    for k in NORMS:
        W[k] = [_row(w[k][i]) for i in range(w[k].shape[0])]
    for k in HEAD_GAINS:
        W[k] = [_row(jnp.tile(w[k][j], LANES // HEAD_DIM)) for j in range(w[k].shape[0])]
    for k in SHARDED:
        g = gathered[k]
        L = g.shape[0]
        if k == "w_pool_grp":
            W[k] = [jnp.transpose(g[i], (1, 0, 2, 3)).reshape(g.shape[2], -1, g.shape[4]) for i in range(L)]
        elif k == "w_ple_proj":
            W[k] = [jnp.transpose(g[i], (1, 0, 2)).reshape(g.shape[2], -1) for i in range(L)]
        elif k in COLUMN_SHARDED:
            W[k] = g
        else:
            W[k] = g.reshape(L, -1, g.shape[3])
    W["pool_scale"] = [_row(pool_scale_all[j]) for j in range(pool_scale_all.shape[0])]
    return W


def _shard_major(k, g):
    if k == "w_pool_grp":
        return jnp.stack([jnp.transpose(t.reshape(t.shape[0], N_CHIPS, t.shape[1] // N_CHIPS, t.shape[2]), (1, 0, 2, 3)) for t in g])
    if k == "w_ple_proj":
        return jnp.stack([jnp.transpose(t.reshape(t.shape[0], N_CHIPS, t.shape[1] // N_CHIPS), (1, 0, 2)) for t in g])
    if k in COLUMN_SHARDED:
        return g
    return g.reshape(g.shape[0], N_CHIPS, g.shape[1] // N_CHIPS, g.shape[2])


def kernel(x, p, norm_ffn1, w_ffn1_gu, w_ffn1_down, norm_mix, w_qkv, q_norm, k_norm, w_o, w_pool_in, w_pool_grp, pool_scale, norm_ffn2, w_ffn2_gu, w_ffn2_down, norm_ple, w_ple_gate, w_ple_proj, loss_target, m_norm_ffn1, m_w_ffn1_gu, m_w_ffn1_down, m_norm_mix, m_w_qkv, m_q_norm, m_k_norm, m_w_o, m_w_pool_in, m_w_pool_grp, m_pool_scale, m_norm_ffn2, m_w_ffn2_gu, m_w_ffn2_down, m_norm_ple, m_w_ple_gate, m_w_ple_proj, v_norm_ffn1, v_w_ffn1_gu, v_w_ffn1_down, v_norm_mix, v_w_qkv, v_q_norm, v_k_norm, v_w_o, v_w_pool_in, v_w_pool_grp, v_pool_scale, v_norm_ffn2, v_w_ffn2_gu, v_w_ffn2_down, v_norm_ple, v_w_ple_gate, v_w_ple_proj):
    w = dict(norm_ffn1=norm_ffn1, w_ffn1_gu=w_ffn1_gu, w_ffn1_down=w_ffn1_down, norm_mix=norm_mix, w_qkv=w_qkv, q_norm=q_norm,
             k_norm=k_norm, w_o=w_o, w_pool_in=w_pool_in, w_pool_grp=w_pool_grp, pool_scale=pool_scale, norm_ffn2=norm_ffn2,
             w_ffn2_gu=w_ffn2_gu, w_ffn2_down=w_ffn2_down, norm_ple=norm_ple, w_ple_gate=w_ple_gate, w_ple_proj=w_ple_proj)
    m = dict(norm_ffn1=m_norm_ffn1, w_ffn1_gu=m_w_ffn1_gu, w_ffn1_down=m_w_ffn1_down, norm_mix=m_norm_mix, w_qkv=m_w_qkv,
             q_norm=m_q_norm, k_norm=m_k_norm, w_o=m_w_o, w_pool_in=m_w_pool_in, w_pool_grp=m_w_pool_grp, pool_scale=m_pool_scale,
             norm_ffn2=m_norm_ffn2, w_ffn2_gu=m_w_ffn2_gu, w_ffn2_down=m_w_ffn2_down, norm_ple=m_norm_ple, w_ple_gate=m_w_ple_gate,
             w_ple_proj=m_w_ple_proj)
    v = dict(norm_ffn1=v_norm_ffn1, w_ffn1_gu=v_w_ffn1_gu, w_ffn1_down=v_w_ffn1_down, norm_mix=v_norm_mix, w_qkv=v_w_qkv,
             q_norm=v_q_norm, k_norm=v_k_norm, w_o=v_w_o, w_pool_in=v_w_pool_in, w_pool_grp=v_w_pool_grp, pool_scale=v_pool_scale,
             norm_ffn2=v_norm_ffn2, w_ffn2_gu=v_w_ffn2_gu, w_ffn2_down=v_w_ffn2_down, norm_ple=v_norm_ple, w_ple_gate=v_w_ple_gate,
             w_ple_proj=v_w_ple_proj)
    chip = 2 * lax.axis_index("x") + lax.axis_index("y")
    D = x.shape[-1]
    shard_cols = pool_scale.shape[1]

    core = lax.axis_index("c")
    shards = [w[k].astype(BF) for k in SHARDED]
    gathered = {k: lax.dynamic_update_slice_in_dim(g, s[:, None], chip, axis=1)
                for k, s, g in zip(SHARDED, shards, _gather_weights(shards, "gather_weights"))}
    scale_rows = jnp.zeros((8, shard_cols), F32).at[:pool_scale.shape[0]].set(pool_scale)
    scale_all = _gather_small(scale_rows, "gather_pool_scale")
    pool_scale_all = jnp.transpose(scale_all[::2, :pool_scale.shape[0]], (1, 0, 2)).reshape(pool_scale.shape[0], D)
    W = _whole_weights(w, gathered, pool_scale_all)

    sq, dx, G = _local_step(x[0], p[:, 0], loss_target[0], W)
    loss = lax.psum(0.5 / D * sq[0, 0], ("x", "y", "c"))

    partials = [_shard_major(k, G[k]) for k in SHARDED]
    theirs = _swap_partials(partials, "swap_partials")
    chip_sums = [_add_pair(lax.dynamic_slice_in_dim(g, core * t.shape[0], t.shape[0], axis=0), t, f"pair_{k}")
                 for k, g, t in zip(SHARDED, partials, theirs)]
    slots = [lax.dynamic_update_slice_in_dim(r, lax.dynamic_index_in_dim(s, chip, axis=1, keepdims=False)[None], chip, axis=0)
             for s, r in zip(chip_sums, _scatter_sums(chip_sums, "scatter_sums"))]
    halves = [_sum_slots(s, f"sum_{k}") for k, s in zip(SHARDED, slots)]
    grads = dict(zip(SHARDED, _join_halves(halves, "join_halves")))

    small = [G[k][i] for k in NORMS for i in range(len(G[k]))]
    small += [jnp.pad(jnp.concatenate(G[k], axis=1), ((0, 0), (0, D - len(G[k]) * HEAD_DIM))) for k in HEAD_GAINS]
    small += G["pool_scale"]
    small = jnp.concatenate(small + [jnp.zeros((SMALL_ROWS - len(small), D), F32)], axis=0)
    small = _sum_slots(_gather_small(small, "gather_small_grads"), "sum_small_grads")
    row = 0
    for k in NORMS:
        grads[k] = small[row:row + w[k].shape[0]]
        row += w[k].shape[0]
    for k in HEAD_GAINS:
        grads[k] = small[row, :w[k].size].reshape(w[k].shape)
        row += 1
    grads["pool_scale"] = lax.dynamic_slice_in_dim(small[row:row + pool_scale.shape[0]], chip * shard_cols, shard_cols, axis=1)

    delta, new_m, new_v = {}, {}, {}
    small_names = NORMS + HEAD_GAINS + ("pool_scale",)

    def pack(d):
        rows = [jnp.pad(d[k].reshape(-1, d[k].shape[-1]) if k in NORMS + ("pool_scale",) else d[k].reshape(1, -1),
                        ((0, 0), (0, D - (d[k].shape[-1] if k in NORMS + ("pool_scale",) else d[k].size))), constant_values=1.0)
                for k in small_names]
        n = sum(r.shape[0] for r in rows)
        return jnp.concatenate(rows + [jnp.ones((SMALL_ROWS - n, D), F32)], axis=0)

    packed = _adamw(pack(w), pack(grads), pack(m), pack(v), "adamw_small")
    row = 0
    for k in small_names:
        n = w[k].shape[0] if k in NORMS + ("pool_scale",) else 1
        width = w[k].shape[-1] if k in NORMS + ("pool_scale",) else w[k].size
        for dst, src in zip((delta, new_m, new_v), packed):
            dst[k] = src[row:row + n, :width].reshape(w[k].shape)
        row += n
    for k in SHARDED:
        delta[k], new_m[k], new_v[k] = _adamw(w[k], grads[k], m[k], v[k], f"adamw_{k}")

    return (loss, dx[None], *[grads[k] for k in WEIGHTS], *[delta[k] for k in WEIGHTS],
            *[new_m[k] for k in WEIGHTS], *[new_v[k] for k in WEIGHTS])
```

```python
import jax
import jax.numpy as jnp
from jax import lax
from jax.experimental import pallas as pl
from jax.experimental.pallas import tpu as pltpu

BF = jnp.bfloat16
F32 = jnp.float32

N_HEADS = 16
HEAD_DIM = 64
POOL_WINDOWS = (2, 4, 8, 16)
POOL_HALO = 16
EPS = 1e-6
ADAM_LR = 0.001
ADAM_B1 = 0.9
ADAM_B2 = 0.999
ADAM_EPS = 1e-08
ADAM_WD = 0.01
ADAM_STEP = 10

N_CHIPS = 4
N_DEV = 8
LANES = 128
VMEM_LIMIT = 56 * 1024 * 1024
ATT_TK = 256
ATT_TQ = 1024
ATT_SLOTS = 3
MESH = pl.DeviceIdType.MESH

NN = (((1,), (0,)), ((), ()))
NT = (((1,), (1,)), ((), ()))
TN = (((0,), (0,)), ((), ()))


def _params(n_axes):
    return pltpu.CompilerParams(dimension_semantics=("arbitrary",) * n_axes, vmem_limit_bytes=VMEM_LIMIT)


def _dot(a, b, dims):
    return lax.dot_general(a, b, dims, preferred_element_type=F32)


def _layer(li, block, index_map):
    return pl.BlockSpec((None,) + tuple(block), lambda *g: (li,) + tuple(index_map(*g)))


def _mm(a, b, extra, out_shapes, *, dims, grid, a_spec, b_spec, extra_specs, out_specs, acc_shape, epilogue, name, into=None):
    nk = grid[-1]
    aliases = {}
    if into is not None:
        aliases = {2 + len(extra): 0}
        extra = [*extra, into]
        extra_specs = [*extra_specs, pl.BlockSpec(memory_space=pl.ANY)]
        out_shapes = [jax.ShapeDtypeStruct(into.shape, into.dtype), *out_shapes[1:]]
    n_extra = len(extra)
    n_out = len(out_shapes)

    def body(a_ref, b_ref, *rest):
        ex = rest[:n_extra]
        outs = rest[n_extra:n_extra + n_out]
        acc = rest[-1]
        k = pl.program_id(len(grid) - 1)

        @pl.when(k == 0)
        def _():
            acc[...] = jnp.zeros_like(acc)

        acc[...] += _dot(a_ref[...].astype(BF), b_ref[...].astype(BF), dims)
        first = pl.program_id(0) == 0

        @pl.when(k == nk - 1)
        def _():
            epilogue(acc[...], ex, outs, first)

    return pl.pallas_call(
        body, grid=grid, in_specs=[a_spec, b_spec, *extra_specs], out_specs=out_specs, out_shape=out_shapes,
        scratch_shapes=[pltpu.VMEM(acc_shape, F32)], compiler_params=_params(len(grid)), name=name,
        input_output_aliases=aliases,
    )(a, b, *extra)


def _store(dtype, scale=1.0):
    def ep(acc, ex, outs, first):
        outs[0][...] = (acc * scale).astype(dtype)
    return ep


def _residual(scale):
    def ep(acc, ex, outs, first):
        outs[0][...] = ex[0][...] + scale * acc
    return ep


def _rms_bwd_epilogue(acc, ex, outs, first):
    x_ref, g_ref, dx_ref = ex
    dxo_ref, dg_ref = outs
    x = x_ref[...]
    r = lax.rsqrt(jnp.mean(x * x, axis=-1, keepdims=True) + EPS)
    xh = x * r
    dxh = acc * g_ref[...]
    dxo_ref[...] = dx_ref[...] + r * (dxh - xh * jnp.mean(dxh * xh, axis=-1, keepdims=True))

    @pl.when(first)
    def _():
        dg_ref[...] = jnp.zeros_like(dg_ref)

    dg_ref[...] += jnp.sum(acc * xh, axis=0, keepdims=True)


def _mm_fwd(a, w, li, x, scale, tm, name):
    S, K = a.shape
    N = w.shape[2]
    return _mm(a, w, [x], [jax.ShapeDtypeStruct((S, N), F32)], dims=NN, grid=(S // tm, 1),
               a_spec=pl.BlockSpec((tm, K), lambda m, k: (m, 0)), b_spec=_layer(li, (K, N), lambda m, k: (0, 0)),
               extra_specs=[pl.BlockSpec((tm, N), lambda m, k: (m, 0))], out_specs=[pl.BlockSpec((tm, N), lambda m, k: (m, 0))],
               acc_shape=(tm, N), epilogue=_residual(scale), name=name)[0]


def _mm_plain(a, w, li, dtype, tm, name, dims=NN):
    S, K = a.shape
    N = w.shape[2] if dims == NN else w.shape[1]
    return _mm(a, w, [], [jax.ShapeDtypeStruct((S, N), dtype)], dims=dims, grid=(S // tm, 1),
               a_spec=pl.BlockSpec((tm, K), lambda m, k: (m, 0)), b_spec=_layer(li, w.shape[1:], lambda m, k: (0, 0)),
               extra_specs=[], out_specs=[pl.BlockSpec((tm, N), lambda m, k: (m, 0))],
               acc_shape=(tm, N), epilogue=_store(dtype), name=name)[0]


def _mm_cols(a, w, li, dtype, tm, name):
    S, K = a.shape
    _, nj, _, ns = w.shape

    def body(a_ref, w_ref, o_ref):
        av = a_ref[...]
        for j in range(nj):
            o_ref[:, j * ns:(j + 1) * ns] = _dot(av, w_ref[j], NN).astype(dtype)

    return pl.pallas_call(
        body, grid=(S // tm,), in_specs=[pl.BlockSpec((tm, K), lambda m: (m, 0)), _layer(li, (nj, K, ns), lambda m: (0, 0, 0))],
        out_specs=pl.BlockSpec((tm, nj * ns), lambda m: (m, 0)), out_shape=jax.ShapeDtypeStruct((S, nj * ns), dtype),
        compiler_params=_params(1), name=name)(a, w)


def _mm_wgrad(a, b, scale, into, li, tk, name, tn=None):
    S, M = a.shape
    N = b.shape[1]
    tn = N if tn is None else tn
    return _mm(a, b, [], [None], dims=TN, grid=(N // tn, S // tk),
               a_spec=pl.BlockSpec((tk, M), lambda n, k: (k, 0)), b_spec=pl.BlockSpec((tk, tn), lambda n, k: (k, n)),
               extra_specs=[], out_specs=[_layer(li, (M, tn), lambda n, k: (0, n))],
               acc_shape=(M, tn), epilogue=_store(BF, scale), name=name, into=into)[0]


def _mm_wgrad_cols(a, b, into, li, tk, name):
    S, M = a.shape
    _, nj, _, ns = into.shape
    return _mm(a, b, [], [None], dims=TN, grid=(nj, S // tk),
               a_spec=pl.BlockSpec((tk, M), lambda j, k: (k, 0)), b_spec=pl.BlockSpec((tk, ns), lambda j, k: (k, j)),
               extra_specs=[], out_specs=[_layer(li, (None, M, ns), lambda j, k: (j, 0, 0))],
               acc_shape=(M, ns), epilogue=_store(BF), name=name, into=into)[0]


def _mm_dx_norm(a, w, li, x, gain, dx, tm, name):
    S, K = a.shape
    N = w.shape[1]
    row = pl.BlockSpec((tm, N), lambda m, k: (m, 0))
    vec = pl.BlockSpec((1, N), lambda m, k: (0, 0))
    return _mm(a, w, [x, gain, dx], [jax.ShapeDtypeStruct((S, N), F32), jax.ShapeDtypeStruct((1, N), F32)], dims=NT,
               grid=(S // tm, 1), a_spec=pl.BlockSpec((tm, K), lambda m, k: (m, 0)), b_spec=_layer(li, (N, K), lambda m, k: (0, 0)),
               extra_specs=[row, vec, row], out_specs=[row, vec], acc_shape=(tm, N), epilogue=_rms_bwd_epilogue, name=name)


def _mm_dx_norm_cols(a, w, li, x, gain, dx, tm, name):
    S = a.shape[0]
    _, nj, N, ks = w.shape

    def body(a_ref, w_ref, x_ref, g_ref, dx_ref, dxo_ref, dg_ref):
        acc = _dot(a_ref[:, :ks], w_ref[0], NT)
        for j in range(1, nj):
            acc = acc + _dot(a_ref[:, j * ks:(j + 1) * ks], w_ref[j], NT)
        _rms_bwd_epilogue(acc, (x_ref, g_ref, dx_ref), (dxo_ref, dg_ref), pl.program_id(0) == 0)

    row = pl.BlockSpec((tm, N), lambda m: (m, 0))
    vec = pl.BlockSpec((1, N), lambda m: (0, 0))
    return pl.pallas_call(
        body, grid=(S // tm,),
        in_specs=[pl.BlockSpec((tm, nj * ks), lambda m: (m, 0)), _layer(li, (nj, N, ks), lambda m: (0, 0, 0)), row, vec, row],
        out_specs=[row, vec], out_shape=[jax.ShapeDtypeStruct((S, N), F32), jax.ShapeDtypeStruct((1, N), F32)],
        compiler_params=_params(1), name=name)(a, w, x, gain, dx)


def _rmsnorm(x, gain, tm, name):
    S, D = x.shape

    def body(x_ref, g_ref, h_ref):
        xv = x_ref[...]
        r = lax.rsqrt(jnp.mean(xv * xv, axis=-1, keepdims=True) + EPS)
        h_ref[...] = (xv * r * g_ref[...]).astype(BF)

    return pl.pallas_call(
        body, grid=(S // tm,), in_specs=[pl.BlockSpec((tm, D), lambda m: (m, 0)), pl.BlockSpec((1, D), lambda m: (0, 0))],
        out_specs=pl.BlockSpec((tm, D), lambda m: (m, 0)), out_shape=jax.ShapeDtypeStruct((S, D), BF),
        compiler_params=_params(1), name=name)(x, gain)


def _ffn_up(h, wgu, li, tm, name):
    S, D = h.shape
    ns = wgu.shape[3]
    half = wgu.shape[1] // 2

    def body(h_ref, wg_ref, wu_ref, g_ref, u_ref, act_ref):
        hv = h_ref[...]
        g = _dot(hv, wg_ref[...], NN)
        u = _dot(hv, wu_ref[...], NN)
        g_ref[...] = g.astype(BF)
        u_ref[...] = u.astype(BF)
        act_ref[...] = (g * jax.nn.sigmoid(g) * u).astype(BF)

    out = jax.ShapeDtypeStruct((S, half * ns), BF)
    tile = pl.BlockSpec((tm, ns), lambda j, m: (m, j))
    return pl.pallas_call(
        body, grid=(half, S // tm),
        in_specs=[pl.BlockSpec((tm, D), lambda j, m: (m, 0)), _layer(li, (None, D, ns), lambda j, m: (j, 0, 0)),
                  _layer(li, (None, D, ns), lambda j, m: (j + half, 0, 0))],
        out_specs=[tile, tile, tile], out_shape=[out, out, out], compiler_params=_params(2), name=name)(h, wgu, wgu)


def _ffn_dact(dx, wdown, li, g, u, tm, name):
    S, D = dx.shape
    F = wdown.shape[1]

    def ep(acc, ex, outs, first):
        gv = ex[0][...].astype(F32)
        uv = ex[1][...].astype(F32)
        da = 0.5 * acc
        sg = jax.nn.sigmoid(gv)
        outs[0][:, :F] = (da * uv * (sg * (1.0 + gv * (1.0 - sg)))).astype(BF)
        outs[0][:, F:] = (da * (gv * sg)).astype(BF)

    row = pl.BlockSpec((tm, F), lambda m, k: (m, 0))
    return _mm(dx, wdown, [g, u], [jax.ShapeDtypeStruct((S, 2 * F), BF)], dims=NT, grid=(S // tm, 1),
               a_spec=pl.BlockSpec((tm, D), lambda m, k: (m, 0)), b_spec=_layer(li, (F, D), lambda m, k: (0, 0)),
               extra_specs=[row, row], out_specs=[pl.BlockSpec((tm, 2 * F), lambda m, k: (m, 0))],
               acc_shape=(tm, F), epilogue=ep, name=name)[0]


def _ple_fwd(x, gain, wgate, li, p, wproj, tm, name):
    S, D = x.shape
    P = p.shape[1]

    def body(x_ref, g_ref, wg_ref, p_ref, wp_ref, xo_ref, hp_ref, gp_ref, pe_ref):
        xv = x_ref[...]
        r = lax.rsqrt(jnp.mean(xv * xv, axis=-1, keepdims=True) + EPS)
        hp = (xv * r * g_ref[...]).astype(BF)
        gp = _dot(hp, wg_ref[...], NN)
        pe = _dot(p_ref[...].astype(BF), wp_ref[...], NN)
        xo_ref[...] = xv + jax.nn.sigmoid(gp) * pe
        hp_ref[...] = hp
        gp_ref[...] = gp.astype(BF)
        pe_ref[...] = pe.astype(BF)

    row = pl.BlockSpec((tm, D), lambda m: (m, 0))
    return pl.pallas_call(
        body, grid=(S // tm,),
        in_specs=[row, pl.BlockSpec((1, D), lambda m: (0, 0)), _layer(li, (D, D), lambda m: (0, 0)),
                  pl.BlockSpec((tm, P), lambda m: (m, 0)), pl.BlockSpec((P, D), lambda m: (0, 0))],
        out_specs=[row, row, row, row],
        out_shape=[jax.ShapeDtypeStruct((S, D), F32)] + [jax.ShapeDtypeStruct((S, D), BF)] * 3,
        compiler_params=_params(1), name=name)(x, gain, wgate, p, wproj)


def _ple_bwd(dx, gp, pe, wgate, li, x, gain, tm, name):
    S, D = dx.shape

    def body(dx_ref, gp_ref, pe_ref, wg_ref, x_ref, g_ref, dxo_ref, dgp_ref, dpe_ref, dg_ref):
        dxv = dx_ref[...]
        sg = jax.nn.sigmoid(gp_ref[...].astype(F32))
        dgp = (dxv * pe_ref[...].astype(F32) * (sg * (1.0 - sg))).astype(BF)
        dgp_ref[...] = dgp
        dpe_ref[...] = (dxv * sg).astype(BF)
        dhp = _dot(dgp, wg_ref[...], NT)
        _rms_bwd_epilogue(dhp, (x_ref, g_ref, dx_ref), (dxo_ref, dg_ref), pl.program_id(0) == 0)

    row = pl.BlockSpec((tm, D), lambda m: (m, 0))
    vec = pl.BlockSpec((1, D), lambda m: (0, 0))
    return pl.pallas_call(
        body, grid=(S // tm,), in_specs=[row, row, row, _layer(li, (D, D), lambda m: (0, 0)), row, vec],
        out_specs=[row, row, row, vec],
        out_shape=[jax.ShapeDtypeStruct((S, D), F32), jax.ShapeDtypeStruct((S, D), BF), jax.ShapeDtypeStruct((S, D), BF),
                   jax.ShapeDtypeStruct((1, D), F32)],
        compiler_params=_params(1), name=name)(dx, gp, pe, wgate, x, gain)


def _loss_head(y, target, tm, name):
    S, D = y.shape

    def body(y_ref, t_ref, sq_ref, dy_ref):
        d = y_ref[...] - t_ref[...]
        dy_ref[...] = d * (1.0 / D)

        @pl.when(pl.program_id(0) == 0)
        def _():
            sq_ref[...] = jnp.zeros_like(sq_ref)

        sq_ref[...] += jnp.sum(d * d, keepdims=True)

    row = pl.BlockSpec((tm, D), lambda m: (m, 0))
    return pl.pallas_call(
        body, grid=(S // tm,), in_specs=[row, row], out_specs=[pl.BlockSpec((1, 1), lambda m: (0, 0)), row],
        out_shape=[jax.ShapeDtypeStruct((1, 1), F32), jax.ShapeDtypeStruct((S, D), F32)],
        compiler_params=_params(1), name=name)(y, target)


def _qk_norm(qkv, qgain, kgain, tm, name):
    S = qkv.shape[0]
    D = qkv.shape[1] // 3
    nb = D // LANES

    def norm2(t, gain):
        lo = lax.broadcasted_iota(jnp.int32, t.shape, 1) < HEAD_DIM
        sq = t * t
        s_lo = jnp.sum(jnp.where(lo, sq, 0.0), axis=1, keepdims=True)
        s_hi = jnp.sum(jnp.where(lo, 0.0, sq), axis=1, keepdims=True)
        r = lax.rsqrt(jnp.where(lo, s_lo, s_hi) * (1.0 / HEAD_DIM) + EPS)
        return t * r * gain

    def body(q_ref, k_ref, v_ref, qg_ref, kg_ref, qo_ref, ko_ref, vo_ref):
        qo_ref[...] = (norm2(q_ref[...], qg_ref[...]) * (HEAD_DIM ** -0.5)).astype(BF)
        ko_ref[...] = norm2(k_ref[...], kg_ref[...]).astype(BF)
        vo_ref[...] = v_ref[...].astype(BF)

    vec = pl.BlockSpec((1, LANES), lambda m, h: (0, 0))
    tile = pl.BlockSpec((tm, LANES), lambda m, h: (m, h))
    out = jax.ShapeDtypeStruct((S, D), BF)
    return pl.pallas_call(
        body, grid=(S // tm, nb),
        in_specs=[tile, pl.BlockSpec((tm, LANES), lambda m, h: (m, nb + h)), pl.BlockSpec((tm, LANES), lambda m, h: (m, 2 * nb + h)), vec, vec],
        out_specs=[tile, tile, tile], out_shape=[out, out, out], compiler_params=_params(2), name=name)(qkv, qkv, qkv, qgain, kgain)


def _qk_norm_bwd(qkv, dqs, dkn, dv, qgain, kgain, tm, name):
    S = qkv.shape[0]
    D = qkv.shape[1] // 3
    nb = D // LANES

    def norm2_bwd(t, gain, dn):
        lo = lax.broadcasted_iota(jnp.int32, t.shape, 1) < HEAD_DIM

        def headsum(val):
            s_lo = jnp.sum(jnp.where(lo, val, 0.0), axis=1, keepdims=True)
            s_hi = jnp.sum(jnp.where(lo, 0.0, val), axis=1, keepdims=True)
            return jnp.where(lo, s_lo, s_hi)

        r = lax.rsqrt(headsum(t * t) * (1.0 / HEAD_DIM) + EPS)
        th = t * r
        dth = dn * gain
        dt = r * (dth - th * (headsum(dth * th) * (1.0 / HEAD_DIM)))
        return dt, jnp.sum(dn * th, axis=0, keepdims=True)

    def body(q_ref, k_ref, dq_ref, dk_ref, dv_ref, qg_ref, kg_ref, dqo_ref, dko_ref, dvo_ref, dqg_ref, dkg_ref):
        dq, dqg = norm2_bwd(q_ref[...], qg_ref[...], dq_ref[...] * (HEAD_DIM ** -0.5))
        dk, dkg = norm2_bwd(k_ref[...], kg_ref[...], dk_ref[...])
        dqo_ref[...] = dq.astype(BF)
        dko_ref[...] = dk.astype(BF)
        dvo_ref[...] = dv_ref[...].astype(BF)

        @pl.when((pl.program_id(0) == 0) & (pl.program_id(1) == 0))
        def _():
            dqg_ref[...] = jnp.zeros_like(dqg_ref)
            dkg_ref[...] = jnp.zeros_like(dkg_ref)

        dqg_ref[...] += dqg
        dkg_ref[...] += dkg

    vec = pl.BlockSpec((1, LANES), lambda m, h: (0, 0))
    tile = pl.BlockSpec((tm, LANES), lambda m, h: (m, h))
    return pl.pallas_call(
        body, grid=(S // tm, nb),
        in_specs=[tile, pl.BlockSpec((tm, LANES), lambda m, h: (m, nb + h)), tile, tile, tile, vec, vec],
        out_specs=[tile, tile, tile, vec, vec],
        out_shape=[jax.ShapeDtypeStruct((S, D), BF)] * 3 + [jax.ShapeDtypeStruct((1, LANES), F32)] * 2,
        compiler_params=_params(2), name=name)(qkv, qkv, dqs, dkn, dv, qgain, kgain)


HEADS_PER_BLOCK = LANES // HEAD_DIM


def _key_order(tk, left):
    j = lax.broadcasted_iota(jnp.int32, (tk, tk), 0)
    s = lax.broadcasted_iota(jnp.int32, (tk, tk), 1)
    return (j < s if left else j > s).astype(BF)


def _attn_tiles(qs, ks, stays, valid, after):
    zs = [_dot(q, k, NT) for q, k in zip(qs, ks)]
    sps = [jnp.maximum(z, 0.0) + jnp.log(1.0 + jnp.exp(-jnp.abs(z))) for z in zs]
    if valid is not None:
        sps = [jnp.where(valid, sp, 0.0) for sp in sps]
    rights = [_dot(sp.astype(BF), after, NN) for sp in sps]
    ws = [jnp.exp((z - sp) - r + stay) for z, sp, r, stay in zip(zs, sps, rights, stays)]
    if valid is not None:
        ws = [jnp.where(valid, w, 0.0) for w in ws]
    return ws, [r[:, :1] + sp[:, :1] for r, sp in zip(rights, sps)]


def _below(r0, t):
    return t if r0 == 0 else t[r0:]


def _with_below(r0, t, part):
    return part if r0 == 0 else jnp.concatenate([t[:r0], part], axis=0)


def _diagonal_mask(tq, tk, r0):
    return lax.broadcasted_iota(jnp.int32, (tq - r0, tk), 1) < lax.broadcasted_iota(jnp.int32, (tq - r0, tk), 0)


def _attn_walk(qi, tq, tk, block, carry):
    n_diag = tq // tk
    for n, d in enumerate(reversed(range(n_diag))):
        carry = block(n, n_diag * qi + d, d * tk, _diagonal_mask(tq, tk, d * tk), carry)
    return lax.fori_loop(0, n_diag * qi, lambda i, c: block(n_diag + i, n_diag * qi - 1 - i, 0, None, c), carry)


def _tile_slot(qi, kj, tq, tk):
    n_diag = tq // tk
    return n_diag * (qi * (qi + 1) // 2) + kj


def _attn_fwd(qs, kn, vb, name):
    S, D = qs.shape
    TQ, TK = ATT_TQ, ATT_TK
    heads = [slice(hh * HEAD_DIM, (hh + 1) * HEAD_DIM) for hh in range(HEADS_PER_BLOCK)]

    def body(q_ref, k_ref, v_ref, o_ref, tiles_ref, buf, sems):
        hp, qi = pl.program_id(0), pl.program_id(1)
        after = _key_order(TK, left=False)
        q = [q_ref[:, lanes] for lanes in heads]
        n_blocks = (TQ // TK) * (qi + 1)

        def save(slot, kj):
            return pltpu.make_async_copy(buf.at[slot], tiles_ref.at[hp, _tile_slot(qi, kj, TQ, TK)], sems.at[slot])

        def block(n, kj, r0, valid, carry):
            slot = n % ATT_SLOTS
            rows = pl.ds(pl.multiple_of(kj * TK, TK), TK)
            stays = [_below(r0, c[1]) for c in carry]
            ws, totals = _attn_tiles([_below(r0, qh) for qh in q], [k_ref[rows, lanes] for lanes in heads], stays, valid, after)
            wbs = [w.astype(BF) for w in ws]
            outs = [_dot(wb, v_ref[rows, lanes], NN) for wb, lanes in zip(wbs, heads)]

            @pl.when(n >= ATT_SLOTS)
            def _():
                save(slot, kj).wait()

            for hh, wb in enumerate(wbs):
                if r0:
                    buf[slot, hh, :r0] = jnp.zeros((r0, TK), BF)
                buf[slot, hh, r0:] = wb
            save(slot, kj).start()
            return tuple((_with_below(r0, c[0], _below(r0, c[0]) + o), _with_below(r0, c[1], stay - t))
                         for c, o, stay, t in zip(carry, outs, stays, totals))

        carry = tuple((jnp.zeros((TQ, HEAD_DIM), F32), jnp.zeros((TQ, 1), F32)) for _ in heads)
        carry = _attn_walk(qi, TQ, TK, block, carry)
        for slot in range(ATT_SLOTS):
            @pl.when(slot < n_blocks)
            def _():
                save(slot, 0).wait()
        for hh, lanes in enumerate(heads):
            o_ref[:, lanes] = carry[hh][0].astype(BF)

    tile = pl.BlockSpec((TQ, LANES), lambda h, m: (m, h))
    full = pl.BlockSpec((S, LANES), lambda h, m: (0, h))
    n_tiles = _tile_slot(S // TQ, 0, TQ, TK)
    return pl.pallas_call(
        body, grid=(D // LANES, S // TQ), in_specs=[tile, full, full], out_specs=[tile, ANY],
        out_shape=[jax.ShapeDtypeStruct((S, D), BF), jax.ShapeDtypeStruct((D // LANES, n_tiles, HEADS_PER_BLOCK, TQ, TK), BF)],
        scratch_shapes=[pltpu.VMEM((ATT_SLOTS, HEADS_PER_BLOCK, TQ, TK), BF), pltpu.SemaphoreType.DMA((ATT_SLOTS,))],
        compiler_params=_params(2), name=name)(qs, kn, vb)


def _attn_bwd(qs, kn, vb, do, tiles, name):
    S, D = qs.shape
    TQ, TK = ATT_TQ, ATT_TK
    heads = [slice(hh * HEAD_DIM, (hh + 1) * HEAD_DIM) for hh in range(HEADS_PER_BLOCK)]
    n_diag = TQ // TK

    def body(q_ref, k_ref, v_ref, do_ref, tiles_ref, dq_ref, dk_ref, dv_ref, buf, sems):
        hp, qi = pl.program_id(0), pl.program_id(1)

        @pl.when(qi == 0)
        def _():
            dk_ref[...] = jnp.zeros_like(dk_ref)
            dv_ref[...] = jnp.zeros_like(dv_ref)

        before = _key_order(TK, left=True)
        q = [q_ref[:, lanes] for lanes in heads]
        dout = [do_ref[:, lanes] for lanes in heads]
        n_rest = n_diag * qi
        n_blocks = n_rest + n_diag

        def fetch(slot, kj):
            return pltpu.make_async_copy(tiles_ref.at[hp, _tile_slot(qi, kj, TQ, TK)], buf.at[slot], sems.at[slot])

        for slot in range(ATT_SLOTS):
            @pl.when(slot < n_blocks)
            def _():
                fetch(slot, slot).start()

        def block(kj, r0, valid, carry):
            slot = kj % ATT_SLOTS
            rows = pl.ds(pl.multiple_of(kj * TK, TK), TK)
            ks = [k_ref[rows, lanes] for lanes in heads]
            qs_, douts = [_below(r0, qh) for qh in q], [_below(r0, d) for d in dout]
            betas = [jax.nn.sigmoid(_dot(qh, k, NT)) for qh, k in zip(qs_, ks)]
            das = [_dot(d, v_ref[rows, lanes], NT) for d, lanes in zip(douts, heads)]
            fetch(slot, kj).wait()
            wbs = [buf[slot, hh, r0:] for hh in range(HEADS_PER_BLOCK)]
            gs = [wb.astype(F32) * da for wb, da in zip(wbs, das)]
            lefts = [_dot(g.astype(BF), before, NN) for g in gs]
            dzs = [g - beta * (g + (_below(r0, c[1]) + left)) for g, beta, c, left in zip(gs, betas, carry, lefts)]
            if valid is not None:
                dzs = [jnp.where(valid, dz, 0.0) for dz in dzs]
            dzbs = [dz.astype(BF) for dz in dzs]
            for hh, lanes in enumerate(heads):
                dk_ref[rows, lanes] += _dot(dzbs[hh], qs_[hh], TN)
                dv_ref[rows, lanes] += _dot(wbs[hh], douts[hh], TN)
            new = tuple((_with_below(r0, c[0], _below(r0, c[0]) + _dot(dzb, k, NN)),
                         _with_below(r0, c[1], _below(r0, c[1]) + jnp.sum(g, axis=1, keepdims=True)))
                        for c, dzb, k, g in zip(carry, dzbs, ks, gs))

            @pl.when(kj + ATT_SLOTS < n_blocks)
            def _():
                fetch(slot, kj + ATT_SLOTS).start()

            return new

        carry = tuple((jnp.zeros((TQ, HEAD_DIM), F32), jnp.zeros((TQ, 1), F32)) for _ in heads)
        carry = lax.fori_loop(0, n_rest, lambda i, c: block(i, 0, None, c), carry)
        for d in range(n_diag):
            carry = block(n_rest + d, d * TK, _diagonal_mask(TQ, TK, d * TK), carry)
        for hh, lanes in enumerate(heads):
            dq_ref[:, lanes] = carry[hh][0]

    tile = pl.BlockSpec((TQ, LANES), lambda h, m: (m, h))
    full = pl.BlockSpec((S, LANES), lambda h, m: (0, h))
    out = jax.ShapeDtypeStruct((S, D), F32)
    return pl.pallas_call(
        body, grid=(D // LANES, S // TQ), in_specs=[tile, full, full, tile, ANY], out_specs=[tile, full, full],
        out_shape=[out, out, out],
        scratch_shapes=[pltpu.VMEM((ATT_SLOTS, HEADS_PER_BLOCK, TQ, TK), BF), pltpu.SemaphoreType.DMA((ATT_SLOTS,))],
        compiler_params=_params(2), name=name)(qs, kn, vb, do, tiles)


def _pool_counts(T, first_row):
    pos = first_row + lax.broadcasted_iota(jnp.int32, (T, 1), 0)
    return [jnp.minimum(pos + 1, w).astype(F32) for w in POOL_WINDOWS]


def _pool_fwd(u, wgrp, scale, x, tm, name):
    S, D = u.shape
    G = len(POOL_WINDOWS)
    C = D // G
    H = POOL_HALO

    def body(u_ref, prev_ref, w_ref, s_ref, x_ref, xo_ref, pooled_ref):
        m = pl.program_id(0)
        prev = jnp.where(m == 0, 0.0, prev_ref[...])
        ext = jnp.concatenate([prev, u_ref[...]], axis=0)
        counts = _pool_counts(tm, m * tm)
        ys = []
        acc = ext
        shift = 1
        for gi, w in enumerate(POOL_WINDOWS):
            while shift < w:
                acc = acc + pltpu.roll(acc, shift, axis=0)
                shift *= 2
            cols = slice(gi * C, (gi + 1) * C)
            pooled = (acc[H:, cols] / counts[gi] - ext[H:, cols]).astype(BF)
            pooled_ref[:, cols] = pooled
            ys.append(_dot(pooled, w_ref[gi], NN))
        xo_ref[...] = x_ref[...] + jnp.concatenate(ys, axis=1) * s_ref[...]

    row = pl.BlockSpec((tm, D), lambda m: (m, 0))
    return pl.pallas_call(
        body, grid=(S // tm,),
        in_specs=[row, pl.BlockSpec((H, D), lambda m: (jnp.maximum(m * (tm // H) - 1, 0), 0)),
                  pl.BlockSpec((G, C, C), lambda m: (0, 0, 0)), pl.BlockSpec((1, D), lambda m: (0, 0)), row],
        out_specs=[row, row], out_shape=[jax.ShapeDtypeStruct((S, D), F32), jax.ShapeDtypeStruct((S, D), BF)],
        compiler_params=_params(1), name=name)(u, u, wgrp, scale, x)


def _pool_bwd_grp(dx, pooled, wgrp, scale, tm, name):
    S, D = dx.shape
    G = len(POOL_WINDOWS)
    C = D // G

    def body(dx_ref, pooled_ref, w_ref, s_ref, dp_ref, dw_ref, ds_ref, dw_acc):
        m = pl.program_id(0)

        @pl.when(m == 0)
        def _():
            dw_acc[...] = jnp.zeros_like(dw_acc)
            ds_ref[...] = jnp.zeros_like(ds_ref)

        dxv = dx_ref[...]
        dy = (dxv * s_ref[...]).astype(BF)
        ys = []
        for gi in range(G):
            cols = slice(gi * C, (gi + 1) * C)
            pg = pooled_ref[:, cols]
            ys.append(_dot(pg, w_ref[gi], NN))
            dw_acc[gi] += _dot(pg, dy[:, cols], TN)
            dp_ref[:, cols] = _dot(dy[:, cols], w_ref[gi], NT)
        ds_ref[...] += jnp.sum(dxv * jnp.concatenate(ys, axis=1), axis=0, keepdims=True)

        @pl.when(m == S // tm - 1)
        def _():
            dw_ref[...] = dw_acc[...].astype(BF)

    row = pl.BlockSpec((tm, D), lambda m: (m, 0))
    wspec = pl.BlockSpec((G, C, C), lambda m: (0, 0, 0))
    vec = pl.BlockSpec((1, D), lambda m: (0, 0))
    return pl.pallas_call(
        body, grid=(S // tm,), in_specs=[row, row, wspec, vec], out_specs=[row, wspec, vec],
        out_shape=[jax.ShapeDtypeStruct((S, D), F32), jax.ShapeDtypeStruct((G, C, C), BF), jax.ShapeDtypeStruct((1, D), F32)],
        scratch_shapes=[pltpu.VMEM((G, C, C), F32)], compiler_params=_params(1), name=name)(dx, pooled, wgrp, scale)


def _pool_bwd_window(dp, tm, name):
    S, D = dp.shape
    G = len(POOL_WINDOWS)
    C = D // G
    H = POOL_HALO
    last = S // tm - 1

    def body(dp_ref, next_ref, du_ref):
        m = pl.program_id(0)
        counts = _pool_counts(tm + H, m * tm)
        nxt = jnp.where(m == last, 0.0, next_ref[...])
        ext = jnp.concatenate([dp_ref[...], nxt], axis=0)
        for gi, w in enumerate(POOL_WINDOWS):
            cols = slice(gi * C, (gi + 1) * C)
            acc = ext[:, cols] / counts[gi]
            shift = 1
            while shift < w:
                acc = acc + pltpu.roll(acc, tm + H - shift, axis=0)
                shift *= 2
            du_ref[:, cols] = (acc[:tm] - ext[:tm, cols]).astype(BF)

    row = pl.BlockSpec((tm, D), lambda m: (m, 0))
    return pl.pallas_call(
        body, grid=(S // tm,),
        in_specs=[row, pl.BlockSpec((H, D), lambda m: (jnp.minimum((m + 1) * (tm // H), S // H - 1), 0))],
        out_specs=row, out_shape=jax.ShapeDtypeStruct((S, D), BF), compiler_params=_params(1), name=name)(dp, dp)


ELEMENTWISE_TILE_BYTES = 1 << 20


def _row_tile(rows, row_bytes):
    for cand in (512, 256, 128, 64, 32, 16, 8):
        if rows % cand == 0 and cand * row_bytes <= ELEMENTWISE_TILE_BYTES:
            return cand
    return rows


def _adamw(w, g, m, v, name):
    shape = w.shape
    cols = shape[-1]
    rows = w.size // cols
    tr = _row_tile(rows, cols * 4)

    def body(w_ref, g_ref, m_ref, v_ref, d_ref, mo_ref, vo_ref):
        gv = g_ref[...]
        mn = ADAM_B1 * m_ref[...] + (1.0 - ADAM_B1) * gv
        vn = ADAM_B2 * v_ref[...] + (1.0 - ADAM_B2) * jnp.square(gv)
        m_hat = mn / (1.0 - ADAM_B1 ** ADAM_STEP)
        v_hat = vn / (1.0 - ADAM_B2 ** ADAM_STEP)
        d_ref[...] = -ADAM_LR * (m_hat / (jnp.sqrt(v_hat) + ADAM_EPS) + ADAM_WD * w_ref[...])
        mo_ref[...] = mn
        vo_ref[...] = vn

    tile = pl.BlockSpec((tr, cols), lambda i: (i, 0))
    out = jax.ShapeDtypeStruct((rows, cols), F32)
    res = pl.pallas_call(
        body, grid=(rows // tr,), in_specs=[tile] * 4, out_specs=[tile] * 3, out_shape=[out] * 3,
        compiler_params=_params(1), name=name)(*[t.reshape(rows, cols) for t in (w, g, m, v)])
    return [t.reshape(shape) for t in res]


def _sum_slots(r, name):
    n = r.shape[0]
    shape = r.shape[1:]
    cols = shape[-1]
    rows = r.size // (n * cols)
    tr = _row_tile(rows, n * cols * r.dtype.itemsize)

    def body(r_ref, o_ref):
        acc = r_ref[0].astype(F32)
        for d in range(1, n):
            acc = acc + r_ref[d].astype(F32)
        o_ref[...] = acc

    return pl.pallas_call(
        body, grid=(rows // tr,), in_specs=[pl.BlockSpec((n, tr, cols), lambda i: (0, i, 0))],
        out_specs=pl.BlockSpec((tr, cols), lambda i: (i, 0)), out_shape=jax.ShapeDtypeStruct((rows, cols), F32),
        compiler_params=_params(1), name=name)(r.reshape(n, rows, cols)).reshape(shape)


def _add_pair(a, b, name):
    shape = a.shape
    cols = shape[-1]
    rows = a.size // cols
    tr = _row_tile(rows, cols * 4)

    def body(a_ref, b_ref, o_ref):
        o_ref[...] = (a_ref[...].astype(F32) + b_ref[...].astype(F32)).astype(BF)

    tile = pl.BlockSpec((tr, cols), lambda i: (i, 0))
    return pl.pallas_call(
        body, grid=(rows // tr,), in_specs=[tile, tile], out_specs=tile, out_shape=jax.ShapeDtypeStruct((rows, cols), BF),
        compiler_params=_params(1), name=name)(a.reshape(rows, cols), b.reshape(rows, cols)).reshape(shape)


ANY = pl.BlockSpec(memory_space=pl.ANY)


def _position():
    return lax.axis_index("x"), lax.axis_index("y"), lax.axis_index("c")


def _gather_small(t, name):
    rows, cols = t.shape

    def body(t_ref, o_ref, send_sems, recv_sems):
        x, y, c = _position()
        me = 4 * x + 2 * y + c
        o_ref[me] = t_ref[...]
        copies = []
        for k in range(1, N_DEV):
            peer = (x ^ (k >> 2), y ^ ((k >> 1) & 1), c ^ (k & 1))
            cp = pltpu.make_async_remote_copy(src_ref=t_ref, dst_ref=o_ref.at[me], send_sem=send_sems.at[k - 1],
                                              recv_sem=recv_sems.at[k - 1], device_id=peer, device_id_type=MESH)
            cp.start()
            copies.append((cp, 4 * peer[0] + 2 * peer[1] + peer[2]))
        for k, (cp, src) in enumerate(copies):
            pltpu.make_async_remote_copy(src_ref=t_ref, dst_ref=o_ref.at[src], send_sem=send_sems.at[k], recv_sem=recv_sems.at[k],
                                         device_id=(x, y, c), device_id_type=MESH).wait_recv()
        for cp, _ in copies:
            cp.wait_send()

    return pl.pallas_call(
        body, in_specs=[pl.BlockSpec(memory_space=pltpu.VMEM)], out_specs=pl.BlockSpec(memory_space=pltpu.VMEM),
        out_shape=jax.ShapeDtypeStruct((N_DEV, rows, cols), F32),
        scratch_shapes=[pltpu.SemaphoreType.DMA((N_DEV - 1,)), pltpu.SemaphoreType.DMA((N_DEV - 1,))], name=name)(t)


def _other_chips(x, y):
    return [(1 - x, y), (x, 1 - y), (1 - x, 1 - y)]


def _remote(src, dst, send_sems, recv_sems, k, to):
    return pltpu.make_async_remote_copy(src_ref=src, dst_ref=dst, send_sem=send_sems.at[k], recv_sem=recv_sems.at[k],
                                        device_id=to, device_id_type=MESH)


def _gather_weights(shards, name):
    n = len(shards)

    def body(*refs):
        ins, outs = refs[:n], refs[n:2 * n]
        ici_send, ici_recv, d2d_send, d2d_recv = refs[2 * n:]
        x, y, c = _position()
        me, sibling, chip = (x, y, c), (x, y, 1 - c), 2 * x + y
        chips = _other_chips(x, y)
        halves = [s.shape[0] // 2 for s in shards]
        sends = []
        for p in range(n):
            mine = pl.ds(c * halves[p], halves[p])
            for j, (px, py) in enumerate(chips):
                sends.append(_remote(ins[p].at[mine], outs[p].at[mine, chip], ici_send, ici_recv, (p, j), (px, py, c)))
                sends[-1].start()
        for p in range(n):
            mine = pl.ds(c * halves[p], halves[p])
            for j, (px, py) in enumerate(chips):
                landed = outs[p].at[mine, 2 * px + py]
                _remote(ins[p].at[mine], landed, ici_send, ici_recv, (p, j), me).wait_recv()
                sends.append(_remote(landed, landed, d2d_send, d2d_recv, (p, j), sibling))
                sends[-1].start()
        for p in range(n):
            theirs = pl.ds((1 - c) * halves[p], halves[p])
            for j, (px, py) in enumerate(chips):
                passed = outs[p].at[theirs, 2 * px + py]
                _remote(passed, passed, d2d_send, d2d_recv, (p, j), me).wait_recv()
        for cp in sends:
            cp.wait_send()

    return pl.pallas_call(
        body, in_specs=[ANY] * n, out_specs=[ANY] * n,
        out_shape=[jax.ShapeDtypeStruct((s.shape[0], N_CHIPS) + s.shape[1:], s.dtype) for s in shards],
        scratch_shapes=[pltpu.SemaphoreType.DMA((n, 3))] * 4, name=name)(*shards)


def _swap_partials(grads, name):
    n = len(grads)

    def body(*refs):
        ins, outs = refs[:n], refs[n:2 * n]
        send_sems, recv_sems = refs[2 * n:]
        x, y, c = _position()
        copies = []
        for p in range(n):
            half = grads[p].shape[0] // 2
            copies.append(_remote(ins[p].at[pl.ds((1 - c) * half, half)], outs[p], send_sems, recv_sems, p, (x, y, 1 - c)))
            copies[-1].start()
        for cp in copies:
            cp.wait()

    return pl.pallas_call(
        body, in_specs=[ANY] * n, out_specs=[ANY] * n,
        out_shape=[jax.ShapeDtypeStruct((g.shape[0] // 2,) + g.shape[1:], g.dtype) for g in grads],
        scratch_shapes=[pltpu.SemaphoreType.DMA((n,))] * 2, name=name)(*grads)


def _scatter_sums(sums, name):
    n = len(sums)

    def body(*refs):
        ins, outs = refs[:n], refs[n:2 * n]
        send_sems, recv_sems = refs[2 * n:]
        x, y, c = _position()
        chip = 2 * x + y
        chips = _other_chips(x, y)
        sends = []
        for p in range(n):
            layers = pl.ds(0, sums[p].shape[0])
            for j, (px, py) in enumerate(chips):
                sends.append(_remote(ins[p].at[layers, 2 * px + py], outs[p].at[chip], send_sems, recv_sems, (p, j), (px, py, c)))
                sends[-1].start()
        for p in range(n):
            layers = pl.ds(0, sums[p].shape[0])
            for j, (px, py) in enumerate(chips):
                _remote(ins[p].at[layers, chip], outs[p].at[2 * px + py], send_sems, recv_sems, (p, j), (x, y, c)).wait_recv()
        for cp in sends:
            cp.wait_send()

    return pl.pallas_call(
        body, in_specs=[ANY] * n, out_specs=[ANY] * n,
        out_shape=[jax.ShapeDtypeStruct((N_CHIPS, s.shape[0]) + s.shape[2:], s.dtype) for s in sums],
        scratch_shapes=[pltpu.SemaphoreType.DMA((n, 3))] * 2, name=name)(*sums)


def _join_halves(halves, name):
    n = len(halves)
    flat = [h.reshape(-1, h.shape[-1]) for h in halves]

    def body(*refs):
        ins, outs = refs[:n], refs[n:2 * n]
        send_sems, recv_sems = refs[2 * n:]
        x, y, c = _position()
        copies = []
        for p in range(n):
            cp = pltpu.make_async_remote_copy(src_ref=ins[p], dst_ref=outs[p], send_sem=send_sems.at[p], recv_sem=recv_sems.at[p],
                                              device_id=(x, y, 1 - c), device_id_type=MESH)
            cp.start()
            copies.append(cp)
        for cp in copies:
            cp.wait()

    theirs = pl.pallas_call(
        body, in_specs=[ANY] * n, out_specs=[ANY] * n, out_shape=[jax.ShapeDtypeStruct(f.shape, f.dtype) for f in flat],
        scratch_shapes=[pltpu.SemaphoreType.DMA((n,)), pltpu.SemaphoreType.DMA((n,))], name=name)(*flat)
    south = lax.axis_index("c") == 0
    return [jnp.concatenate([jnp.where(south, h, t.reshape(h.shape)), jnp.where(south, t.reshape(h.shape), h)], axis=0)
            for h, t in zip(halves, theirs)]


TM = 512
TW = 2 * TM


def _row(v):
    return v.reshape(1, -1)


def _ffn_forward(x, gain, wgu, wdown, li, tag):
    h = _rmsnorm(x, gain, TM, f"norm_{tag}")
    g, u, act = _ffn_up(h, wgu, li, TM, f"ffn_up_{tag}")
    xo = _mm_fwd(act, wdown, li, x, 0.5, TM, f"ffn_down_{tag}")
    return xo, (x, h, g, u, act)


def _ffn_backward(dx, saved, gain, wgu, wdown, d_wgu, d_wdown, li, tag):
    x, h, g, u, act = saved
    dgu = _ffn_dact(dx, wdown, li, g, u, 256, f"ffn_dact_{tag}")
    d_wdown = _mm_wgrad(act, dx, 0.5, d_wdown, li, TW, f"ffn_dwdown_{tag}", tn=512)
    d_wgu = _mm_wgrad_cols(h, dgu, d_wgu, li, TW, f"ffn_dwgu_{tag}")
    dx, d_gain = _mm_dx_norm_cols(dgu, wgu, li, x, gain, dx, 256, f"ffn_dx_{tag}")
    return dx, d_gain, d_wgu, d_wdown


STACKED = ("w_ffn1_gu", "w_ffn1_down", "w_qkv", "w_o", "w_pool_in", "w_ffn2_gu", "w_ffn2_down", "w_ple_gate")


def _local_step(x, p, target, W):
    depth = p.shape[0]
    saved = []
    for i in range(depth):
        j = i // 2
        s = {}
        x, s["ffn1"] = _ffn_forward(x, W["norm_ffn1"][i], W["w_ffn1_gu"], W["w_ffn1_down"], i, f"a{i}")
        s["x_mix"] = x
        hm = _rmsnorm(x, W["norm_mix"][i], TM, f"norm_mix{i}")
        s["hm"] = hm
        if i % 2 == 0:
            qkv = _mm_cols(hm, W["w_qkv"], j, F32, TM, f"qkv{i}")
            qs, kn, vb = _qk_norm(qkv, W["q_norm"][j], W["k_norm"][j], TW, f"qk_norm{i}")
            o, tiles = _attn_fwd(qs, kn, vb, f"attn_fwd{i}")
            x = _mm_fwd(o, W["w_o"], j, x, 1.0, TM, f"attn_out{i}")
            s["mix"] = (qkv, qs, kn, vb, o, tiles)
        else:
            u = _mm_plain(hm, W["w_pool_in"], j, F32, TM, f"pool_in{i}")
            x, pooled = _pool_fwd(u, W["w_pool_grp"][j], W["pool_scale"][j], x, TM, f"pool_fwd{i}")
            s["mix"] = (pooled,)
        x, s["ffn2"] = _ffn_forward(x, W["norm_ffn2"][i], W["w_ffn2_gu"], W["w_ffn2_down"], i, f"b{i}")
        s["x_ple"] = x
        x, hp, gp, pe = _ple_fwd(x, W["norm_ple"][i], W["w_ple_gate"], i, p[i], W["w_ple_proj"][i], TM, f"ple_fwd{i}")
        s["ple"] = (hp, gp, pe)
        saved.append(s)

    sq, dx = _loss_head(x, target, TM, "loss_head")
    G = {k: (lax.empty(v.shape, BF) if k in STACKED else [None] * len(v)) for k, v in W.items()}
    for i in reversed(range(depth)):
        j = i // 2
        s = saved[i]
        hp, gp, pe = s["ple"]
        dx, dgp, dpe, G["norm_ple"][i] = _ple_bwd(dx, gp, pe, W["w_ple_gate"], i, s["x_ple"], W["norm_ple"][i], TM, f"ple_bwd{i}")
        G["w_ple_gate"] = _mm_wgrad(hp, dgp, 1.0, G["w_ple_gate"], i, TW, f"ple_dwgate{i}")
        G["w_ple_proj"][i] = _mm_wgrad(p[i], dpe, 1.0, lax.empty((1,) + W["w_ple_proj"][i].shape, BF), 0, TW, f"ple_dwproj{i}")[0]
        dx, G["norm_ffn2"][i], G["w_ffn2_gu"], G["w_ffn2_down"] = _ffn_backward(
            dx, s["ffn2"], W["norm_ffn2"][i], W["w_ffn2_gu"], W["w_ffn2_down"], G["w_ffn2_gu"], G["w_ffn2_down"], i, f"b{i}")
        hm = s["hm"]
        if i % 2 == 0:
            qkv, qs, kn, vb, o, tiles = s["mix"]
            G["w_o"] = _mm_wgrad(o, dx, 1.0, G["w_o"], j, TW, f"attn_dwo{i}")
            do = _mm_plain(dx, W["w_o"], j, BF, TM, f"attn_do{i}", dims=NT)
            dqs, dkn, dv = _attn_bwd(qs, kn, vb, do, tiles, f"attn_bwd{i}")
            dq, dk, dvb, dqg, dkg = _qk_norm_bwd(qkv, dqs, dkn, dv, W["q_norm"][j], W["k_norm"][j], TW, f"qk_norm_bwd{i}")
            dqkv = jnp.concatenate([dq, dk, dvb], axis=1)
            G["q_norm"][j] = dqg[:, :HEAD_DIM] + dqg[:, HEAD_DIM:]
            G["k_norm"][j] = dkg[:, :HEAD_DIM] + dkg[:, HEAD_DIM:]
            G["w_qkv"] = _mm_wgrad_cols(hm, dqkv, G["w_qkv"], j, TW, f"attn_dwqkv{i}")
            dx, G["norm_mix"][i] = _mm_dx_norm_cols(dqkv, W["w_qkv"], j, s["x_mix"], W["norm_mix"][i], dx, TM, f"attn_dx{i}")
        else:
            (pooled,) = s["mix"]
            dp, G["w_pool_grp"][j], G["pool_scale"][j] = _pool_bwd_grp(dx, pooled, W["w_pool_grp"][j], W["pool_scale"][j], TM, f"pool_bwd_grp{i}")
            du = _pool_bwd_window(dp, TM, f"pool_bwd_win{i}")
            G["w_pool_in"] = _mm_wgrad(hm, du, 1.0, G["w_pool_in"], j, TW, f"pool_dwin{i}")
            dx, G["norm_mix"][i] = _mm_dx_norm(du, W["w_pool_in"], j, s["x_mix"], W["norm_mix"][i], dx, TM, f"pool_dx{i}")
        dx, G["norm_ffn1"][i], G["w_ffn1_gu"], G["w_ffn1_down"] = _ffn_backward(
            dx, s["ffn1"], W["norm_ffn1"][i], W["w_ffn1_gu"], W["w_ffn1_down"], G["w_ffn1_gu"], G["w_ffn1_down"], i, f"a{i}")
    return sq, dx, G


SHARDED = ("w_ffn1_gu", "w_ffn1_down", "w_qkv", "w_o", "w_pool_in", "w_pool_grp", "w_ffn2_gu", "w_ffn2_down", "w_ple_gate", "w_ple_proj")
COLUMN_SHARDED = ("w_ffn1_gu", "w_qkv", "w_ffn2_gu", "w_ple_proj")
NORMS = ("norm_ffn1", "norm_mix", "norm_ffn2", "norm_ple")
HEAD_GAINS = ("q_norm", "k_norm")
WEIGHTS = ("norm_ffn1", "w_ffn1_gu", "w_ffn1_down", "norm_mix", "w_qkv", "q_norm", "k_norm", "w_o", "w_pool_in", "w_pool_grp",
           "pool_scale", "norm_ffn2", "w_ffn2_gu", "w_ffn2_down", "norm_ple", "w_ple_gate", "w_ple_proj")
SMALL_ROWS = 24


def _whole_weights(w, gathered, pool_scale_all):
    W = {}
    for k in NORMS:
        W[k] = [_row(w[k][i]) for i in range(w[k].shape[0])]
    for k in HEAD_GAINS:
        W[k] = [_row(jnp.tile(w[k][j], LANES // HEAD_DIM)) for j in range(w[k].shape[0])]
    for k in SHARDED:
        g = gathered[k]
        L = g.shape[0]
        if k == "w_pool_grp":
            W[k] = [jnp.transpose(g[i], (1, 0, 2, 3)).reshape(g.shape[2], -1, g.shape[4]) for i in range(L)]
        elif k == "w_ple_proj":
            W[k] = [jnp.transpose(g[i], (1, 0, 2)).reshape(g.shape[2], -1) for i in range(L)]
        elif k in COLUMN_SHARDED:
            W[k] = g
        else:
            W[k] = g.reshape(L, -1, g.shape[3])
    W["pool_scale"] = [_row(pool_scale_all[j]) for j in range(pool_scale_all.shape[0])]
    return W


def _shard_major(k, g):
    if k == "w_pool_grp":
        return jnp.stack([jnp.transpose(t.reshape(t.shape[0], N_CHIPS, t.shape[1] // N_CHIPS, t.shape[2]), (1, 0, 2, 3)) for t in g])
    if k == "w_ple_proj":
        return jnp.stack([jnp.transpose(t.reshape(t.shape[0], N_CHIPS, t.shape[1] // N_CHIPS), (1, 0, 2)) for t in g])
    if k in COLUMN_SHARDED:
        return g
    return g.reshape(g.shape[0], N_CHIPS, g.shape[1] // N_CHIPS, g.shape[2])


def kernel(x, p, norm_ffn1, w_ffn1_gu, w_ffn1_down, norm_mix, w_qkv, q_norm, k_norm, w_o, w_pool_in, w_pool_grp, pool_scale, norm_ffn2, w_ffn2_gu, w_ffn2_down, norm_ple, w_ple_gate, w_ple_proj, loss_target, m_norm_ffn1, m_w_ffn1_gu, m_w_ffn1_down, m_norm_mix, m_w_qkv, m_q_norm, m_k_norm, m_w_o, m_w_pool_in, m_w_pool_grp, m_pool_scale, m_norm_ffn2, m_w_ffn2_gu, m_w_ffn2_down, m_norm_ple, m_w_ple_gate, m_w_ple_proj, v_norm_ffn1, v_w_ffn1_gu, v_w_ffn1_down, v_norm_mix, v_w_qkv, v_q_norm, v_k_norm, v_w_o, v_w_pool_in, v_w_pool_grp, v_pool_scale, v_norm_ffn2, v_w_ffn2_gu, v_w_ffn2_down, v_norm_ple, v_w_ple_gate, v_w_ple_proj):
    w = dict(norm_ffn1=norm_ffn1, w_ffn1_gu=w_ffn1_gu, w_ffn1_down=w_ffn1_down, norm_mix=norm_mix, w_qkv=w_qkv, q_norm=q_norm,
             k_norm=k_norm, w_o=w_o, w_pool_in=w_pool_in, w_pool_grp=w_pool_grp, pool_scale=pool_scale, norm_ffn2=norm_ffn2,
             w_ffn2_gu=w_ffn2_gu, w_ffn2_down=w_ffn2_down, norm_ple=norm_ple, w_ple_gate=w_ple_gate, w_ple_proj=w_ple_proj)
    m = dict(norm_ffn1=m_norm_ffn1, w_ffn1_gu=m_w_ffn1_gu, w_ffn1_down=m_w_ffn1_down, norm_mix=m_norm_mix, w_qkv=m_w_qkv,
             q_norm=m_q_norm, k_norm=m_k_norm, w_o=m_w_o, w_pool_in=m_w_pool_in, w_pool_grp=m_w_pool_grp, pool_scale=m_pool_scale,
             norm_ffn2=m_norm_ffn2, w_ffn2_gu=m_w_ffn2_gu, w_ffn2_down=m_w_ffn2_down, norm_ple=m_norm_ple, w_ple_gate=m_w_ple_gate,
             w_ple_proj=m_w_ple_proj)
    v = dict(norm_ffn1=v_norm_ffn1, w_ffn1_gu=v_w_ffn1_gu, w_ffn1_down=v_w_ffn1_down, norm_mix=v_norm_mix, w_qkv=v_w_qkv,
             q_norm=v_q_norm, k_norm=v_k_norm, w_o=v_w_o, w_pool_in=v_w_pool_in, w_pool_grp=v_w_pool_grp, pool_scale=v_pool_scale,
             norm_ffn2=v_norm_ffn2, w_ffn2_gu=v_w_ffn2_gu, w_ffn2_down=v_w_ffn2_down, norm_ple=v_norm_ple, w_ple_gate=v_w_ple_gate,
             w_ple_proj=v_w_ple_proj)
    chip = 2 * lax.axis_index("x") + lax.axis_index("y")
    D = x.shape[-1]
    shard_cols = pool_scale.shape[1]

    core = lax.axis_index("c")
    shards = [w[k].astype(BF) for k in SHARDED]
    gathered = {k: lax.dynamic_update_slice_in_dim(g, s[:, None], chip, axis=1)
                for k, s, g in zip(SHARDED, shards, _gather_weights(shards, "gather_weights"))}
    scale_rows = jnp.zeros((8, shard_cols), F32).at[:pool_scale.shape[0]].set(pool_scale)
    scale_all = _gather_small(scale_rows, "gather_pool_scale")
    pool_scale_all = jnp.transpose(scale_all[::2, :pool_scale.shape[0]], (1, 0, 2)).reshape(pool_scale.shape[0], D)
    W = _whole_weights(w, gathered, pool_scale_all)

    sq, dx, G = _local_step(x[0], p[:, 0], loss_target[0], W)
    loss = lax.psum(0.5 / D * sq[0, 0], ("x", "y", "c"))

    partials = [_shard_major(k, G[k]) for k in SHARDED]
    theirs = _swap_partials(partials, "swap_partials")
    chip_sums = [_add_pair(lax.dynamic_slice_in_dim(g, core * t.shape[0], t.shape[0], axis=0), t, f"pair_{k}")
                 for k, g, t in zip(SHARDED, partials, theirs)]
    slots = [lax.dynamic_update_slice_in_dim(r, lax.dynamic_index_in_dim(s, chip, axis=1, keepdims=False)[None], chip, axis=0)
             for s, r in zip(chip_sums, _scatter_sums(chip_sums, "scatter_sums"))]
    halves = [_sum_slots(s, f"sum_{k}") for k, s in zip(SHARDED, slots)]
    grads = dict(zip(SHARDED, _join_halves(halves, "join_halves")))

    small = [G[k][i] for k in NORMS for i in range(len(G[k]))]
    small += [jnp.pad(jnp.concatenate(G[k], axis=1), ((0, 0), (0, D - len(G[k]) * HEAD_DIM))) for k in HEAD_GAINS]
    small += G["pool_scale"]
    small = jnp.concatenate(small + [jnp.zeros((SMALL_ROWS - len(small), D), F32)], axis=0)
    small = _sum_slots(_gather_small(small, "gather_small_grads"), "sum_small_grads")
    row = 0
    for k in NORMS:
        grads[k] = small[row:row + w[k].shape[0]]
        row += w[k].shape[0]
    for k in HEAD_GAINS:
        grads[k] = small[row, :w[k].size].reshape(w[k].shape)
        row += 1
    grads["pool_scale"] = lax.dynamic_slice_in_dim(small[row:row + pool_scale.shape[0]], chip * shard_cols, shard_cols, axis=1)

    delta, new_m, new_v = {}, {}, {}
    small_names = NORMS + HEAD_GAINS + ("pool_scale",)

    def pack(d):
        rows = [jnp.pad(d[k].reshape(-1, d[k].shape[-1]) if k in NORMS + ("pool_scale",) else d[k].reshape(1, -1),
                        ((0, 0), (0, D - (d[k].shape[-1] if k in NORMS + ("pool_scale",) else d[k].size))), constant_values=1.0)
                for k in small_names]
        n = sum(r.shape[0] for r in rows)
        return jnp.concatenate(rows + [jnp.ones((SMALL_ROWS - n, D), F32)], axis=0)

    packed = _adamw(pack(w), pack(grads), pack(m), pack(v), "adamw_small")
    row = 0
    for k in small_names:
        n = w[k].shape[0] if k in NORMS + ("pool_scale",) else 1
        width = w[k].shape[-1] if k in NORMS + ("pool_scale",) else w[k].size
        for dst, src in zip((delta, new_m, new_v), packed):
            dst[k] = src[row:row + n, :width].reshape(w[k].shape)
        row += n
    for k in SHARDED:
        delta[k], new_m[k], new_v[k] = _adamw(w[k], grads[k], m[k], v[k], f"adamw_{k}")

    return (loss, dx[None], *[grads[k] for k in WEIGHTS], *[delta[k] for k in WEIGHTS],
            *[new_m[k] for k in WEIGHTS], *[new_v[k] for k in WEIGHTS])
```

```python
import jax
import jax.numpy as jnp
from jax import lax
from jax.experimental import pallas as pl
from jax.experimental.pallas import tpu as pltpu

BF = jnp.bfloat16
F32 = jnp.float32

N_HEADS = 16
HEAD_DIM = 64
POOL_WINDOWS = (2, 4, 8, 16)
POOL_HALO = 16
EPS = 1e-6
ADAM_LR = 0.001
ADAM_B1 = 0.9
ADAM_B2 = 0.999
ADAM_EPS = 1e-08
ADAM_WD = 0.01
ADAM_STEP = 10

N_CHIPS = 4
N_DEV = 8
LANES = 128
VMEM_LIMIT = 56 * 1024 * 1024
ATT_TK = 256
ATT_TQ = 1024
ATT_SLOTS = 3
MESH = pl.DeviceIdType.MESH

NN = (((1,), (0,)), ((), ()))
NT = (((1,), (1,)), ((), ()))
TN = (((0,), (0,)), ((), ()))


def _params(n_axes):
    return pltpu.CompilerParams(dimension_semantics=("arbitrary",) * n_axes, vmem_limit_bytes=VMEM_LIMIT)


def _dot(a, b, dims):
    return lax.dot_general(a, b, dims, preferred_element_type=F32)


def _layer(li, block, index_map):
    return pl.BlockSpec((None,) + tuple(block), lambda *g: (li,) + tuple(index_map(*g)))


def _mm(a, b, extra, out_shapes, *, dims, grid, a_spec, b_spec, extra_specs, out_specs, acc_shape, epilogue, name, into=None):
    nk = grid[-1]
    aliases = {}
    if into is not None:
        aliases = {2 + len(extra): 0}
        extra = [*extra, into]
        extra_specs = [*extra_specs, pl.BlockSpec(memory_space=pl.ANY)]
        out_shapes = [jax.ShapeDtypeStruct(into.shape, into.dtype), *out_shapes[1:]]
    n_extra = len(extra)
    n_out = len(out_shapes)

    def body(a_ref, b_ref, *rest):
        ex = rest[:n_extra]
        outs = rest[n_extra:n_extra + n_out]
        acc = rest[-1]
        k = pl.program_id(len(grid) - 1)

        @pl.when(k == 0)
        def _():
            acc[...] = jnp.zeros_like(acc)

        acc[...] += _dot(a_ref[...].astype(BF), b_ref[...].astype(BF), dims)
        first = pl.program_id(0) == 0

        @pl.when(k == nk - 1)
        def _():
            epilogue(acc[...], ex, outs, first)

    return pl.pallas_call(
        body, grid=grid, in_specs=[a_spec, b_spec, *extra_specs], out_specs=out_specs, out_shape=out_shapes,
        scratch_shapes=[pltpu.VMEM(acc_shape, F32)], compiler_params=_params(len(grid)), name=name,
        input_output_aliases=aliases,
    )(a, b, *extra)


def _store(dtype, scale=1.0):
    def ep(acc, ex, outs, first):
        outs[0][...] = (acc * scale).astype(dtype)
    return ep


def _residual(scale):
    def ep(acc, ex, outs, first):
        outs[0][...] = ex[0][...] + scale * acc
    return ep


def _rms_bwd_epilogue(acc, ex, outs, first):
    x_ref, g_ref, dx_ref = ex
    dxo_ref, dg_ref = outs
    x = x_ref[...]
    r = lax.rsqrt(jnp.mean(x * x, axis=-1, keepdims=True) + EPS)
    xh = x * r
    dxh = acc * g_ref[...]
    dxo_ref[...] = dx_ref[...] + r * (dxh - xh * jnp.mean(dxh * xh, axis=-1, keepdims=True))

    @pl.when(first)
    def _():
        dg_ref[...] = jnp.zeros_like(dg_ref)

    dg_ref[...] += jnp.sum(acc * xh, axis=0, keepdims=True)


def _mm_fwd(a, w, li, x, scale, tm, name):
    S, K = a.shape
    N = w.shape[2]
    return _mm(a, w, [x], [jax.ShapeDtypeStruct((S, N), F32)], dims=NN, grid=(S // tm, 1),
               a_spec=pl.BlockSpec((tm, K), lambda m, k: (m, 0)), b_spec=_layer(li, (K, N), lambda m, k: (0, 0)),
               extra_specs=[pl.BlockSpec((tm, N), lambda m, k: (m, 0))], out_specs=[pl.BlockSpec((tm, N), lambda m, k: (m, 0))],
               acc_shape=(tm, N), epilogue=_residual(scale), name=name)[0]


def _mm_plain(a, w, li, dtype, tm, name, dims=NN):
    S, K = a.shape
    N = w.shape[2] if dims == NN else w.shape[1]
    return _mm(a, w, [], [jax.ShapeDtypeStruct((S, N), dtype)], dims=dims, grid=(S // tm, 1),
               a_spec=pl.BlockSpec((tm, K), lambda m, k: (m, 0)), b_spec=_layer(li, w.shape[1:], lambda m, k: (0, 0)),
               extra_specs=[], out_specs=[pl.BlockSpec((tm, N), lambda m, k: (m, 0))],
               acc_shape=(tm, N), epilogue=_store(dtype), name=name)[0]


def _mm_cols(a, w, li, dtype, tm, name):
    S, K = a.shape
    _, nj, _, ns = w.shape

    def body(a_ref, w_ref, o_ref):
        av = a_ref[...]
        for j in range(nj):
            o_ref[:, j * ns:(j + 1) * ns] = _dot(av, w_ref[j], NN).astype(dtype)

    return pl.pallas_call(
        body, grid=(S // tm,), in_specs=[pl.BlockSpec((tm, K), lambda m: (m, 0)), _layer(li, (nj, K, ns), lambda m: (0, 0, 0))],
        out_specs=pl.BlockSpec((tm, nj * ns), lambda m: (m, 0)), out_shape=jax.ShapeDtypeStruct((S, nj * ns), dtype),
        compiler_params=_params(1), name=name)(a, w)


def _mm_wgrad(a, b, scale, into, li, tk, name, tn=None):
    S, M = a.shape
    N = b.shape[1]
    tn = N if tn is None else tn
    return _mm(a, b, [], [None], dims=TN, grid=(N // tn, S // tk),
               a_spec=pl.BlockSpec((tk, M), lambda n, k: (k, 0)), b_spec=pl.BlockSpec((tk, tn), lambda n, k: (k, n)),
               extra_specs=[], out_specs=[_layer(li, (M, tn), lambda n, k: (0, n))],
               acc_shape=(M, tn), epilogue=_store(BF, scale), name=name, into=into)[0]


def _mm_wgrad_cols(a, b, into, li, tk, name):
    S, M = a.shape
    _, nj, _, ns = into.shape
    return _mm(a, b, [], [None], dims=TN, grid=(nj, S // tk),
               a_spec=pl.BlockSpec((tk, M), lambda j, k: (k, 0)), b_spec=pl.BlockSpec((tk, ns), lambda j, k: (k, j)),
               extra_specs=[], out_specs=[_layer(li, (None, M, ns), lambda j, k: (j, 0, 0))],
               acc_shape=(M, ns), epilogue=_store(BF), name=name, into=into)[0]


def _mm_dx_norm(a, w, li, x, gain, dx, tm, name):
    S, K = a.shape
    N = w.shape[1]
    row = pl.BlockSpec((tm, N), lambda m, k: (m, 0))
    vec = pl.BlockSpec((1, N), lambda m, k: (0, 0))
    return _mm(a, w, [x, gain, dx], [jax.ShapeDtypeStruct((S, N), F32), jax.ShapeDtypeStruct((1, N), F32)], dims=NT,
               grid=(S // tm, 1), a_spec=pl.BlockSpec((tm, K), lambda m, k: (m, 0)), b_spec=_layer(li, (N, K), lambda m, k: (0, 0)),
               extra_specs=[row, vec, row], out_specs=[row, vec], acc_shape=(tm, N), epilogue=_rms_bwd_epilogue, name=name)


def _mm_dx_norm_cols(a, w, li, x, gain, dx, tm, name):
    S = a.shape[0]
    _, nj, N, ks = w.shape

    def body(a_ref, w_ref, x_ref, g_ref, dx_ref, dxo_ref, dg_ref):
        acc = _dot(a_ref[:, :ks], w_ref[0], NT)
        for j in range(1, nj):
            acc = acc + _dot(a_ref[:, j * ks:(j + 1) * ks], w_ref[j], NT)
        _rms_bwd_epilogue(acc, (x_ref, g_ref, dx_ref), (dxo_ref, dg_ref), pl.program_id(0) == 0)

    row = pl.BlockSpec((tm, N), lambda m: (m, 0))
    vec = pl.BlockSpec((1, N), lambda m: (0, 0))
    return pl.pallas_call(
        body, grid=(S // tm,),
        in_specs=[pl.BlockSpec((tm, nj * ks), lambda m: (m, 0)), _layer(li, (nj, N, ks), lambda m: (0, 0, 0)), row, vec, row],
        out_specs=[row, vec], out_shape=[jax.ShapeDtypeStruct((S, N), F32), jax.ShapeDtypeStruct((1, N), F32)],
        compiler_params=_params(1), name=name)(a, w, x, gain, dx)


def _rmsnorm(x, gain, tm, name):
    S, D = x.shape

    def body(x_ref, g_ref, h_ref):
        xv = x_ref[...]
        r = lax.rsqrt(jnp.mean(xv * xv, axis=-1, keepdims=True) + EPS)
        h_ref[...] = (xv * r * g_ref[...]).astype(BF)

    return pl.pallas_call(
        body, grid=(S // tm,), in_specs=[pl.BlockSpec((tm, D), lambda m: (m, 0)), pl.BlockSpec((1, D), lambda m: (0, 0))],
        out_specs=pl.BlockSpec((tm, D), lambda m: (m, 0)), out_shape=jax.ShapeDtypeStruct((S, D), BF),
        compiler_params=_params(1), name=name)(x, gain)


def _ffn_up(h, wgu, li, tm, name):
    S, D = h.shape
    ns = wgu.shape[3]
    half = wgu.shape[1] // 2

    def body(h_ref, wg_ref, wu_ref, g_ref, u_ref, act_ref):
        hv = h_ref[...]
        g = _dot(hv, wg_ref[...], NN)
        u = _dot(hv, wu_ref[...], NN)
        g_ref[...] = g.astype(BF)
        u_ref[...] = u.astype(BF)
        act_ref[...] = (g * jax.nn.sigmoid(g) * u).astype(BF)

    out = jax.ShapeDtypeStruct((S, half * ns), BF)
    tile = pl.BlockSpec((tm, ns), lambda j, m: (m, j))
    return pl.pallas_call(
        body, grid=(half, S // tm),
        in_specs=[pl.BlockSpec((tm, D), lambda j, m: (m, 0)), _layer(li, (None, D, ns), lambda j, m: (j, 0, 0)),
                  _layer(li, (None, D, ns), lambda j, m: (j + half, 0, 0))],
        out_specs=[tile, tile, tile], out_shape=[out, out, out], compiler_params=_params(2), name=name)(h, wgu, wgu)


def _ffn_dact(dx, wdown, li, g, u, tm, name):
    S, D = dx.shape
    F = wdown.shape[1]

    def ep(acc, ex, outs, first):
        gv = ex[0][...].astype(F32)
        uv = ex[1][...].astype(F32)
        da = 0.5 * acc
        sg = jax.nn.sigmoid(gv)
        outs[0][:, :F] = (da * uv * (sg * (1.0 + gv * (1.0 - sg)))).astype(BF)
        outs[0][:, F:] = (da * (gv * sg)).astype(BF)

    row = pl.BlockSpec((tm, F), lambda m, k: (m, 0))
    return _mm(dx, wdown, [g, u], [jax.ShapeDtypeStruct((S, 2 * F), BF)], dims=NT, grid=(S // tm, 1),
               a_spec=pl.BlockSpec((tm, D), lambda m, k: (m, 0)), b_spec=_layer(li, (F, D), lambda m, k: (0, 0)),
               extra_specs=[row, row], out_specs=[pl.BlockSpec((tm, 2 * F), lambda m, k: (m, 0))],
               acc_shape=(tm, F), epilogue=ep, name=name)[0]


def _ple_fwd(x, gain, wgate, li, p, wproj, tm, name):
    S, D = x.shape
    P = p.shape[1]

    def body(x_ref, g_ref, wg_ref, p_ref, wp_ref, xo_ref, hp_ref, gp_ref, pe_ref):
        xv = x_ref[...]
        r = lax.rsqrt(jnp.mean(xv * xv, axis=-1, keepdims=True) + EPS)
        hp = (xv * r * g_ref[...]).astype(BF)
        gp = _dot(hp, wg_ref[...], NN)
        pe = _dot(p_ref[...].astype(BF), wp_ref[...], NN)
        xo_ref[...] = xv + jax.nn.sigmoid(gp) * pe
        hp_ref[...] = hp
        gp_ref[...] = gp.astype(BF)
        pe_ref[...] = pe.astype(BF)

    row = pl.BlockSpec((tm, D), lambda m: (m, 0))
    return pl.pallas_call(
        body, grid=(S // tm,),
        in_specs=[row, pl.BlockSpec((1, D), lambda m: (0, 0)), _layer(li, (D, D), lambda m: (0, 0)),
                  pl.BlockSpec((tm, P), lambda m: (m, 0)), pl.BlockSpec((P, D), lambda m: (0, 0))],
        out_specs=[row, row, row, row],
        out_shape=[jax.ShapeDtypeStruct((S, D), F32)] + [jax.ShapeDtypeStruct((S, D), BF)] * 3,
        compiler_params=_params(1), name=name)(x, gain, wgate, p, wproj)


def _ple_bwd(dx, gp, pe, wgate, li, x, gain, tm, name):
    S, D = dx.shape

    def body(dx_ref, gp_ref, pe_ref, wg_ref, x_ref, g_ref, dxo_ref, dgp_ref, dpe_ref, dg_ref):
        dxv = dx_ref[...]
        sg = jax.nn.sigmoid(gp_ref[...].astype(F32))
        dgp = (dxv * pe_ref[...].astype(F32) * (sg * (1.0 - sg))).astype(BF)
        dgp_ref[...] = dgp
        dpe_ref[...] = (dxv * sg).astype(BF)
        dhp = _dot(dgp, wg_ref[...], NT)
        _rms_bwd_epilogue(dhp, (x_ref, g_ref, dx_ref), (dxo_ref, dg_ref), pl.program_id(0) == 0)

    row = pl.BlockSpec((tm, D), lambda m: (m, 0))
    vec = pl.BlockSpec((1, D), lambda m: (0, 0))
    return pl.pallas_call(
        body, grid=(S // tm,), in_specs=[row, row, row, _layer(li, (D, D), lambda m: (0, 0)), row, vec],
        out_specs=[row, row, row, vec],
        out_shape=[jax.ShapeDtypeStruct((S, D), F32), jax.ShapeDtypeStruct((S, D), BF), jax.ShapeDtypeStruct((S, D), BF),
                   jax.ShapeDtypeStruct((1, D), F32)],
        compiler_params=_params(1), name=name)(dx, gp, pe, wgate, x, gain)


def _loss_head(y, target, tm, name):
    S, D = y.shape

    def body(y_ref, t_ref, sq_ref, dy_ref):
        d = y_ref[...] - t_ref[...]
        dy_ref[...] = d * (1.0 / D)

        @pl.when(pl.program_id(0) == 0)
        def _():
            sq_ref[...] = jnp.zeros_like(sq_ref)

        sq_ref[...] += jnp.sum(d * d, keepdims=True)

    row = pl.BlockSpec((tm, D), lambda m: (m, 0))
    return pl.pallas_call(
        body, grid=(S // tm,), in_specs=[row, row], out_specs=[pl.BlockSpec((1, 1), lambda m: (0, 0)), row],
        out_shape=[jax.ShapeDtypeStruct((1, 1), F32), jax.ShapeDtypeStruct((S, D), F32)],
        compiler_params=_params(1), name=name)(y, target)


def _qk_norm(qkv, qgain, kgain, tm, name):
    S = qkv.shape[0]
    D = qkv.shape[1] // 3
    nb = D // LANES

    def norm2(t, gain):
        lo = lax.broadcasted_iota(jnp.int32, t.shape, 1) < HEAD_DIM
        sq = t * t
        s_lo = jnp.sum(jnp.where(lo, sq, 0.0), axis=1, keepdims=True)
        s_hi = jnp.sum(jnp.where(lo, 0.0, sq), axis=1, keepdims=True)
        r = lax.rsqrt(jnp.where(lo, s_lo, s_hi) * (1.0 / HEAD_DIM) + EPS)
        return t * r * gain

    def body(q_ref, k_ref, v_ref, qg_ref, kg_ref, qo_ref, ko_ref, vo_ref):
        qo_ref[...] = (norm2(q_ref[...], qg_ref[...]) * (HEAD_DIM ** -0.5)).astype(BF)
        ko_ref[...] = norm2(k_ref[...], kg_ref[...]).astype(BF)
        vo_ref[...] = v_ref[...].astype(BF)

    vec = pl.BlockSpec((1, LANES), lambda m, h: (0, 0))
    tile = pl.BlockSpec((tm, LANES), lambda m, h: (m, h))
    out = jax.ShapeDtypeStruct((S, D), BF)
    return pl.pallas_call(
        body, grid=(S // tm, nb),
        in_specs=[tile, pl.BlockSpec((tm, LANES), lambda m, h: (m, nb + h)), pl.BlockSpec((tm, LANES), lambda m, h: (m, 2 * nb + h)), vec, vec],
        out_specs=[tile, tile, tile], out_shape=[out, out, out], compiler_params=_params(2), name=name)(qkv, qkv, qkv, qgain, kgain)


def _qk_norm_bwd(qkv, dqs, dkn, dv, qgain, kgain, tm, name):
    S = qkv.shape[0]
    D = qkv.shape[1] // 3
    nb = D // LANES

    def norm2_bwd(t, gain, dn):
        lo = lax.broadcasted_iota(jnp.int32, t.shape, 1) < HEAD_DIM

        def headsum(val):
            s_lo = jnp.sum(jnp.where(lo, val, 0.0), axis=1, keepdims=True)
            s_hi = jnp.sum(jnp.where(lo, 0.0, val), axis=1, keepdims=True)
            return jnp.where(lo, s_lo, s_hi)

        r = lax.rsqrt(headsum(t * t) * (1.0 / HEAD_DIM) + EPS)
        th = t * r
        dth = dn * gain
        dt = r * (dth - th * (headsum(dth * th) * (1.0 / HEAD_DIM)))
        return dt, jnp.sum(dn * th, axis=0, keepdims=True)

    def body(q_ref, k_ref, dq_ref, dk_ref, dv_ref, qg_ref, kg_ref, dqo_ref, dko_ref, dvo_ref, dqg_ref, dkg_ref):
        dq, dqg = norm2_bwd(q_ref[...], qg_ref[...], dq_ref[...] * (HEAD_DIM ** -0.5))
        dk, dkg = norm2_bwd(k_ref[...], kg_ref[...], dk_ref[...])
        dqo_ref[...] = dq.astype(BF)
        dko_ref[...] = dk.astype(BF)
        dvo_ref[...] = dv_ref[...].astype(BF)

        @pl.when((pl.program_id(0) == 0) & (pl.program_id(1) == 0))
        def _():
            dqg_ref[...] = jnp.zeros_like(dqg_ref)
            dkg_ref[...] = jnp.zeros_like(dkg_ref)

        dqg_ref[...] += dqg
        dkg_ref[...] += dkg

    vec = pl.BlockSpec((1, LANES), lambda m, h: (0, 0))
    tile = pl.BlockSpec((tm, LANES), lambda m, h: (m, h))
    return pl.pallas_call(
        body, grid=(S // tm, nb),
        in_specs=[tile, pl.BlockSpec((tm, LANES), lambda m, h: (m, nb + h)), tile, tile, tile, vec, vec],
        out_specs=[tile, tile, tile, vec, vec],
        out_shape=[jax.ShapeDtypeStruct((S, D), BF)] * 3 + [jax.ShapeDtypeStruct((1, LANES), F32)] * 2,
        compiler_params=_params(2), name=name)(qkv, qkv, dqs, dkn, dv, qgain, kgain)


HEADS_PER_BLOCK = LANES // HEAD_DIM


def _key_order(tk, left):
    j = lax.broadcasted_iota(jnp.int32, (tk, tk), 0)
    s = lax.broadcasted_iota(jnp.int32, (tk, tk), 1)
    return (j < s if left else j > s).astype(BF)


def _attn_tiles(qs, ks, stays, valid, after):
    zs = [_dot(q, k, NT) for q, k in zip(qs, ks)]
    sps = [jnp.maximum(z, 0.0) + jnp.log(1.0 + jnp.exp(-jnp.abs(z))) for z in zs]
    if valid is not None:
        sps = [jnp.where(valid, sp, 0.0) for sp in sps]
    rights = [_dot(sp.astype(BF), after, NN) for sp in sps]
    ws = [jnp.exp((z - sp) - r + stay) for z, sp, r, stay in zip(zs, sps, rights, stays)]
    if valid is not None:
        ws = [jnp.where(valid, w, 0.0) for w in ws]
    return ws, [r[:, :1] + sp[:, :1] for r, sp in zip(rights, sps)]


def _below(r0, t):
    return t if r0 == 0 else t[r0:]


def _with_below(r0, t, part):
    return part if r0 == 0 else jnp.concatenate([t[:r0], part], axis=0)


def _diagonal_mask(tq, tk, r0):
    return lax.broadcasted_iota(jnp.int32, (tq - r0, tk), 1) < lax.broadcasted_iota(jnp.int32, (tq - r0, tk), 0)


def _attn_walk(qi, tq, tk, block, carry):
    n_diag = tq // tk
    for n, d in enumerate(reversed(range(n_diag))):
        carry = block(n, n_diag * qi + d, d * tk, _diagonal_mask(tq, tk, d * tk), carry)
    return lax.fori_loop(0, n_diag * qi, lambda i, c: block(n_diag + i, n_diag * qi - 1 - i, 0, None, c), carry)


def _tile_slot(qi, kj, tq, tk):
    n_diag = tq // tk
    return n_diag * (qi * (qi + 1) // 2) + kj


def _attn_fwd(qs, kn, vb, name):
    S, D = qs.shape
    TQ, TK = ATT_TQ, ATT_TK
    heads = [slice(hh * HEAD_DIM, (hh + 1) * HEAD_DIM) for hh in range(HEADS_PER_BLOCK)]

    def body(q_ref, k_ref, v_ref, o_ref, tiles_ref, buf, sems):
        hp, qi = pl.program_id(0), pl.program_id(1)
        after = _key_order(TK, left=False)
        q = [q_ref[:, lanes] for lanes in heads]
        n_blocks = (TQ // TK) * (qi + 1)

        def save(slot, kj):
            return pltpu.make_async_copy(buf.at[slot], tiles_ref.at[hp, _tile_slot(qi, kj, TQ, TK)], sems.at[slot])

        def block(n, kj, r0, valid, carry):
            slot = n % ATT_SLOTS
            rows = pl.ds(pl.multiple_of(kj * TK, TK), TK)
            stays = [_below(r0, c[1]) for c in carry]
            ws, totals = _attn_tiles([_below(r0, qh) for qh in q], [k_ref[rows, lanes] for lanes in heads], stays, valid, after)
            wbs = [w.astype(BF) for w in ws]
            outs = [_dot(wb, v_ref[rows, lanes], NN) for wb, lanes in zip(wbs, heads)]

            @pl.when(n >= ATT_SLOTS)
            def _():
                save(slot, kj).wait()

            for hh, wb in enumerate(wbs):
                if r0:
                    buf[slot, hh, :r0] = jnp.zeros((r0, TK), BF)
                buf[slot, hh, r0:] = wb
            save(slot, kj).start()
            return tuple((_with_below(r0, c[0], _below(r0, c[0]) + o), _with_below(r0, c[1], stay - t))
                         for c, o, stay, t in zip(carry, outs, stays, totals))

        carry = tuple((jnp.zeros((TQ, HEAD_DIM), F32), jnp.zeros((TQ, 1), F32)) for _ in heads)
        carry = _attn_walk(qi, TQ, TK, block, carry)
        for slot in range(ATT_SLOTS):
            @pl.when(slot < n_blocks)
            def _():
                save(slot, 0).wait()
        for hh, lanes in enumerate(heads):
            o_ref[:, lanes] = carry[hh][0].astype(BF)

    tile = pl.BlockSpec((TQ, LANES), lambda h, m: (m, h))
    full = pl.BlockSpec((S, LANES), lambda h, m: (0, h))
    n_tiles = _tile_slot(S // TQ, 0, TQ, TK)
    return pl.pallas_call(
        body, grid=(D // LANES, S // TQ), in_specs=[tile, full, full], out_specs=[tile, ANY],
        out_shape=[jax.ShapeDtypeStruct((S, D), BF), jax.ShapeDtypeStruct((D // LANES, n_tiles, HEADS_PER_BLOCK, TQ, TK), BF)],
        scratch_shapes=[pltpu.VMEM((ATT_SLOTS, HEADS_PER_BLOCK, TQ, TK), BF), pltpu.SemaphoreType.DMA((ATT_SLOTS,))],
        compiler_params=_params(2), name=name)(qs, kn, vb)


def _attn_bwd(qs, kn, vb, do, tiles, name):
    S, D = qs.shape
    TQ, TK = ATT_TQ, ATT_TK
    heads = [slice(hh * HEAD_DIM, (hh + 1) * HEAD_DIM) for hh in range(HEADS_PER_BLOCK)]
    n_diag = TQ // TK

    def body(q_ref, k_ref, v_ref, do_ref, tiles_ref, dq_ref, dk_ref, dv_ref, buf, sems):
        hp, qi = pl.program_id(0), pl.program_id(1)

        @pl.when(qi == 0)
        def _():
            dk_ref[...] = jnp.zeros_like(dk_ref)
            dv_ref[...] = jnp.zeros_like(dv_ref)

        before = _key_order(TK, left=True)
        q = [q_ref[:, lanes] for lanes in heads]
        dout = [do_ref[:, lanes] for lanes in heads]
        n_rest = n_diag * qi
        n_blocks = n_rest + n_diag

        def fetch(slot, kj):
            return pltpu.make_async_copy(tiles_ref.at[hp, _tile_slot(qi, kj, TQ, TK)], buf.at[slot], sems.at[slot])

        for slot in range(ATT_SLOTS):
            @pl.when(slot < n_blocks)
            def _():
                fetch(slot, slot).start()

        def block(kj, r0, valid, carry):
            slot = kj % ATT_SLOTS
            rows = pl.ds(pl.multiple_of(kj * TK, TK), TK)
            ks = [k_ref[rows, lanes] for lanes in heads]
            qs_, douts = [_below(r0, qh) for qh in q], [_below(r0, d) for d in dout]
            betas = [0.5 * jnp.tanh(0.5 * _dot(qh, k, NT)) + 0.5 for qh, k in zip(qs_, ks)]
            das = [_dot(d, v_ref[rows, lanes], NT) for d, lanes in zip(douts, heads)]
            fetch(slot, kj).wait()
            wbs = [buf[slot, hh, r0:] for hh in range(HEADS_PER_BLOCK)]
            gs = [wb.astype(F32) * da for wb, da in zip(wbs, das)]
            lefts = [_dot(g.astype(BF), before, NN) for g in gs]
            dzs = [g - beta * (g + (_below(r0, c[1]) + left)) for g, beta, c, left in zip(gs, betas, carry, lefts)]
            if valid is not None:
                dzs = [jnp.where(valid, dz, 0.0) for dz in dzs]
            dzbs = [dz.astype(BF) for dz in dzs]
            for hh, lanes in enumerate(heads):
                dk_ref[rows, lanes] += _dot(dzbs[hh], qs_[hh], TN)
                dv_ref[rows, lanes] += _dot(wbs[hh], douts[hh], TN)
            new = tuple((_with_below(r0, c[0], _below(r0, c[0]) + _dot(dzb, k, NN)),
                         _with_below(r0, c[1], _below(r0, c[1]) + jnp.sum(g, axis=1, keepdims=True)))
                        for c, dzb, k, g in zip(carry, dzbs, ks, gs))

            @pl.when(kj + ATT_SLOTS < n_blocks)
            def _():
                fetch(slot, kj + ATT_SLOTS).start()

            return new

        carry = tuple((jnp.zeros((TQ, HEAD_DIM), F32), jnp.zeros((TQ, 1), F32)) for _ in heads)
        carry = lax.fori_loop(0, n_rest, lambda i, c: block(i, 0, None, c), carry)
        for d in range(n_diag):
            carry = block(n_rest + d, d * TK, _diagonal_mask(TQ, TK, d * TK), carry)
        for hh, lanes in enumerate(heads):
            dq_ref[:, lanes] = carry[hh][0]

    tile = pl.BlockSpec((TQ, LANES), lambda h, m: (m, h))
    full = pl.BlockSpec((S, LANES), lambda h, m: (0, h))
    out = jax.ShapeDtypeStruct((S, D), F32)
    return pl.pallas_call(
        body, grid=(D // LANES, S // TQ), in_specs=[tile, full, full, tile, ANY], out_specs=[tile, full, full],
        out_shape=[out, out, out],
        scratch_shapes=[pltpu.VMEM((ATT_SLOTS, HEADS_PER_BLOCK, TQ, TK), BF), pltpu.SemaphoreType.DMA((ATT_SLOTS,))],
        compiler_params=_params(2), name=name)(qs, kn, vb, do, tiles)


def _pool_counts(T, first_row):
    pos = first_row + lax.broadcasted_iota(jnp.int32, (T, 1), 0)
    return [jnp.minimum(pos + 1, w).astype(F32) for w in POOL_WINDOWS]


def _pool_fwd(u, wgrp, scale, x, tm, name):
    S, D = u.shape
    G = len(POOL_WINDOWS)
    C = D // G
    H = POOL_HALO

    def body(u_ref, prev_ref, w_ref, s_ref, x_ref, xo_ref, pooled_ref):
        m = pl.program_id(0)
        prev = jnp.where(m == 0, 0.0, prev_ref[...])
        ext = jnp.concatenate([prev, u_ref[...]], axis=0)
        counts = _pool_counts(tm, m * tm)
        ys = []
        acc = ext
        shift = 1
        for gi, w in enumerate(POOL_WINDOWS):
            while shift < w:
                acc = acc + pltpu.roll(acc, shift, axis=0)
                shift *= 2
            cols = slice(gi * C, (gi + 1) * C)
            pooled = (acc[H:, cols] / counts[gi] - ext[H:, cols]).astype(BF)
            pooled_ref[:, cols] = pooled
            ys.append(_dot(pooled, w_ref[gi], NN))
        xo_ref[...] = x_ref[...] + jnp.concatenate(ys, axis=1) * s_ref[...]

    row = pl.BlockSpec((tm, D), lambda m: (m, 0))
    return pl.pallas_call(
        body, grid=(S // tm,),
        in_specs=[row, pl.BlockSpec((H, D), lambda m: (jnp.maximum(m * (tm // H) - 1, 0), 0)),
                  pl.BlockSpec((G, C, C), lambda m: (0, 0, 0)), pl.BlockSpec((1, D), lambda m: (0, 0)), row],
        out_specs=[row, row], out_shape=[jax.ShapeDtypeStruct((S, D), F32), jax.ShapeDtypeStruct((S, D), BF)],
        compiler_params=_params(1), name=name)(u, u, wgrp, scale, x)


def _pool_bwd_grp(dx, pooled, wgrp, scale, tm, name):
    S, D = dx.shape
    G = len(POOL_WINDOWS)
    C = D // G

    def body(dx_ref, pooled_ref, w_ref, s_ref, dp_ref, dw_ref, ds_ref, dw_acc):
        m = pl.program_id(0)

        @pl.when(m == 0)
        def _():
            dw_acc[...] = jnp.zeros_like(dw_acc)
            ds_ref[...] = jnp.zeros_like(ds_ref)

        dxv = dx_ref[...]
        dy = (dxv * s_ref[...]).astype(BF)
        ys = []
        for gi in range(G):
            cols = slice(gi * C, (gi + 1) * C)
            pg = pooled_ref[:, cols]
            ys.append(_dot(pg, w_ref[gi], NN))
            dw_acc[gi] += _dot(pg, dy[:, cols], TN)
            dp_ref[:, cols] = _dot(dy[:, cols], w_ref[gi], NT)
        ds_ref[...] += jnp.sum(dxv * jnp.concatenate(ys, axis=1), axis=0, keepdims=True)

        @pl.when(m == S // tm - 1)
        def _():
            dw_ref[...] = dw_acc[...].astype(BF)

    row = pl.BlockSpec((tm, D), lambda m: (m, 0))
    wspec = pl.BlockSpec((G, C, C), lambda m: (0, 0, 0))
    vec = pl.BlockSpec((1, D), lambda m: (0, 0))
    return pl.pallas_call(
        body, grid=(S // tm,), in_specs=[row, row, wspec, vec], out_specs=[row, wspec, vec],
        out_shape=[jax.ShapeDtypeStruct((S, D), F32), jax.ShapeDtypeStruct((G, C, C), BF), jax.ShapeDtypeStruct((1, D), F32)],
        scratch_shapes=[pltpu.VMEM((G, C, C), F32)], compiler_params=_params(1), name=name)(dx, pooled, wgrp, scale)


def _pool_bwd_window(dp, tm, name):
    S, D = dp.shape
    G = len(POOL_WINDOWS)
    C = D // G
    H = POOL_HALO
    last = S // tm - 1

    def body(dp_ref, next_ref, du_ref):
        m = pl.program_id(0)
        counts = _pool_counts(tm + H, m * tm)
        nxt = jnp.where(m == last, 0.0, next_ref[...])
        ext = jnp.concatenate([dp_ref[...], nxt], axis=0)
        for gi, w in enumerate(POOL_WINDOWS):
            cols = slice(gi * C, (gi + 1) * C)
            acc = ext[:, cols] / counts[gi]
            shift = 1
            while shift < w:
                acc = acc + pltpu.roll(acc, tm + H - shift, axis=0)
                shift *= 2
            du_ref[:, cols] = (acc[:tm] - ext[:tm, cols]).astype(BF)

    row = pl.BlockSpec((tm, D), lambda m: (m, 0))
    return pl.pallas_call(
        body, grid=(S // tm,),
        in_specs=[row, pl.BlockSpec((H, D), lambda m: (jnp.minimum((m + 1) * (tm // H), S // H - 1), 0))],
        out_specs=row, out_shape=jax.ShapeDtypeStruct((S, D), BF), compiler_params=_params(1), name=name)(dp, dp)


ELEMENTWISE_TILE_BYTES = 1 << 20


def _row_tile(rows, row_bytes):
    for cand in (512, 256, 128, 64, 32, 16, 8):
        if rows % cand == 0 and cand * row_bytes <= ELEMENTWISE_TILE_BYTES:
            return cand
    return rows


def _adamw(w, g, m, v, name):
    shape = w.shape
    cols = shape[-1]
    rows = w.size // cols
    tr = _row_tile(rows, cols * 4)

    def body(w_ref, g_ref, m_ref, v_ref, d_ref, mo_ref, vo_ref):
        gv = g_ref[...]
        mn = ADAM_B1 * m_ref[...] + (1.0 - ADAM_B1) * gv
        vn = ADAM_B2 * v_ref[...] + (1.0 - ADAM_B2) * jnp.square(gv)
        m_hat = mn / (1.0 - ADAM_B1 ** ADAM_STEP)
        v_hat = vn / (1.0 - ADAM_B2 ** ADAM_STEP)
        d_ref[...] = -ADAM_LR * (m_hat / (jnp.sqrt(v_hat) + ADAM_EPS) + ADAM_WD * w_ref[...])
        mo_ref[...] = mn
        vo_ref[...] = vn

    tile = pl.BlockSpec((tr, cols), lambda i: (i, 0))
    out = jax.ShapeDtypeStruct((rows, cols), F32)
    res = pl.pallas_call(
        body, grid=(rows // tr,), in_specs=[tile] * 4, out_specs=[tile] * 3, out_shape=[out] * 3,
        compiler_params=_params(1), name=name)(*[t.reshape(rows, cols) for t in (w, g, m, v)])
    return [t.reshape(shape) for t in res]


def _sum_slots(r, name):
    n = r.shape[0]
    shape = r.shape[1:]
    cols = shape[-1]
    rows = r.size // (n * cols)
    tr = _row_tile(rows, n * cols * r.dtype.itemsize)

    def body(r_ref, o_ref):
        acc = r_ref[0].astype(F32)
        for d in range(1, n):
            acc = acc + r_ref[d].astype(F32)
        o_ref[...] = acc

    return pl.pallas_call(
        body, grid=(rows // tr,), in_specs=[pl.BlockSpec((n, tr, cols), lambda i: (0, i, 0))],
        out_specs=pl.BlockSpec((tr, cols), lambda i: (i, 0)), out_shape=jax.ShapeDtypeStruct((rows, cols), F32),
        compiler_params=_params(1), name=name)(r.reshape(n, rows, cols)).reshape(shape)


def _add_pair(a, b, name):
    shape = a.shape
    cols = shape[-1]
    rows = a.size // cols
    tr = _row_tile(rows, cols * 4)

    def body(a_ref, b_ref, o_ref):
        o_ref[...] = (a_ref[...].astype(F32) + b_ref[...].astype(F32)).astype(BF)

    tile = pl.BlockSpec((tr, cols), lambda i: (i, 0))
    return pl.pallas_call(
        body, grid=(rows // tr,), in_specs=[tile, tile], out_specs=tile, out_shape=jax.ShapeDtypeStruct((rows, cols), BF),
        compiler_params=_params(1), name=name)(a.reshape(rows, cols), b.reshape(rows, cols)).reshape(shape)


ANY = pl.BlockSpec(memory_space=pl.ANY)


def _position():
    return lax.axis_index("x"), lax.axis_index("y"), lax.axis_index("c")


def _gather_small(t, name):
    rows, cols = t.shape

    def body(t_ref, o_ref, send_sems, recv_sems):
        x, y, c = _position()
        me = 4 * x + 2 * y + c
        o_ref[me] = t_ref[...]
        copies = []
        for k in range(1, N_DEV):
            peer = (x ^ (k >> 2), y ^ ((k >> 1) & 1), c ^ (k & 1))
            cp = pltpu.make_async_remote_copy(src_ref=t_ref, dst_ref=o_ref.at[me], send_sem=send_sems.at[k - 1],
                                              recv_sem=recv_sems.at[k - 1], device_id=peer, device_id_type=MESH)
            cp.start()
            copies.append((cp, 4 * peer[0] + 2 * peer[1] + peer[2]))
        for k, (cp, src) in enumerate(copies):
            pltpu.make_async_remote_copy(src_ref=t_ref, dst_ref=o_ref.at[src], send_sem=send_sems.at[k], recv_sem=recv_sems.at[k],
                                         device_id=(x, y, c), device_id_type=MESH).wait_recv()
        for cp, _ in copies:
            cp.wait_send()

    return pl.pallas_call(
        body, in_specs=[pl.BlockSpec(memory_space=pltpu.VMEM)], out_specs=pl.BlockSpec(memory_space=pltpu.VMEM),
        out_shape=jax.ShapeDtypeStruct((N_DEV, rows, cols), F32),
        scratch_shapes=[pltpu.SemaphoreType.DMA((N_DEV - 1,)), pltpu.SemaphoreType.DMA((N_DEV - 1,))], name=name)(t)


def _other_chips(x, y):
    return [(1 - x, y), (x, 1 - y), (1 - x, 1 - y)]


def _remote(src, dst, send_sems, recv_sems, k, to):
    return pltpu.make_async_remote_copy(src_ref=src, dst_ref=dst, send_sem=send_sems.at[k], recv_sem=recv_sems.at[k],
                                        device_id=to, device_id_type=MESH)


def _gather_weights(shards, name):
    n = len(shards)

    def body(*refs):
        ins, outs = refs[:n], refs[n:2 * n]
        ici_send, ici_recv, d2d_send, d2d_recv = refs[2 * n:]
        x, y, c = _position()
        me, sibling, chip = (x, y, c), (x, y, 1 - c), 2 * x + y
        chips = _other_chips(x, y)
        halves = [s.shape[0] // 2 for s in shards]
        sends = []
        for p in range(n):
            mine = pl.ds(c * halves[p], halves[p])
            for j, (px, py) in enumerate(chips):
                sends.append(_remote(ins[p].at[mine], outs[p].at[mine, chip], ici_send, ici_recv, (p, j), (px, py, c)))
                sends[-1].start()
        for p in range(n):
            mine = pl.ds(c * halves[p], halves[p])
            for j, (px, py) in enumerate(chips):
                landed = outs[p].at[mine, 2 * px + py]
                _remote(ins[p].at[mine], landed, ici_send, ici_recv, (p, j), me).wait_recv()
                sends.append(_remote(landed, landed, d2d_send, d2d_recv, (p, j), sibling))
                sends[-1].start()
        for p in range(n):
            theirs = pl.ds((1 - c) * halves[p], halves[p])
            for j, (px, py) in enumerate(chips):
                passed = outs[p].at[theirs, 2 * px + py]
                _remote(passed, passed, d2d_send, d2d_recv, (p, j), me).wait_recv()
        for cp in sends:
            cp.wait_send()

    return pl.pallas_call(
        body, in_specs=[ANY] * n, out_specs=[ANY] * n,
        out_shape=[jax.ShapeDtypeStruct((s.shape[0], N_CHIPS) + s.shape[1:], s.dtype) for s in shards],
        scratch_shapes=[pltpu.SemaphoreType.DMA((n, 3))] * 4, name=name)(*shards)


def _swap_partials(grads, name):
    n = len(grads)

    def body(*refs):
        ins, outs = refs[:n], refs[n:2 * n]
        send_sems, recv_sems = refs[2 * n:]
        x, y, c = _position()
        copies = []
        for p in range(n):
            half = grads[p].shape[0] // 2
            copies.append(_remote(ins[p].at[pl.ds((1 - c) * half, half)], outs[p], send_sems, recv_sems, p, (x, y, 1 - c)))
            copies[-1].start()
        for cp in copies:
            cp.wait()

    return pl.pallas_call(
        body, in_specs=[ANY] * n, out_specs=[ANY] * n,
        out_shape=[jax.ShapeDtypeStruct((g.shape[0] // 2,) + g.shape[1:], g.dtype) for g in grads],
        scratch_shapes=[pltpu.SemaphoreType.DMA((n,))] * 2, name=name)(*grads)


def _scatter_sums(sums, name):
    n = len(sums)

    def body(*refs):
        ins, outs = refs[:n], refs[n:2 * n]
        send_sems, recv_sems = refs[2 * n:]
        x, y, c = _position()
        chip = 2 * x + y
        chips = _other_chips(x, y)
        sends = []
        for p in range(n):
            layers = pl.ds(0, sums[p].shape[0])
            for j, (px, py) in enumerate(chips):
                sends.append(_remote(ins[p].at[layers, 2 * px + py], outs[p].at[chip], send_sems, recv_sems, (p, j), (px, py, c)))
                sends[-1].start()
        for p in range(n):
            layers = pl.ds(0, sums[p].shape[0])
            for j, (px, py) in enumerate(chips):
                _remote(ins[p].at[layers, chip], outs[p].at[2 * px + py], send_sems, recv_sems, (p, j), (x, y, c)).wait_recv()
        for cp in sends:
            cp.wait_send()

    return pl.pallas_call(
        body, in_specs=[ANY] * n, out_specs=[ANY] * n,
        out_shape=[jax.ShapeDtypeStruct((N_CHIPS, s.shape[0]) + s.shape[2:], s.dtype) for s in sums],
        scratch_shapes=[pltpu.SemaphoreType.DMA((n, 3))] * 2, name=name)(*sums)


def _join_halves(halves, name):
    n = len(halves)
    flat = [h.reshape(-1, h.shape[-1]) for h in halves]

    def body(*refs):
        ins, outs = refs[:n], refs[n:2 * n]
        send_sems, recv_sems = refs[2 * n:]
        x, y, c = _position()
        copies = []
        for p in range(n):
            cp = pltpu.make_async_remote_copy(src_ref=ins[p], dst_ref=outs[p], send_sem=send_sems.at[p], recv_sem=recv_sems.at[p],
                                              device_id=(x, y, 1 - c), device_id_type=MESH)
            cp.start()
            copies.append(cp)
        for cp in copies:
            cp.wait()

    theirs = pl.pallas_call(
        body, in_specs=[ANY] * n, out_specs=[ANY] * n, out_shape=[jax.ShapeDtypeStruct(f.shape, f.dtype) for f in flat],
        scratch_shapes=[pltpu.SemaphoreType.DMA((n,)), pltpu.SemaphoreType.DMA((n,))], name=name)(*flat)
    south = lax.axis_index("c") == 0
    return [jnp.concatenate([jnp.where(south, h, t.reshape(h.shape)), jnp.where(south, t.reshape(h.shape), h)], axis=0)
            for h, t in zip(halves, theirs)]


TM = 512
TW = 2 * TM


def _row(v):
    return v.reshape(1, -1)


def _ffn_forward(x, gain, wgu, wdown, li, tag):
    h = _rmsnorm(x, gain, TM, f"norm_{tag}")
    g, u, act = _ffn_up(h, wgu, li, TM, f"ffn_up_{tag}")
    xo = _mm_fwd(act, wdown, li, x, 0.5, TM, f"ffn_down_{tag}")
    return xo, (x, h, g, u, act)


def _ffn_backward(dx, saved, gain, wgu, wdown, d_wgu, d_wdown, li, tag):
    x, h, g, u, act = saved
    dgu = _ffn_dact(dx, wdown, li, g, u, 256, f"ffn_dact_{tag}")
    d_wdown = _mm_wgrad(act, dx, 0.5, d_wdown, li, TW, f"ffn_dwdown_{tag}", tn=512)
    d_wgu = _mm_wgrad_cols(h, dgu, d_wgu, li, TW, f"ffn_dwgu_{tag}")
    dx, d_gain = _mm_dx_norm_cols(dgu, wgu, li, x, gain, dx, 256, f"ffn_dx_{tag}")
    return dx, d_gain, d_wgu, d_wdown


STACKED = ("w_ffn1_gu", "w_ffn1_down", "w_qkv", "w_o", "w_pool_in", "w_ffn2_gu", "w_ffn2_down", "w_ple_gate")


def _local_step(x, p, target, W):
    depth = p.shape[0]
    saved = []
    for i in range(depth):
        j = i // 2
        s = {}
        x, s["ffn1"] = _ffn_forward(x, W["norm_ffn1"][i], W["w_ffn1_gu"], W["w_ffn1_down"], i, f"a{i}")
        s["x_mix"] = x
        hm = _rmsnorm(x, W["norm_mix"][i], TM, f"norm_mix{i}")
        s["hm"] = hm
        if i % 2 == 0:
            qkv = _mm_cols(hm, W["w_qkv"], j, F32, TM, f"qkv{i}")
            qs, kn, vb = _qk_norm(qkv, W["q_norm"][j], W["k_norm"][j], TW, f"qk_norm{i}")
            o, tiles = _attn_fwd(qs, kn, vb, f"attn_fwd{i}")
            x = _mm_fwd(o, W["w_o"], j, x, 1.0, TM, f"attn_out{i}")
            s["mix"] = (qkv, qs, kn, vb, o, tiles)
        else:
            u = _mm_plain(hm, W["w_pool_in"], j, F32, TM, f"pool_in{i}")
            x, pooled = _pool_fwd(u, W["w_pool_grp"][j], W["pool_scale"][j], x, TM, f"pool_fwd{i}")
            s["mix"] = (pooled,)
        x, s["ffn2"] = _ffn_forward(x, W["norm_ffn2"][i], W["w_ffn2_gu"], W["w_ffn2_down"], i, f"b{i}")
        s["x_ple"] = x
        x, hp, gp, pe = _ple_fwd(x, W["norm_ple"][i], W["w_ple_gate"], i, p[i], W["w_ple_proj"][i], TM, f"ple_fwd{i}")
        s["ple"] = (hp, gp, pe)
        saved.append(s)

    sq, dx = _loss_head(x, target, TM, "loss_head")
    G = {k: (lax.empty(v.shape, BF) if k in STACKED else [None] * len(v)) for k, v in W.items()}
    for i in reversed(range(depth)):
        j = i // 2
        s = saved[i]
        hp, gp, pe = s["ple"]
        dx, dgp, dpe, G["norm_ple"][i] = _ple_bwd(dx, gp, pe, W["w_ple_gate"], i, s["x_ple"], W["norm_ple"][i], TM, f"ple_bwd{i}")
        G["w_ple_gate"] = _mm_wgrad(hp, dgp, 1.0, G["w_ple_gate"], i, TW, f"ple_dwgate{i}")
        G["w_ple_proj"][i] = _mm_wgrad(p[i], dpe, 1.0, lax.empty((1,) + W["w_ple_proj"][i].shape, BF), 0, TW, f"ple_dwproj{i}")[0]
        dx, G["norm_ffn2"][i], G["w_ffn2_gu"], G["w_ffn2_down"] = _ffn_backward(
            dx, s["ffn2"], W["norm_ffn2"][i], W["w_ffn2_gu"], W["w_ffn2_down"], G["w_ffn2_gu"], G["w_ffn2_down"], i, f"b{i}")
        hm = s["hm"]
        if i % 2 == 0:
            qkv, qs, kn, vb, o, tiles = s["mix"]
            G["w_o"] = _mm_wgrad(o, dx, 1.0, G["w_o"], j, TW, f"attn_dwo{i}")
            do = _mm_plain(dx, W["w_o"], j, BF, TM, f"attn_do{i}", dims=NT)
            dqs, dkn, dv = _attn_bwd(qs, kn, vb, do, tiles, f"attn_bwd{i}")
            dq, dk, dvb, dqg, dkg = _qk_norm_bwd(qkv, dqs, dkn, dv, W["q_norm"][j], W["k_norm"][j], TW, f"qk_norm_bwd{i}")
            dqkv = jnp.concatenate([dq, dk, dvb], axis=1)
            G["q_norm"][j] = dqg[:, :HEAD_DIM] + dqg[:, HEAD_DIM:]
            G["k_norm"][j] = dkg[:, :HEAD_DIM] + dkg[:, HEAD_DIM:]
            G["w_qkv"] = _mm_wgrad_cols(hm, dqkv, G["w_qkv"], j, TW, f"attn_dwqkv{i}")
            dx, G["norm_mix"][i] = _mm_dx_norm_cols(dqkv, W["w_qkv"], j, s["x_mix"], W["norm_mix"][i], dx, TM, f"attn_dx{i}")
        else:
            (pooled,) = s["mix"]
            dp, G["w_pool_grp"][j], G["pool_scale"][j] = _pool_bwd_grp(dx, pooled, W["w_pool_grp"][j], W["pool_scale"][j], TM, f"pool_bwd_grp{i}")
            du = _pool_bwd_window(dp, TM, f"pool_bwd_win{i}")
            G["w_pool_in"] = _mm_wgrad(hm, du, 1.0, G["w_pool_in"], j, TW, f"pool_dwin{i}")
            dx, G["norm_mix"][i] = _mm_dx_norm(du, W["w_pool_in"], j, s["x_mix"], W["norm_mix"][i], dx, TM, f"pool_dx{i}")
        dx, G["norm_ffn1"][i], G["w_ffn1_gu"], G["w_ffn1_down"] = _ffn_backward(
            dx, s["ffn1"], W["norm_ffn1"][i], W["w_ffn1_gu"], W["w_ffn1_down"], G["w_ffn1_gu"], G["w_ffn1_down"], i, f"a{i}")
    return sq, dx, G


SHARDED = ("w_ffn1_gu", "w_ffn1_down", "w_qkv", "w_o", "w_pool_in", "w_pool_grp", "w_ffn2_gu", "w_ffn2_down", "w_ple_gate", "w_ple_proj")
COLUMN_SHARDED = ("w_ffn1_gu", "w_qkv", "w_ffn2_gu", "w_ple_proj")
NORMS = ("norm_ffn1", "norm_mix", "norm_ffn2", "norm_ple")
HEAD_GAINS = ("q_norm", "k_norm")
WEIGHTS = ("norm_ffn1", "w_ffn1_gu", "w_ffn1_down", "norm_mix", "w_qkv", "q_norm", "k_norm", "w_o", "w_pool_in", "w_pool_grp",
           "pool_scale", "norm_ffn2", "w_ffn2_gu", "w_ffn2_down", "norm_ple", "w_ple_gate", "w_ple_proj")
SMALL_ROWS = 24


def _whole_weights(w, gathered, pool_scale_all):
    W = {}
    for k in NORMS:
        W[k] = [_row(w[k][i]) for i in range(w[k].shape[0])]
    for k in HEAD_GAINS:
        W[k] = [_row(jnp.tile(w[k][j], LANES // HEAD_DIM)) for j in range(w[k].shape[0])]
    for k in SHARDED:
        g = gathered[k]
        L = g.shape[0]
        if k == "w_pool_grp":
            W[k] = [jnp.transpose(g[i], (1, 0, 2, 3)).reshape(g.shape[2], -1, g.shape[4]) for i in range(L)]
        elif k == "w_ple_proj":
            W[k] = [jnp.transpose(g[i], (1, 0, 2)).reshape(g.shape[2], -1) for i in range(L)]
        elif k in COLUMN_SHARDED:
            W[k] = g
        else:
            W[k] = g.reshape(L, -1, g.shape[3])
    W["pool_scale"] = [_row(pool_scale_all[j]) for j in range(pool_scale_all.shape[0])]
    return W


def _shard_major(k, g):
    if k == "w_pool_grp":
        return jnp.stack([jnp.transpose(t.reshape(t.shape[0], N_CHIPS, t.shape[1] // N_CHIPS, t.shape[2]), (1, 0, 2, 3)) for t in g])
    if k == "w_ple_proj":
        return jnp.stack([jnp.transpose(t.reshape(t.shape[0], N_CHIPS, t.shape[1] // N_CHIPS), (1, 0, 2)) for t in g])
    if k in COLUMN_SHARDED:
        return g
    return g.reshape(g.shape[0], N_CHIPS, g.shape[1] // N_CHIPS, g.shape[2])


def kernel(x, p, norm_ffn1, w_ffn1_gu, w_ffn1_down, norm_mix, w_qkv, q_norm, k_norm, w_o, w_pool_in, w_pool_grp, pool_scale, norm_ffn2, w_ffn2_gu, w_ffn2_down, norm_ple, w_ple_gate, w_ple_proj, loss_target, m_norm_ffn1, m_w_ffn1_gu, m_w_ffn1_down, m_norm_mix, m_w_qkv, m_q_norm, m_k_norm, m_w_o, m_w_pool_in, m_w_pool_grp, m_pool_scale, m_norm_ffn2, m_w_ffn2_gu, m_w_ffn2_down, m_norm_ple, m_w_ple_gate, m_w_ple_proj, v_norm_ffn1, v_w_ffn1_gu, v_w_ffn1_down, v_norm_mix, v_w_qkv, v_q_norm, v_k_norm, v_w_o, v_w_pool_in, v_w_pool_grp, v_pool_scale, v_norm_ffn2, v_w_ffn2_gu, v_w_ffn2_down, v_norm_ple, v_w_ple_gate, v_w_ple_proj):
    w = dict(norm_ffn1=norm_ffn1, w_ffn1_gu=w_ffn1_gu, w_ffn1_down=w_ffn1_down, norm_mix=norm_mix, w_qkv=w_qkv, q_norm=q_norm,
             k_norm=k_norm, w_o=w_o, w_pool_in=w_pool_in, w_pool_grp=w_pool_grp, pool_scale=pool_scale, norm_ffn2=norm_ffn2,
             w_ffn2_gu=w_ffn2_gu, w_ffn2_down=w_ffn2_down, norm_ple=norm_ple, w_ple_gate=w_ple_gate, w_ple_proj=w_ple_proj)
    m = dict(norm_ffn1=m_norm_ffn1, w_ffn1_gu=m_w_ffn1_gu, w_ffn1_down=m_w_ffn1_down, norm_mix=m_norm_mix, w_qkv=m_w_qkv,
             q_norm=m_q_norm, k_norm=m_k_norm, w_o=m_w_o, w_pool_in=m_w_pool_in, w_pool_grp=m_w_pool_grp, pool_scale=m_pool_scale,
             norm_ffn2=m_norm_ffn2, w_ffn2_gu=m_w_ffn2_gu, w_ffn2_down=m_w_ffn2_down, norm_ple=m_norm_ple, w_ple_gate=m_w_ple_gate,
             w_ple_proj=m_w_ple_proj)
    v = dict(norm_ffn1=v_norm_ffn1, w_ffn1_gu=v_w_ffn1_gu, w_ffn1_down=v_w_ffn1_down, norm_mix=v_norm_mix, w_qkv=v_w_qkv,
             q_norm=v_q_norm, k_norm=v_k_norm, w_o=v_w_o, w_pool_in=v_w_pool_in, w_pool_grp=v_w_pool_grp, pool_scale=v_pool_scale,
             norm_ffn2=v_norm_ffn2, w_ffn2_gu=v_w_ffn2_gu, w_ffn2_down=v_w_ffn2_down, norm_ple=v_norm_ple, w_ple_gate=v_w_ple_gate,
             w_ple_proj=v_w_ple_proj)
    chip = 2 * lax.axis_index("x") + lax.axis_index("y")
    D = x.shape[-1]
    shard_cols = pool_scale.shape[1]

    core = lax.axis_index("c")
    shards = [w[k].astype(BF) for k in SHARDED]
    gathered = {k: lax.dynamic_update_slice_in_dim(g, s[:, None], chip, axis=1)
                for k, s, g in zip(SHARDED, shards, _gather_weights(shards, "gather_weights"))}
    scale_rows = jnp.zeros((8, shard_cols), F32).at[:pool_scale.shape[0]].set(pool_scale)
    scale_all = _gather_small(scale_rows, "gather_pool_scale")
    pool_scale_all = jnp.transpose(scale_all[::2, :pool_scale.shape[0]], (1, 0, 2)).reshape(pool_scale.shape[0], D)
    W = _whole_weights(w, gathered, pool_scale_all)

    sq, dx, G = _local_step(x[0], p[:, 0], loss_target[0], W)
    loss = lax.psum(0.5 / D * sq[0, 0], ("x", "y", "c"))

    partials = [_shard_major(k, G[k]) for k in SHARDED]
    theirs = _swap_partials(partials, "swap_partials")
    chip_sums = [_add_pair(lax.dynamic_slice_in_dim(g, core * t.shape[0], t.shape[0], axis=0), t, f"pair_{k}")
                 for k, g, t in zip(SHARDED, partials, theirs)]
    slots = [lax.dynamic_update_slice_in_dim(r, lax.dynamic_index_in_dim(s, chip, axis=1, keepdims=False)[None], chip, axis=0)
             for s, r in zip(chip_sums, _scatter_sums(chip_sums, "scatter_sums"))]
    halves = [_sum_slots(s, f"sum_{k}") for k, s in zip(SHARDED, slots)]
    grads = dict(zip(SHARDED, _join_halves(halves, "join_halves")))

    small = [G[k][i] for k in NORMS for i in range(len(G[k]))]
    small += [jnp.pad(jnp.concatenate(G[k], axis=1), ((0, 0), (0, D - len(G[k]) * HEAD_DIM))) for k in HEAD_GAINS]
    small += G["pool_scale"]
    small = jnp.concatenate(small + [jnp.zeros((SMALL_ROWS - len(small), D), F32)], axis=0)
    small = _sum_slots(_gather_small(small, "gather_small_grads"), "sum_small_grads")
    row = 0
    for k in NORMS:
        grads[k] = small[row:row + w[k].shape[0]]
        row += w[k].shape[0]
    for k in HEAD_GAINS:
        grads[k] = small[row, :w[k].size].reshape(w[k].shape)
        row += 1
    grads["pool_scale"] = lax.dynamic_slice_in_dim(small[row:row + pool_scale.shape[0]], chip * shard_cols, shard_cols, axis=1)

    delta, new_m, new_v = {}, {}, {}
    small_names = NORMS + HEAD_GAINS + ("pool_scale",)

    def pack(d):
        rows = [jnp.pad(d[k].reshape(-1, d[k].shape[-1]) if k in NORMS + ("pool_scale",) else d[k].reshape(1, -1),
                        ((0, 0), (0, D - (d[k].shape[-1] if k in NORMS + ("pool_scale",) else d[k].size))), constant_values=1.0)
                for k in small_names]
        n = sum(r.shape[0] for r in rows)
        return jnp.concatenate(rows + [jnp.ones((SMALL_ROWS - n, D), F32)], axis=0)

    packed = _adamw(pack(w), pack(grads), pack(m), pack(v), "adamw_small")
    row = 0
    for k in small_names:
        n = w[k].shape[0] if k in NORMS + ("pool_scale",) else 1
        width = w[k].shape[-1] if k in NORMS + ("pool_scale",) else w[k].size
        for dst, src in zip((delta, new_m, new_v), packed):
            dst[k] = src[row:row + n, :width].reshape(w[k].shape)
        row += n
    for k in SHARDED:
        delta[k], new_m[k], new_v[k] = _adamw(w[k], grads[k], m[k], v[k], f"adamw_{k}")

    return (loss, dx[None], *[grads[k] for k in WEIGHTS], *[delta[k] for k in WEIGHTS],
            *[new_m[k] for k in WEIGHTS], *[new_v[k] for k in WEIGHTS])
```
